```python
import jax, jax.numpy as jnp
from jax import lax
import numpy as np

D_MODEL = 1024
BATCH = 8
SEQ = 2048
DEPTH = 1

GRID_W = 64
CTX_LEN = 256
N_HEADS = 8
QK_NOPE_DIM = 64
QK_ROPE_DIM = 32
V_HEAD_DIM = 64
Q_LORA_RANK = 384
KV_LORA_RANK = 256
ROPE_THETA = 10000.0
CONV_DIM = 512
CONV_WIDTH = 3
D_FF = 2816
Q_BLOCK = 128
EPS = 1e-6
ATTN_DIM = N_HEADS * V_HEAD_DIM
KV_END = KV_LORA_RANK
KR_END = KV_END + QK_ROPE_DIM
Q_END = KR_END + Q_LORA_RANK
CX_END = Q_END + CONV_DIM
CB_END = CX_END + CONV_DIM
CC_END = CB_END + CONV_DIM
GA_END = CC_END + D_MODEL
GC_END = GA_END + D_MODEL
IN_COLS = GC_END

kernel_name = 'hybrid_mla_shortconv_convffn_dit_block'


def rmsnorm(x, g):
    xf = x.astype(jnp.float32)
    y = xf * lax.rsqrt(jnp.mean(xf * xf, axis=-1, keepdims=True) + EPS)
    return (y * g.astype(jnp.float32)).astype(x.dtype)


def modulate(h, shift, scale):
    return h * (1 + scale[:, None, :]) + shift[:, None, :]


def dwconv(x, w, b):
    T = x.shape[1]
    pad = CONV_WIDTH // 2
    xp = jnp.pad(x, ((0, 0), (pad, CONV_WIDTH - 1 - pad), (0, 0)))
    y = b
    for k in range(CONV_WIDTH):
        y = y + xp[:, k:k + T] * w[k]
    return y


def axial_angles(rows):
    row = jnp.repeat(jnp.arange(rows), GRID_W).astype(jnp.float32)
    col = jnp.tile(jnp.arange(GRID_W), rows).astype(jnp.float32)
    axis_dim = QK_ROPE_DIM // 2
    inv = ROPE_THETA ** (-jnp.arange(0, axis_dim, 2, dtype=jnp.float32) / axis_dim)
    return jnp.concatenate([row[:, None] * inv, col[:, None] * inv], axis=-1)


def rope2d(x, ang):
    nf = QK_ROPE_DIM // 4
    xr = x.reshape(*x.shape[:-1], 2, 2, nf)
    x1, x2 = xr[..., 0, :], xr[..., 1, :]
    a = ang.reshape(ang.shape[0], 1, 2, nf)
    cos, sin = jnp.cos(a).astype(x.dtype), jnp.sin(a).astype(x.dtype)
    out = jnp.stack([x1 * cos - x2 * sin, x1 * sin + x2 * cos], axis=-2)
    return out.reshape(x.shape)


def mla_kv(p, kv_norm_g, w_ukv, ang):
    B, T = p.shape[:2]
    ckv = rmsnorm(p[..., :KV_END], kv_norm_g)
    kv = (ckv @ w_ukv).reshape(B, T, N_HEADS, QK_NOPE_DIM + V_HEAD_DIM)
    k_nope, v = kv[..., :QK_NOPE_DIM], kv[..., QK_NOPE_DIM:]
    k_rope = p[..., KV_END:KR_END][:, :, None, :]
    if ang is not None:
        k_rope = rope2d(k_rope, ang)
    return k_nope, k_rope[:, :, 0], v


def mla_query(p, q_norm_g, w_uq, ang):
    B, T = p.shape[:2]
    cq = rmsnorm(p[..., KR_END:Q_END], q_norm_g)
    q = (cq @ w_uq).reshape(B, T, N_HEADS, QK_NOPE_DIM + QK_ROPE_DIM)
    q_nope, q_rope = q[..., :QK_NOPE_DIM], q[..., QK_NOPE_DIM:]
    if ang is not None:
        q_rope = rope2d(q_rope, ang)
    return q_nope, q_rope


def mla_attend(q_nope, q_rope, k_nope, k_rope, v):
    scale = (QK_NOPE_DIM + QK_ROPE_DIM) ** -0.5
    s = (jnp.einsum('bqhd,bkhd->bhqk', q_nope, k_nope)
         + jnp.einsum('bqhr,bkr->bhqk', q_rope, k_rope))
    pr = jax.nn.softmax(s.astype(jnp.float32) * scale, axis=-1).astype(v.dtype)
    return jnp.einsum('bhqk,bkhd->bqhd', pr, v)


def mla_latent(q_nope, q_rope, k_nope, k_rope, v):
    B, T = q_nope.shape[:2]
    nb = T // Q_BLOCK

    def blocks(a):
        return a.reshape(B, nb, Q_BLOCK, *a.shape[2:]).swapaxes(0, 1)

    o = lax.map(lambda qs: mla_attend(qs[0], qs[1], k_nope, k_rope, v), (blocks(q_nope), blocks(q_rope)))
    return o.swapaxes(0, 1).reshape(B, T, ATTN_DIM)


def gated_merge(p, o_attn, conv_w, conv_b, w_attn_out, w_conv_out, w_o):
    x_in = p[..., Q_END:CX_END]
    b_gate = p[..., CX_END:CB_END]
    c_gate = p[..., CB_END:CC_END]
    y_conv = (b_gate * dwconv(c_gate * x_in, conv_w, conv_b)) @ w_conv_out
    y_attn = o_attn @ w_attn_out
    g_attn = jax.nn.sigmoid(p[..., CC_END:GA_END])
    g_conv = jax.nn.sigmoid(p[..., GA_END:GC_END])
    return (g_attn * y_attn + g_conv * y_conv) @ w_o


def conv_ffn(h, w_up, conv_w, conv_b, w_down):
    u = dwconv(h @ w_up, conv_w, conv_b)
    gate, val = jnp.split(u, 2, axis=-1)
    return (jax.nn.silu(gate) * val) @ w_down


def trunk_layer(x, ctx, mod_lat, mod_ctx, ang, update_ctx, norm1_g, w_in, q_norm_g, kv_norm_g,
                w_uq, w_ukv, conv_w, conv_b, w_attn_out, w_conv_out, w_o, norm2_g, w_up,
                ffn_conv_w, ffn_conv_b, w_down):
    sh1, sc1, g1, sh2, sc2, g2 = jnp.split(mod_lat, 6, axis=-1)
    csh1, csc1, cg1, csh2, csc2, cg2 = jnp.split(mod_ctx, 6, axis=-1)
    h = modulate(rmsnorm(x, norm1_g), sh1, sc1)
    hc = modulate(rmsnorm(ctx, norm1_g), csh1, csc1)
    p = h @ w_in
    pc = hc @ w_in if update_ctx else hc @ w_in[:, :KR_END]
    kn_c, kr_c, v_c = mla_kv(pc, kv_norm_g, w_ukv, None)
    kn_l, kr_l, v_l = mla_kv(p, kv_norm_g, w_ukv, ang)
    qn, qr = mla_query(p, q_norm_g, w_uq, ang)
    o = mla_latent(qn, qr, jnp.concatenate([kn_c, kn_l], axis=1),
                   jnp.concatenate([kr_c, kr_l], axis=1), jnp.concatenate([v_c, v_l], axis=1))
    x = x + g1[:, None, :] * gated_merge(p, o, conv_w, conv_b, w_attn_out, w_conv_out, w_o)
    x = x + g2[:, None, :] * conv_ffn(modulate(rmsnorm(x, norm2_g), sh2, sc2), w_up, ffn_conv_w, ffn_conv_b, w_down)
    if update_ctx:
        qn_c, qr_c = mla_query(pc, q_norm_g, w_uq, None)
        oc = mla_attend(qn_c, qr_c, kn_c, kr_c, v_c).reshape(ctx.shape[0], ctx.shape[1], ATTN_DIM)
        ctx = ctx + cg1[:, None, :] * gated_merge(pc, oc, conv_w, conv_b, w_attn_out, w_conv_out, w_o)
        ctx = ctx + cg2[:, None, :] * conv_ffn(modulate(rmsnorm(ctx, norm2_g), csh2, csc2), w_up, ffn_conv_w, ffn_conv_b, w_down)
    return x, ctx


def setup_inputs(seed: int = 0) -> dict:
    key = jax.random.key(seed)
    ks = jax.random.split(key, 24)
    L, D = DEPTH, D_MODEL

    def nrm(k, shape, fan_in, gain=1.0):
        return jax.random.normal(k, shape, jnp.float32) * (gain * fan_in ** -0.5)

    def norm_gain(k, shape):
        return 1.0 + 0.05 * jax.random.normal(k, shape, jnp.float32)

    def bias(k, shape):
        return 0.02 * jax.random.normal(k, shape, jnp.float32)

    return {
        'x': jax.random.normal(ks[0], (BATCH, SEQ, D), jnp.float32),
        'c': jax.random.normal(ks[1], (BATCH, D), jnp.float32),
        'ctx': jax.random.normal(ks[2], (BATCH, CTX_LEN, D), jnp.float32),
        'c_ctx': jax.random.normal(ks[3], (D,), jnp.float32),
        'w_ada': nrm(ks[4], (L, D, 6 * D), D, 0.5),
        'b_ada': bias(ks[5], (L, 6 * D)),
        'norm1_g': norm_gain(ks[6], (L, D)),
        'w_in': nrm(ks[7], (L, D, IN_COLS), D),
        'q_norm_g': norm_gain(ks[8], (L, Q_LORA_RANK)),
        'kv_norm_g': norm_gain(ks[9], (L, KV_LORA_RANK)),
        'w_uq': nrm(ks[10], (L, Q_LORA_RANK, N_HEADS * (QK_NOPE_DIM + QK_ROPE_DIM)), Q_LORA_RANK),
        'w_ukv': nrm(ks[11], (L, KV_LORA_RANK, N_HEADS * (QK_NOPE_DIM + V_HEAD_DIM)), KV_LORA_RANK),
        'conv_w': nrm(ks[12], (L, CONV_WIDTH, CONV_DIM), CONV_WIDTH),
        'conv_b': bias(ks[13], (L, CONV_DIM)),
        'w_attn_out': nrm(ks[14], (L, ATTN_DIM, D), ATTN_DIM),
        'w_conv_out': nrm(ks[15], (L, CONV_DIM, D), CONV_DIM),
        'w_o': nrm(ks[16], (L, D, D), D),
        'norm2_g': norm_gain(ks[17], (L, D)),
        'w_up': nrm(ks[18], (L, D, 2 * D_FF), D),
        'ffn_conv_w': nrm(ks[19], (L, CONV_WIDTH, 2 * D_FF), CONV_WIDTH),
        'ffn_conv_b': bias(ks[20], (L, 2 * D_FF)),
        'w_down': nrm(ks[21], (L, D_FF, D), D_FF),
        'final_g': norm_gain(ks[22], (D,)),
    }


def reference(x, c, ctx, c_ctx, w_ada, b_ada, norm1_g, w_in, q_norm_g, kv_norm_g, w_uq, w_ukv,
              conv_w, conv_b, w_attn_out, w_conv_out, w_o, norm2_g, w_up, ffn_conv_w, ffn_conv_b,
              w_down, final_g):
    rows = x.shape[1] // GRID_W
    ang = axial_angles(rows)
    sc = jax.nn.silu(c)
    sc_ctx = jax.nn.silu(c_ctx)[None, :]
    for i in range(DEPTH):
        mod_lat = sc @ w_ada[i] + b_ada[i]
        mod_ctx = sc_ctx @ w_ada[i] + b_ada[i]
        x, ctx = trunk_layer(x, ctx, mod_lat, mod_ctx, ang, i < DEPTH - 1, norm1_g[i], w_in[i],
                             q_norm_g[i], kv_norm_g[i], w_uq[i], w_ukv[i], conv_w[i], conv_b[i],
                             w_attn_out[i], w_conv_out[i], w_o[i], norm2_g[i], w_up[i],
                             ffn_conv_w[i], ffn_conv_b[i], w_down[i])
    return rmsnorm(x, final_g)
```

```python
import functools

import jax
import jax.numpy as jnp
from jax import lax
from jax.experimental import pallas as pl
from jax.experimental.pallas import tpu as pltpu

D_MODEL = 1024
BATCH = 8
SEQ = 2048
GRID_W = 64
CTX_LEN = 256
N_HEADS = 8
QK_NOPE_DIM = 64
QK_ROPE_DIM = 32
V_HEAD_DIM = 64
Q_LORA_RANK = 384
KV_LORA_RANK = 256
ROPE_THETA = 10000.0
CONV_DIM = 512
D_FF = 2816
EPS = 1e-6
ATTN_DIM = N_HEADS * V_HEAD_DIM

LANES = 128
HEAD_COLS = N_HEADS * LANES
ROPE_HALF = QK_ROPE_DIM // 4

C_KV = 0
C_KR = C_KV + KV_LORA_RANK
C_Q = C_KR + LANES
C_CX = C_Q + Q_LORA_RANK
C_CB = C_CX + CONV_DIM
C_CC = C_CB + CONV_DIM
C_GA = C_CC + CONV_DIM
C_GC = C_GA + D_MODEL
C_END = C_GC + D_MODEL

VMEM_LIMIT = 56 * 1024 * 1024

BF16 = jnp.bfloat16
F32 = jnp.float32


def _dot(a, b):
    return jnp.dot(a, b, preferred_element_type=F32)


def _dot_nt(a, b):
    return lax.dot_general(a, b, (((1,), (1,)), ((), ())), preferred_element_type=F32)


def _rms(x, g):
    return x * lax.rsqrt(jnp.mean(x * x, axis=-1, keepdims=True) + EPS) * g


def _sigmoid(x):
    return 1.0 / (1.0 + jnp.exp(-x))


def _rope_block(xh, c, s0, s1):
    return (xh * c + pltpu.roll(xh, LANES - ROPE_HALF, 1) * s0
            + pltpu.roll(xh, ROPE_HALF, 1) * s1)


def _shift_rows(x, prev_row, next_row):
    m = x.shape[0]
    row = lax.broadcasted_iota(jnp.int32, x.shape, 0)
    down = jnp.where(row == 0, prev_row, pltpu.roll(x, 1, 0))
    up = jnp.where(row == m - 1, next_row, pltpu.roll(x, m - 1, 0))
    return down, up


def _const_spec(shape):
    nd = len(shape)
    return pl.BlockSpec(shape, lambda *_: (0,) * nd, pipeline_mode=pl.Buffered(1))


def _params(n_grid):
    return pltpu.CompilerParams(dimension_semantics=("arbitrary",) * n_grid,
                                vmem_limit_bytes=VMEM_LIMIT)


def _ada_kernel(c_ref, w_ref, b_ref, o_ref):
    c = c_ref[...]
    s = (c * _sigmoid(c)).astype(BF16)
    o_ref[...] = _dot(s, w_ref[...].astype(BF16)) + b_ref[...]


def _ada(cvec, w_ada, b_ada):
    rows = cvec.shape[0]
    n = w_ada.shape[1]
    tn = 1024
    return pl.pallas_call(
        _ada_kernel,
        grid=(n // tn,),
        in_specs=[pl.BlockSpec((rows, D_MODEL), lambda j: (0, 0)),
                  pl.BlockSpec((D_MODEL, tn), lambda j: (0, j)),
                  pl.BlockSpec((1, tn), lambda j: (0, j))],
        out_specs=pl.BlockSpec((rows, tn), lambda j: (0, j)),
        out_shape=jax.ShapeDtypeStruct((rows, n), F32),
        compiler_params=_params(1),
    )(cvec, w_ada, b_ada)


def _kv_from(pa, kr, kvg_ref, wkv_ref, k_ref, v_ref):
    ckv = _rms(pa[:, C_KV:C_KV + KV_LORA_RANK], kvg_ref[...]).astype(BF16)
    kvv = _dot(ckv, wkv_ref[...])
    for h in range(N_HEADS):
        sl = slice(h * LANES, (h + 1) * LANES)
        k_ref[:, sl] = (kvv[:, sl] + kr).astype(BF16)
    v_ref[...] = kvv[:, HEAD_COLS:].astype(BF16)


def _inproj_kernel(x_ref, xp_ref, xn_ref, mod_ref, tab_ref, g1_ref, win_ref, kvg_ref, qg_ref,
                   wkv_ref, wq_ref, cw_ref, cb_ref, wco_ref,
                   q_ref, k_ref, v_ref, ga_ref, gcy_ref, *, tm):
    i = pl.program_id(0)
    tps = SEQ // tm
    first = (i % tps) == 0
    last = (i % tps) == tps - 1
    sh = mod_ref[0, 0:1, :]
    sc = mod_ref[0, 1:2, :]
    g1 = g1_ref[...]

    def hmod(xv):
        return (_rms(xv, g1) * (1.0 + sc) + sh).astype(BF16)

    hb = hmod(x_ref[...])

    pa = _dot(hb, win_ref[:, C_KV:C_CX])
    krb = pa[:, C_KR:C_KR + LANES]
    kr = _rope_block(krb, tab_ref[3], tab_ref[4], tab_ref[5])
    _kv_from(pa, kr, kvg_ref, wkv_ref, k_ref, v_ref)
    cq = _rms(pa[:, C_Q:C_Q + Q_LORA_RANK], qg_ref[...]).astype(BF16)
    qf = _dot(cq, wq_ref[...])
    ca, sa0, sa1 = tab_ref[0], tab_ref[1], tab_ref[2]
    for h in range(N_HEADS):
        sl = slice(h * LANES, (h + 1) * LANES)
        q_ref[:, sl] = _rope_block(qf[:, sl], ca, sa0, sa1).astype(BF16)

    pc = _dot(hb, win_ref[:, C_CX:C_GA])
    x_in = pc[:, 0:CONV_DIM]
    b_gate = pc[:, CONV_DIM:2 * CONV_DIM]
    c_gate = pc[:, 2 * CONV_DIM:3 * CONV_DIM]
    cx = c_gate * x_in
    hh = hmod(jnp.concatenate([xp_ref[...], xn_ref[...]], axis=0))
    cxh = _dot(hh, win_ref[:, C_CX:C_CB]) * _dot(hh, win_ref[:, C_CC:C_GA])
    prev_row = jnp.where(first, 0.0, cxh[7:8, :])
    next_row = jnp.where(last, 0.0, cxh[8:9, :])
    cx_dn, cx_up = _shift_rows(cx, prev_row, next_row)
    conv = cb_ref[...] + cx_dn * cw_ref[0:1, :] + cx * cw_ref[1:2, :] + cx_up * cw_ref[2:3, :]
    y_conv = _dot((b_gate * conv).astype(BF16), wco_ref[...])

    pg = _dot(hb, win_ref[:, C_GA:C_END])
    ga_ref[...] = _sigmoid(pg[:, :D_MODEL]).astype(BF16)
    gcy_ref[...] = (_sigmoid(pg[:, D_MODEL:]) * y_conv).astype(BF16)


def _inproj(x2, mod_lat, tab, g1, w_in_p, kvg, qg, w_kv_p, w_q_p, cw, cb, wco, tm):
    n = x2.shape[0]
    nt = n // tm
    tps = SEQ // tm
    r8 = tm // 8
    row = lambda i: (i, 0)
    out = jax.ShapeDtypeStruct((n, HEAD_COLS), BF16)
    return pl.pallas_call(
        functools.partial(_inproj_kernel, tm=tm),
        grid=(nt,),
        in_specs=[
            pl.BlockSpec((tm, D_MODEL), row),
            pl.BlockSpec((8, D_MODEL), lambda i: (jnp.maximum(i * r8 - 1, 0), 0)),
            pl.BlockSpec((8, D_MODEL), lambda i: (jnp.minimum((i + 1) * r8, n // 8 - 1), 0)),
            pl.BlockSpec((1, 6, D_MODEL), lambda i: (i // tps, 0, 0)),
            pl.BlockSpec((6, tm, LANES), lambda i: (0, i % tps, 0)),
            _const_spec((1, D_MODEL)),
            _const_spec((D_MODEL, C_END)),
            _const_spec((1, KV_LORA_RANK)),
            _const_spec((1, Q_LORA_RANK)),
            _const_spec((KV_LORA_RANK, 2 * HEAD_COLS)),
            _const_spec((Q_LORA_RANK, HEAD_COLS)),
            _const_spec((3, CONV_DIM)),
            _const_spec((1, CONV_DIM)),
            _const_spec((CONV_DIM, D_MODEL)),
        ],
        out_specs=[pl.BlockSpec((tm, HEAD_COLS), row)] * 5,
        out_shape=[out] * 5,
        compiler_params=_params(1),
    )(x2, x2, x2, mod_lat, tab, g1, w_in_p, kvg, qg, w_kv_p, w_q_p, cw, cb, wco)


def _ctxproj_kernel(x_ref, mod_ref, g1_ref, win_ref, kvg_ref, wkv_ref, k_ref, v_ref):
    sh = mod_ref[0, 0:1, :]
    sc = mod_ref[0, 1:2, :]
    hb = (_rms(x_ref[...], g1_ref[...]) * (1.0 + sc) + sh).astype(BF16)
    pa = _dot(hb, win_ref[:, C_KV:C_Q])
    _kv_from(pa, pa[:, C_KR:C_KR + LANES], kvg_ref, wkv_ref, k_ref, v_ref)


def _ctxproj(c2, mod_ctx, g1, w_in_p, kvg, w_kv_p, tm):
    n = c2.shape[0]
    row = lambda i: (i, 0)
    out = jax.ShapeDtypeStruct((n, HEAD_COLS), BF16)
    return pl.pallas_call(
        _ctxproj_kernel,
        grid=(n // tm,),
        in_specs=[
            pl.BlockSpec((tm, D_MODEL), row),
            _const_spec((1, 6, D_MODEL)),
            _const_spec((1, D_MODEL)),
            pl.BlockSpec((D_MODEL, C_Q), lambda i: (0, 0), pipeline_mode=pl.Buffered(1)),
            _const_spec((1, KV_LORA_RANK)),
            _const_spec((KV_LORA_RANK, 2 * HEAD_COLS)),
        ],
        out_specs=[pl.BlockSpec((tm, HEAD_COLS), row)] * 2,
        out_shape=[out] * 2,
        compiler_params=_params(1),
    )(c2, mod_ctx, g1, w_in_p, kvg, w_kv_p)


def _attn_kernel(q_ref, kl_ref, vl_ref, kc_ref, vc_ref, o_ref):
    for hp in range(N_HEADS // 2):
        pair = None
        for e in range(2):
            h = 2 * hp + e
            sl = slice(h * LANES, (h + 1) * LANES)
            qh = q_ref[:, sl]
            s_c = _dot_nt(qh, kc_ref[:, sl])
            s_l = _dot_nt(qh, kl_ref[:, sl])
            m = jnp.maximum(jnp.max(s_c, axis=-1, keepdims=True),
                            jnp.max(s_l, axis=-1, keepdims=True))
            p_c = jnp.exp(s_c - m)
            p_l = jnp.exp(s_l - m)
            den = jnp.sum(p_c, axis=-1, keepdims=True) + jnp.sum(p_l, axis=-1, keepdims=True)
            acc = _dot(p_c.astype(BF16), vc_ref[:, sl]) + _dot(p_l.astype(BF16), vl_ref[:, sl])
            acc = acc * (1.0 / den)
            pair = acc if pair is None else pair + acc
        o_ref[:, hp * LANES:(hp + 1) * LANES] = pair.astype(BF16)


def _attn(q, kl, vl, kc, vc, tq):
    nq = SEQ // tq
    return pl.pallas_call(
        _attn_kernel,
        grid=(BATCH, nq),
        in_specs=[
            pl.BlockSpec((tq, HEAD_COLS), lambda b, j: (b * nq + j, 0)),
            pl.BlockSpec((SEQ, HEAD_COLS), lambda b, j: (b, 0)),
            pl.BlockSpec((SEQ, HEAD_COLS), lambda b, j: (b, 0)),
            pl.BlockSpec((CTX_LEN, HEAD_COLS), lambda b, j: (b, 0)),
            pl.BlockSpec((CTX_LEN, HEAD_COLS), lambda b, j: (b, 0)),
        ],
        out_specs=pl.BlockSpec((tq, ATTN_DIM), lambda b, j: (b * nq + j, 0)),
        out_shape=jax.ShapeDtypeStruct((BATCH * SEQ, ATTN_DIM), BF16),
        compiler_params=_params(2),
    )(q, kl, vl, kc, vc)


def _merge_kernel(o_ref, ga_ref, gcy_ref, x_ref, mod_ref, wao_ref, wo_ref, g2_ref, x1_ref, h2_ref):
    y_attn = _dot(o_ref[...], wao_ref[...])
    m = (ga_ref[...].astype(F32) * y_attn + gcy_ref[...].astype(F32)).astype(BF16)
    x1 = x_ref[...] + mod_ref[0, 2:3, :] * _dot(m, wo_ref[...])
    x1_ref[...] = x1
    h2 = _rms(x1, g2_ref[...]) * (1.0 + mod_ref[0, 4:5, :]) + mod_ref[0, 3:4, :]
    h2_ref[...] = h2.astype(BF16)


def _merge(o, ga, gcy, x2, mod_lat, wao, wo, g2, tm):
    n = x2.shape[0]
    tps = SEQ // tm
    row = lambda i: (i, 0)
    return pl.pallas_call(
        _merge_kernel,
        grid=(n // tm,),
        in_specs=[
            pl.BlockSpec((tm, ATTN_DIM), row),
            pl.BlockSpec((tm, D_MODEL), row),
            pl.BlockSpec((tm, D_MODEL), row),
            pl.BlockSpec((tm, D_MODEL), row),
            pl.BlockSpec((1, 6, D_MODEL), lambda i: (i // tps, 0, 0)),
            _const_spec((ATTN_DIM, D_MODEL)),
            _const_spec((D_MODEL, D_MODEL)),
            _const_spec((1, D_MODEL)),
        ],
        out_specs=[pl.BlockSpec((tm, D_MODEL), row)] * 2,
        out_shape=[jax.ShapeDtypeStruct((n, D_MODEL), F32),
                   jax.ShapeDtypeStruct((n, D_MODEL), BF16)],
        compiler_params=_params(1),
    )(o, ga, gcy, x2, mod_lat, wao, wo, g2)


def _ffn_kernel(h_ref, hp_ref, hn_ref, x1_ref, mod_ref, wup_ref, fcw_ref, fcb_ref, wdn_ref, gf_ref,
                o_ref, act_ref, *, tm, nc):
    i = pl.program_id(0)
    tps = SEQ // tm
    first = (i % tps) == 0
    last = (i % tps) == tps - 1
    hb = h_ref[...]
    hh = jnp.concatenate([hp_ref[...], hn_ref[...]], axis=0)

    def conv_cols(c0):
        sl = slice(c0, c0 + nc)
        u = _dot(hb, wup_ref[:, sl])
        uh = _dot(hh, wup_ref[:, sl])
        prev_row = jnp.where(first, 0.0, uh[7:8, :])
        next_row = jnp.where(last, 0.0, uh[8:9, :])
        u_dn, u_up = _shift_rows(u, prev_row, next_row)
        return (fcb_ref[:, sl] + u_dn * fcw_ref[0:1, sl] + u * fcw_ref[1:2, sl]
                + u_up * fcw_ref[2:3, sl])

    for c in range(D_FF // nc):
        gate = conv_cols(c * nc)
        val = conv_cols(D_FF + c * nc)
        act_ref[:, c * nc:(c + 1) * nc] = (gate * _sigmoid(gate) * val).astype(BF16)

    y = _dot(act_ref[...], wdn_ref[...])
    x2 = x1_ref[...] + mod_ref[0, 5:6, :] * y
    o_ref[...] = _rms(x2, gf_ref[...])


def _ffn(h2, x1, mod_lat, wup, fcw, fcb, wdn, gf, tm, nc):
    n = x1.shape[0]
    tps = SEQ // tm
    r8 = tm // 8
    row = lambda i: (i, 0)
    return pl.pallas_call(
        functools.partial(_ffn_kernel, tm=tm, nc=nc),
        grid=(n // tm,),
        in_specs=[
            pl.BlockSpec((tm, D_MODEL), row),
            pl.BlockSpec((8, D_MODEL), lambda i: (jnp.maximum(i * r8 - 1, 0), 0)),
            pl.BlockSpec((8, D_MODEL), lambda i: (jnp.minimum((i + 1) * r8, n // 8 - 1), 0)),
            pl.BlockSpec((tm, D_MODEL), row),
            pl.BlockSpec((1, 6, D_MODEL), lambda i: (i // tps, 0, 0)),
            _const_spec((D_MODEL, 2 * D_FF)),
            _const_spec((3, 2 * D_FF)),
            _const_spec((1, 2 * D_FF)),
            _const_spec((D_FF, D_MODEL)),
            _const_spec((1, D_MODEL)),
        ],
        out_specs=pl.BlockSpec((tm, D_MODEL), row),
        out_shape=jax.ShapeDtypeStruct((n, D_MODEL), F32),
        scratch_shapes=[pltpu.VMEM((tm, D_FF), BF16)],
        compiler_params=_params(1),
    )(h2, h2, h2, x1, mod_lat, wup, fcw, fcb, wdn, gf)


def _rope_tables():
    rows = SEQ // GRID_W
    row = jnp.repeat(jnp.arange(rows), GRID_W).astype(F32)
    col = jnp.tile(jnp.arange(GRID_W), rows).astype(F32)
    axis_dim = QK_ROPE_DIM // 2
    inv = ROPE_THETA ** (-jnp.arange(0, axis_dim, 2, dtype=F32) / axis_dim)
    ang = jnp.concatenate([row[:, None] * inv, col[:, None] * inv], axis=-1)
    j = jnp.arange(QK_ROPE_DIM)
    idx = (j // axis_dim) * ROPE_HALF + j % ROPE_HALF
    first_half = ((j % axis_dim) // ROPE_HALF) == 0
    cos = jnp.cos(ang)[:, idx]
    sin = jnp.sin(ang)[:, idx]
    s0 = jnp.where(first_half, -sin, 0.0)
    s1 = jnp.where(first_half, 0.0, sin)
    scale = (QK_NOPE_DIM + QK_ROPE_DIM) ** -0.5

    def block(nope_val, rope_part):
        return jnp.concatenate([jnp.full((SEQ, QK_NOPE_DIM), nope_val, F32), rope_part,
                                jnp.zeros((SEQ, LANES - QK_NOPE_DIM - QK_ROPE_DIM), F32)], axis=1)

    return jnp.stack([block(scale, scale * cos), block(0.0, scale * s0), block(0.0, scale * s1),
                      block(0.0, cos), block(0.0, s0), block(0.0, s1)])


def _pack_w_in(w):
    z = lambda n: jnp.zeros((D_MODEL, n), w.dtype)
    kr_end = KV_LORA_RANK + QK_ROPE_DIM
    return jnp.concatenate([w[:, :KV_LORA_RANK], z(QK_NOPE_DIM), w[:, KV_LORA_RANK:kr_end],
                            z(LANES - QK_NOPE_DIM - QK_ROPE_DIM), w[:, kr_end:]], axis=1).astype(BF16)


def _pack_w_ukv(w):
    w = w.reshape(KV_LORA_RANK, N_HEADS, QK_NOPE_DIM + V_HEAD_DIM)
    kn, vv = w[..., :QK_NOPE_DIM], w[..., QK_NOPE_DIM:]
    z = jnp.zeros_like(vv)
    wk = jnp.concatenate([kn, jnp.zeros_like(kn)], axis=-1)
    even = (jnp.arange(N_HEADS) % 2 == 0)[None, :, None]
    wv = jnp.where(even, jnp.concatenate([vv, z], axis=-1), jnp.concatenate([z, vv], axis=-1))
    return jnp.concatenate([wk.reshape(KV_LORA_RANK, HEAD_COLS),
                            wv.reshape(KV_LORA_RANK, HEAD_COLS)], axis=1).astype(BF16)


def _pack_w_uq(w):
    w = w.reshape(Q_LORA_RANK, N_HEADS, QK_NOPE_DIM + QK_ROPE_DIM)
    pad = jnp.zeros((Q_LORA_RANK, N_HEADS, LANES - QK_NOPE_DIM - QK_ROPE_DIM), w.dtype)
    return jnp.concatenate([w, pad], axis=-1).reshape(Q_LORA_RANK, HEAD_COLS).astype(BF16)


def kernel(x, c, ctx, c_ctx, w_ada, b_ada, norm1_g, w_in, q_norm_g, kv_norm_g, w_uq, w_ukv,
           conv_w, conv_b, w_attn_out, w_conv_out, w_o, norm2_g, w_up, ffn_conv_w, ffn_conv_b,
           w_down, final_g):
    assert x.shape == (BATCH, SEQ, D_MODEL) and ctx.shape == (BATCH, CTX_LEN, D_MODEL)
    assert w_ada.shape[0] == 1, "single-layer block"
    tm = 512

    cvec = jnp.concatenate([c, c_ctx[None, :], jnp.zeros((16 - BATCH - 1, D_MODEL), F32)], axis=0)
    mod = _ada(cvec, w_ada[0], b_ada[0][None, :])
    mod_lat = mod[:BATCH].reshape(BATCH, 6, D_MODEL)
    mod_ctx = mod[BATCH:BATCH + 1].reshape(1, 6, D_MODEL)

    tab = _rope_tables()
    w_in_p = _pack_w_in(w_in[0])
    w_kv_p = _pack_w_ukv(w_ukv[0])
    w_q_p = _pack_w_uq(w_uq[0])
    g1 = norm1_g[0][None, :]
    kvg = kv_norm_g[0][None, :]
    qg = q_norm_g[0][None, :]

    x2 = x.reshape(BATCH * SEQ, D_MODEL)
    c2 = ctx.reshape(BATCH * CTX_LEN, D_MODEL)

    kc, vc = _ctxproj(c2, mod_ctx, g1, w_in_p, kvg, w_kv_p, CTX_LEN)
    q, kl, vl, ga, gcy = _inproj(x2, mod_lat, tab, g1, w_in_p, kvg, qg, w_kv_p, w_q_p,
                                 conv_w[0], conv_b[0][None, :], w_conv_out[0].astype(BF16), tm)
    o = _attn(q, kl, vl, kc, vc, 512)
    x1, h2 = _merge(o, ga, gcy, x2, mod_lat, w_attn_out[0].astype(BF16), w_o[0].astype(BF16),
                    norm2_g[0][None, :], tm)
    out = _ffn(h2, x1, mod_lat, w_up[0].astype(BF16), ffn_conv_w[0], ffn_conv_b[0][None, :],
               w_down[0].astype(BF16), final_g[None, :], tm, 256)
    return out.reshape(BATCH, SEQ, D_MODEL)
```

```python
import functools

import jax
import jax.numpy as jnp
from jax import lax
from jax.experimental import pallas as pl
from jax.experimental.pallas import tpu as pltpu

D_MODEL = 1024
BATCH = 8
SEQ = 2048
GRID_W = 64
CTX_LEN = 256
N_HEADS = 8
QK_NOPE_DIM = 64
QK_ROPE_DIM = 32
V_HEAD_DIM = 64
Q_LORA_RANK = 384
KV_LORA_RANK = 256
ROPE_THETA = 10000.0
CONV_DIM = 512
D_FF = 2816
EPS = 1e-6
ATTN_DIM = N_HEADS * V_HEAD_DIM

LANES = 128
HALF = LANES // 2
HEAD_COLS = N_HEADS * LANES
ROPE_HALF = QK_ROPE_DIM // 4
HALO = 8
HALO_BF16 = 16

C_KV = 0
C_KR = C_KV + KV_LORA_RANK
C_Q = C_KR + LANES
C_CX = C_Q + Q_LORA_RANK
C_CB = C_CX + CONV_DIM
C_CC = C_CB + CONV_DIM
C_GA = C_CC + CONV_DIM
C_GC = C_GA + D_MODEL
C_END = C_GC + D_MODEL

VMEM_LIMIT = 56 * 1024 * 1024

TM = 512
TQ = 512
FF_CHUNK = 256

BF16 = jnp.bfloat16
F32 = jnp.float32


def _dot(a, b):
    return jnp.dot(a, b, preferred_element_type=F32)


def _dot_nt(a, b):
    return lax.dot_general(a, b, (((1,), (1,)), ((), ())), preferred_element_type=F32)


def _rms(x, g):
    return x * lax.rsqrt(jnp.mean(x * x, axis=-1, keepdims=True) + EPS) * g


def _sigmoid(x):
    return 1.0 / (1.0 + jnp.exp(-x))


def _rope_block(xh, c, s0, s1):
    return (xh * c + pltpu.roll(xh, LANES - ROPE_HALF, 1) * s0
            + pltpu.roll(xh, ROPE_HALF, 1) * s1)


def _conv3(u_all, m, first, last, w_ref, b_ref, sl):
    u = u_all[:m]
    prev_row = jnp.where(first, 0.0, u_all[m + HALO - 1:m + HALO])
    next_row = jnp.where(last, 0.0, u_all[m + HALO:m + HALO + 1])
    row = lax.broadcasted_iota(jnp.int32, u.shape, 0)
    u_dn = jnp.where(row == 0, prev_row, pltpu.roll(u, 1, 0))
    u_up = jnp.where(row == m - 1, next_row, pltpu.roll(u, m - 1, 0))
    return b_ref[:, sl] + u_dn * w_ref[0:1, sl] + u * w_ref[1:2, sl] + u_up * w_ref[2:3, sl]


def _const_spec(shape):
    nd = len(shape)
    return pl.BlockSpec(shape, lambda *_: (0,) * nd, pipeline_mode=pl.Buffered(1))


def _params(n_grid):
    return pltpu.CompilerParams(dimension_semantics=("arbitrary",) * n_grid,
                                vmem_limit_bytes=VMEM_LIMIT)


def _ada_kernel(c_ref, w_ref, b_ref, o_ref):
    c = c_ref[...]
    s = (c * _sigmoid(c)).astype(BF16)
    o_ref[...] = _dot(s, w_ref[...].astype(BF16)) + b_ref[...]


def _ada(cvec, w_ada, b_ada):
    rows = cvec.shape[0]
    n = w_ada.shape[1]
    tn = 1024
    return pl.pallas_call(
        _ada_kernel,
        name="ada",
        grid=(n // tn,),
        in_specs=[pl.BlockSpec((rows, D_MODEL), lambda j: (0, 0)),
                  pl.BlockSpec((D_MODEL, tn), lambda j: (0, j)),
                  pl.BlockSpec((1, tn), lambda j: (0, j))],
        out_specs=pl.BlockSpec((rows, tn), lambda j: (0, j)),
        out_shape=jax.ShapeDtypeStruct((rows, n), F32),
        compiler_params=_params(1),
    )(cvec, w_ada, b_ada)


def _kv_from(pa, kr_even, kvg_ref, wkv_ref, k_ref, v_ref):
    ckv = _rms(pa[:, C_KV:C_KV + KV_LORA_RANK], kvg_ref[...]).astype(BF16)
    kvv = _dot(ckv, wkv_ref[...])
    kr_odd = pltpu.roll(kr_even, HALF, 1)
    lo = lax.broadcasted_iota(jnp.int32, kr_even.shape, 1) < HALF
    for p in range(N_HEADS // 2):
        kn_pair = kvv[:, p * LANES:(p + 1) * LANES]
        k_ref[:, (2 * p) * LANES:(2 * p + 1) * LANES] = jnp.where(lo, kn_pair, kr_even).astype(BF16)
        k_ref[:, (2 * p + 1) * LANES:(2 * p + 2) * LANES] = jnp.where(lo, kr_odd, kn_pair).astype(BF16)
    v_ref[...] = kvv[:, ATTN_DIM:].astype(BF16)


def _inproj_kernel(x_ref, xp_ref, xn_ref, mod_ref, tab_ref, g1_ref, win_ref, kvg_ref, qg_ref,
                   wkv_ref, wq_ref, cw_ref, cb_ref, wco_ref,
                   q_ref, k_ref, v_ref, gg_ref):
    s = pl.program_id(0)
    first = s == 0
    last = s == pl.num_programs(0) - 1
    sh = mod_ref[0, 0:1, :]
    sc = mod_ref[0, 1:2, :]
    g1 = g1_ref[...]
    xcat = jnp.concatenate([x_ref[...], xp_ref[...], xn_ref[...]], axis=0)
    hcat = (_rms(xcat, g1) * (1.0 + sc) + sh).astype(BF16)
    hb = hcat[:TM]

    pa = _dot(hb, win_ref[:, C_KV:C_CX])
    kr = _rope_block(pa[:, C_KR:C_KR + LANES], tab_ref[3], tab_ref[4], tab_ref[5])
    _kv_from(pa, kr, kvg_ref, wkv_ref, k_ref, v_ref)
    cq = _rms(pa[:, C_Q:C_Q + Q_LORA_RANK], qg_ref[...]).astype(BF16)
    qf = _dot(cq, wq_ref[...])
    for h in range(N_HEADS):
        sl = slice(h * LANES, (h + 1) * LANES)
        t = 6 * (h % 2)
        q_ref[:, sl] = _rope_block(qf[:, sl], tab_ref[t], tab_ref[t + 1], tab_ref[t + 2]).astype(BF16)

    x_in = _dot(hcat, win_ref[:, C_CX:C_CB])
    b_gate = _dot(hb, win_ref[:, C_CB:C_CC])
    c_gate = _dot(hcat, win_ref[:, C_CC:C_GA])
    conv = _conv3(c_gate * x_in, TM, first, last, cw_ref, cb_ref, slice(0, CONV_DIM))
    y_conv = _dot((b_gate * conv).astype(BF16), wco_ref[...])

    pg = _dot(hb, win_ref[:, C_GA:C_END])
    gg_ref[:, :D_MODEL] = _sigmoid(pg[:, :D_MODEL]).astype(BF16)
    gg_ref[:, D_MODEL:] = (_sigmoid(pg[:, D_MODEL:]) * y_conv).astype(BF16)


def _halo_specs(tps, cols, rows=HALO):
    r = TM // rows
    nblk = BATCH * SEQ // rows
    prev = pl.BlockSpec((rows, cols), lambda s, b: (jnp.maximum((b * tps + s) * r - 1, 0), 0))
    nxt = pl.BlockSpec((rows, cols), lambda s, b: (jnp.minimum((b * tps + s + 1) * r, nblk - 1), 0))
    return prev, nxt


def _inproj(x2, mod_lat, tab, g1, w_in_p, kvg, qg, w_kv_p, w_q_p, cw, cb, wco):
    n = x2.shape[0]
    tps = SEQ // TM
    row = lambda s, b: (b * tps + s, 0)
    xprev, xnext = _halo_specs(tps, D_MODEL)
    return pl.pallas_call(
        _inproj_kernel,
        name="inproj",
        grid=(tps, BATCH),
        in_specs=[
            pl.BlockSpec((TM, D_MODEL), row),
            xprev,
            xnext,
            pl.BlockSpec((1, 6, D_MODEL), lambda s, b: (b, 0, 0)),
            pl.BlockSpec((12, TM, LANES), lambda s, b: (0, s, 0)),
            _const_spec((1, D_MODEL)),
            _const_spec((D_MODEL, C_END)),
            _const_spec((1, KV_LORA_RANK)),
            _const_spec((1, Q_LORA_RANK)),
            _const_spec((KV_LORA_RANK, 2 * ATTN_DIM)),
            _const_spec((Q_LORA_RANK, HEAD_COLS)),
            _const_spec((3, CONV_DIM)),
            _const_spec((1, CONV_DIM)),
            _const_spec((CONV_DIM, D_MODEL)),
        ],
        out_specs=[pl.BlockSpec((TM, HEAD_COLS), row), pl.BlockSpec((TM, HEAD_COLS), row),
                   pl.BlockSpec((TM, ATTN_DIM), row), pl.BlockSpec((TM, 2 * D_MODEL), row)],
        out_shape=[jax.ShapeDtypeStruct((n, HEAD_COLS), BF16), jax.ShapeDtypeStruct((n, HEAD_COLS), BF16),
                   jax.ShapeDtypeStruct((n, ATTN_DIM), BF16), jax.ShapeDtypeStruct((n, 2 * D_MODEL), BF16)],
        compiler_params=_params(2),
    )(x2, x2, x2, mod_lat, tab, g1, w_in_p, kvg, qg, w_kv_p, w_q_p, cw, cb, wco)


def _ctxproj_kernel(x_ref, mod_ref, g1_ref, win_ref, kvg_ref, wkv_ref, k_ref, v_ref):
    sh = mod_ref[0, 0:1, :]
    sc = mod_ref[0, 1:2, :]
    hb = (_rms(x_ref[...], g1_ref[...]) * (1.0 + sc) + sh).astype(BF16)
    pa = _dot(hb, win_ref[...])
    _kv_from(pa, pa[:, C_KR:C_KR + LANES], kvg_ref, wkv_ref, k_ref, v_ref)


def _ctxproj(c2, mod_ctx, g1, w_in_p, kvg, w_kv_p):
    n = c2.shape[0]
    tm = CTX_LEN
    row = lambda i: (i, 0)
    return pl.pallas_call(
        _ctxproj_kernel,
        name="ctxproj",
        grid=(n // tm,),
        in_specs=[
            pl.BlockSpec((tm, D_MODEL), row),
            _const_spec((1, 6, D_MODEL)),
            _const_spec((1, D_MODEL)),
            _const_spec((D_MODEL, C_Q)),
            _const_spec((1, KV_LORA_RANK)),
            _const_spec((KV_LORA_RANK, 2 * ATTN_DIM)),
        ],
        out_specs=[pl.BlockSpec((tm, HEAD_COLS), row), pl.BlockSpec((tm, ATTN_DIM), row)],
        out_shape=[jax.ShapeDtypeStruct((n, HEAD_COLS), BF16), jax.ShapeDtypeStruct((n, ATTN_DIM), BF16)],
        compiler_params=_params(1),
    )(c2, mod_ctx, g1, w_in_p, kvg, w_kv_p)


def _attn_kernel(q_ref, kl_ref, vl_ref, kc_ref, vc_ref, o_ref):
    lo = lax.broadcasted_iota(jnp.int32, (TQ, LANES), 1) < HALF
    for hp in range(N_HEADS // 2):
        vsl = slice(hp * LANES, (hp + 1) * LANES)
        accs = []
        for e in range(2):
            h = 2 * hp + e
            sl = slice(h * LANES, (h + 1) * LANES)
            qh = q_ref[:, sl]
            s_c = _dot_nt(qh, kc_ref[:, sl])
            s_l = _dot_nt(qh, kl_ref[:, sl])
            m = jnp.maximum(jnp.max(s_c, axis=-1, keepdims=True),
                            jnp.max(s_l, axis=-1, keepdims=True))
            p_c = jnp.exp(s_c - m)
            p_l = jnp.exp(s_l - m)
            den = jnp.sum(p_c, axis=-1, keepdims=True) + jnp.sum(p_l, axis=-1, keepdims=True)
            acc = _dot(p_c.astype(BF16), vc_ref[:, vsl]) + _dot(p_l.astype(BF16), vl_ref[:, vsl])
            accs.append(acc * (1.0 / den))
        o_ref[:, vsl] = jnp.where(lo, accs[0], accs[1]).astype(BF16)


def _attn(q, kl, vl, kc, vc):
    nq = SEQ // TQ
    return pl.pallas_call(
        _attn_kernel,
        name="attn",
        grid=(BATCH, nq),
        in_specs=[
            pl.BlockSpec((TQ, HEAD_COLS), lambda b, j: (b * nq + j, 0)),
            pl.BlockSpec((SEQ, HEAD_COLS), lambda b, j: (b, 0)),
            pl.BlockSpec((SEQ, ATTN_DIM), lambda b, j: (b, 0)),
            pl.BlockSpec((CTX_LEN, HEAD_COLS), lambda b, j: (b, 0)),
            pl.BlockSpec((CTX_LEN, ATTN_DIM), lambda b, j: (b, 0)),
        ],
        out_specs=pl.BlockSpec((TQ, ATTN_DIM), lambda b, j: (b * nq + j, 0)),
        out_shape=jax.ShapeDtypeStruct((BATCH * SEQ, ATTN_DIM), BF16),
        compiler_params=_params(2),
    )(q, kl, vl, kc, vc)


def _halo_rows_bf16(prev_ref, next_ref):
    p = prev_ref[...].astype(F32)[HALO_BF16 - HALO:]
    n = next_ref[...].astype(F32)[:HALO]
    return jnp.concatenate([p, n], axis=0).astype(BF16)


def _post_kernel(o_ref, op_ref, on_ref, gg_ref, gp_ref, gn_ref, x_ref, xp_ref, xn_ref, mod_ref,
                 wao_ref, wo_ref, g2_ref, wup_ref, fcw_ref, fcb_ref, wdn_ref, gf_ref,
                 out_ref, act_ref):
    s = pl.program_id(0)
    first = s == 0
    last = s == pl.num_programs(0) - 1
    o = jnp.concatenate([o_ref[...], _halo_rows_bf16(op_ref, on_ref)], axis=0)
    gg = jnp.concatenate([gg_ref[...], _halo_rows_bf16(gp_ref, gn_ref)], axis=0)
    x = jnp.concatenate([x_ref[...], xp_ref[...], xn_ref[...]], axis=0)

    y_attn = _dot(o, wao_ref[...])
    m = (gg[:, :D_MODEL].astype(F32) * y_attn + gg[:, D_MODEL:].astype(F32)).astype(BF16)
    x1 = x + mod_ref[0, 2:3, :] * _dot(m, wo_ref[...])
    h2 = (_rms(x1, g2_ref[...]) * (1.0 + mod_ref[0, 4:5, :]) + mod_ref[0, 3:4, :]).astype(BF16)

    for c in range(D_FF // FF_CHUNK):
        gsl = slice(c * FF_CHUNK, (c + 1) * FF_CHUNK)
        vsl = slice(D_FF + c * FF_CHUNK, D_FF + (c + 1) * FF_CHUNK)
        gate = _conv3(_dot(h2, wup_ref[:, gsl]), TM, first, last, fcw_ref, fcb_ref, gsl)
        val = _conv3(_dot(h2, wup_ref[:, vsl]), TM, first, last, fcw_ref, fcb_ref, vsl)
        act_ref[:, gsl] = (gate * _sigmoid(gate) * val).astype(BF16)

    y = _dot(act_ref[...], wdn_ref[...])
    x2 = x1[:TM] + mod_ref[0, 5:6, :] * y
    out_ref[...] = _rms(x2, gf_ref[...])


def _post(o, gg, x2, mod_lat, wao, wo, g2, wup, fcw, fcb, wdn, gf):
    n = x2.shape[0]
    tps = SEQ // TM
    row = lambda s, b: (b * tps + s, 0)
    oprev, onext = _halo_specs(tps, ATTN_DIM, HALO_BF16)
    gprev, gnext = _halo_specs(tps, 2 * D_MODEL, HALO_BF16)
    xprev, xnext = _halo_specs(tps, D_MODEL)
    return pl.pallas_call(
        _post_kernel,
        name="post",
        grid=(tps, BATCH),
        in_specs=[
            pl.BlockSpec((TM, ATTN_DIM), row), oprev, onext,
            pl.BlockSpec((TM, 2 * D_MODEL), row), gprev, gnext,
            pl.BlockSpec((TM, D_MODEL), row), xprev, xnext,
            pl.BlockSpec((1, 6, D_MODEL), lambda s, b: (b, 0, 0)),
            _const_spec((ATTN_DIM, D_MODEL)),
            _const_spec((D_MODEL, D_MODEL)),
            _const_spec((1, D_MODEL)),
            _const_spec((D_MODEL, 2 * D_FF)),
            _const_spec((3, 2 * D_FF)),
            _const_spec((1, 2 * D_FF)),
            _const_spec((D_FF, D_MODEL)),
            _const_spec((1, D_MODEL)),
        ],
        out_specs=pl.BlockSpec((TM, D_MODEL), row),
        out_shape=jax.ShapeDtypeStruct((n, D_MODEL), F32),
        scratch_shapes=[pltpu.VMEM((TM, D_FF), BF16)],
        compiler_params=_params(2),
    )(o, o, o, gg, gg, gg, x2, x2, x2, mod_lat, wao, wo, g2, wup, fcw, fcb, wdn, gf)


def _rope_tables():
    rows = SEQ // GRID_W
    row = jnp.repeat(jnp.arange(rows), GRID_W).astype(F32)
    col = jnp.tile(jnp.arange(GRID_W), rows).astype(F32)
    axis_dim = QK_ROPE_DIM // 2
    inv = ROPE_THETA ** (-jnp.arange(0, axis_dim, 2, dtype=F32) / axis_dim)
    ang = jnp.concatenate([row[:, None] * inv, col[:, None] * inv], axis=-1)
    j = jnp.arange(QK_ROPE_DIM)
    idx = (j // axis_dim) * ROPE_HALF + j % ROPE_HALF
    first_half = ((j % axis_dim) // ROPE_HALF) == 0
    cos = jnp.cos(ang)[:, idx]
    sin = jnp.sin(ang)[:, idx]
    s0 = jnp.where(first_half, -sin, 0.0)
    s1 = jnp.where(first_half, 0.0, sin)
    scale = (QK_NOPE_DIM + QK_ROPE_DIM) ** -0.5
    pad = jnp.zeros((SEQ, LANES - QK_NOPE_DIM - QK_ROPE_DIM), F32)

    def even(nope_val, rope_part):
        return jnp.concatenate([jnp.full((SEQ, QK_NOPE_DIM), nope_val, F32), rope_part, pad], axis=1)

    def odd(nope_val, rope_part):
        return jnp.concatenate([rope_part, pad, jnp.full((SEQ, QK_NOPE_DIM), nope_val, F32)], axis=1)

    tabs = []
    for blk in (even, odd):
        tabs += [blk(scale, scale * cos), blk(0.0, scale * s0), blk(0.0, scale * s1),
                 blk(0.0, cos), blk(0.0, s0), blk(0.0, s1)]
    return jnp.stack(tabs)


def _pack_w_in(w):
    w = w.astype(BF16)
    z = lambda n: jnp.zeros((D_MODEL, n), BF16)
    kr_end = KV_LORA_RANK + QK_ROPE_DIM
    return jnp.concatenate([w[:, :KV_LORA_RANK], z(QK_NOPE_DIM), w[:, KV_LORA_RANK:kr_end],
                            z(LANES - QK_NOPE_DIM - QK_ROPE_DIM), w[:, kr_end:]], axis=1)


def _pack_w_ukv(w):
    w = w.astype(BF16).reshape(KV_LORA_RANK, N_HEADS, QK_NOPE_DIM + V_HEAD_DIM)
    return jnp.concatenate([w[..., :QK_NOPE_DIM].reshape(KV_LORA_RANK, N_HEADS * QK_NOPE_DIM),
                            w[..., QK_NOPE_DIM:].reshape(KV_LORA_RANK, ATTN_DIM)], axis=1)


def _pack_w_uq(w):
    w = w.astype(BF16).reshape(Q_LORA_RANK, N_HEADS // 2, 2, QK_NOPE_DIM + QK_ROPE_DIM)
    pad = jnp.zeros((Q_LORA_RANK, N_HEADS // 2, LANES - QK_NOPE_DIM - QK_ROPE_DIM), BF16)
    ev, od = w[:, :, 0], w[:, :, 1]
    ev = jnp.concatenate([ev, pad], axis=-1)
    od = jnp.concatenate([od[..., QK_NOPE_DIM:], pad, od[..., :QK_NOPE_DIM]], axis=-1)
    return jnp.stack([ev, od], axis=2).reshape(Q_LORA_RANK, HEAD_COLS)


def kernel(x, c, ctx, c_ctx, w_ada, b_ada, norm1_g, w_in, q_norm_g, kv_norm_g, w_uq, w_ukv,
           conv_w, conv_b, w_attn_out, w_conv_out, w_o, norm2_g, w_up, ffn_conv_w, ffn_conv_b,
           w_down, final_g):
    assert x.shape == (BATCH, SEQ, D_MODEL) and ctx.shape == (BATCH, CTX_LEN, D_MODEL)
    assert w_ada.shape[0] == 1, "single-layer block"

    cvec = jnp.concatenate([c, c_ctx[None, :], jnp.zeros((16 - BATCH - 1, D_MODEL), F32)], axis=0)
    mod = _ada(cvec, w_ada.reshape(D_MODEL, 6 * D_MODEL), b_ada.reshape(1, 6 * D_MODEL))
    mod_lat = mod[:BATCH].reshape(BATCH, 6, D_MODEL)
    mod_ctx = mod[BATCH:BATCH + 1].reshape(1, 6, D_MODEL)

    tab = _rope_tables()
    w_in_p = _pack_w_in(w_in.reshape(D_MODEL, -1))
    w_kv_p = _pack_w_ukv(w_ukv.reshape(KV_LORA_RANK, -1))
    w_q_p = _pack_w_uq(w_uq.reshape(Q_LORA_RANK, -1))
    g1 = norm1_g.reshape(1, D_MODEL)
    kvg = kv_norm_g.reshape(1, KV_LORA_RANK)
    qg = q_norm_g.reshape(1, Q_LORA_RANK)

    x2 = x.reshape(BATCH * SEQ, D_MODEL)
    c2 = ctx.reshape(BATCH * CTX_LEN, D_MODEL)

    kc, vc = _ctxproj(c2, mod_ctx, g1, w_in_p, kvg, w_kv_p)
    q, kl, vl, gg = _inproj(x2, mod_lat, tab, g1, w_in_p, kvg, qg, w_kv_p, w_q_p,
                            conv_w.reshape(3, CONV_DIM), conv_b.reshape(1, CONV_DIM),
                            w_conv_out.reshape(CONV_DIM, D_MODEL).astype(BF16))
    o = _attn(q, kl, vl, kc, vc)
    out = _post(o, gg, x2, mod_lat,
                w_attn_out.reshape(ATTN_DIM, D_MODEL).astype(BF16),
                w_o.reshape(D_MODEL, D_MODEL).astype(BF16), norm2_g.reshape(1, D_MODEL),
                w_up.reshape(D_MODEL, 2 * D_FF).astype(BF16), ffn_conv_w.reshape(3, 2 * D_FF),
                ffn_conv_b.reshape(1, 2 * D_FF), w_down.reshape(D_FF, D_MODEL).astype(BF16),
                final_g.reshape(1, D_MODEL))
    return out.reshape(BATCH, SEQ, D_MODEL)
```

```python
import functools

import jax
import jax.numpy as jnp
from jax import lax
from jax.experimental import pallas as pl
from jax.experimental.pallas import tpu as pltpu

D_MODEL = 1024
BATCH = 8
SEQ = 2048
GRID_W = 64
CTX_LEN = 256
N_HEADS = 8
QK_NOPE_DIM = 64
QK_ROPE_DIM = 32
V_HEAD_DIM = 64
Q_LORA_RANK = 384
KV_LORA_RANK = 256
ROPE_THETA = 10000.0
CONV_DIM = 512
D_FF = 2816
EPS = 1e-6
ATTN_DIM = N_HEADS * V_HEAD_DIM

LANES = 128
HALF = LANES // 2
HEAD_COLS = N_HEADS * LANES
ROPE_HALF = QK_ROPE_DIM // 4
HALO = 8
HALO_BF16 = 16

C_KV = 0
C_KR = C_KV + KV_LORA_RANK
C_Q = C_KR + LANES
C_LAT = C_Q + Q_LORA_RANK
O_CX = KV_LORA_RANK + QK_ROPE_DIM + Q_LORA_RANK
O_CB = O_CX + CONV_DIM
O_CC = O_CB + CONV_DIM
O_GA = O_CC + CONV_DIM
O_GC = O_GA + D_MODEL
O_END = O_GC + D_MODEL

LOG2E = 1.4426950408889634

VMEM_LIMIT = 56 * 1024 * 1024

TM = 512
TQ = 512
Q_SUB = 1
FF_CHUNK = 256

BF16 = jnp.bfloat16
F32 = jnp.float32


def _dot(a, b):
    return jnp.dot(a, b, preferred_element_type=F32)


def _dot_nt(a, b):
    return lax.dot_general(a, b, (((1,), (1,)), ((), ())), preferred_element_type=F32)


def _rms(x, g):
    return x * lax.rsqrt(jnp.mean(x * x, axis=-1, keepdims=True) + EPS) * g


def _sigmoid(x):
    return 1.0 / (1.0 + jnp.exp(-x))


def _rope_block(xh, c, s0, s1):
    return (xh * c + pltpu.roll(xh, LANES - ROPE_HALF, 1) * s0
            + pltpu.roll(xh, ROPE_HALF, 1) * s1)


def _conv3(u_all, m, first, last, w_ref, b_ref, sl):
    u = u_all[:m]
    prev_row = jnp.where(first, 0.0, u_all[m + HALO - 1:m + HALO])
    next_row = jnp.where(last, 0.0, u_all[m + HALO:m + HALO + 1])
    row = lax.broadcasted_iota(jnp.int32, u.shape, 0)
    u_dn = jnp.where(row == 0, prev_row, pltpu.roll(u, 1, 0))
    u_up = jnp.where(row == m - 1, next_row, pltpu.roll(u, m - 1, 0))
    return b_ref[:, sl] + u_dn * w_ref[0:1, sl] + u * w_ref[1:2, sl] + u_up * w_ref[2:3, sl]


def _const_spec(shape):
    nd = len(shape)
    return pl.BlockSpec(shape, lambda *_: (0,) * nd, pipeline_mode=pl.Buffered(1))


def _params(n_grid):
    return pltpu.CompilerParams(dimension_semantics=("arbitrary",) * n_grid,
                                vmem_limit_bytes=VMEM_LIMIT)


def _ada_kernel(c_ref, w_ref, b_ref, o_ref):
    c = c_ref[...]
    s = (c * _sigmoid(c)).astype(BF16)
    o_ref[...] = _dot(s, w_ref[...].astype(BF16)) + b_ref[...]


def _ada(cvec, w_ada, b_ada):
    rows = cvec.shape[0]
    n = w_ada.shape[1]
    tn = 1024
    return pl.pallas_call(
        _ada_kernel,
        name="ada",
        grid=(n // tn,),
        in_specs=[pl.BlockSpec((rows, D_MODEL), lambda j: (0, 0)),
                  pl.BlockSpec((D_MODEL, tn), lambda j: (0, j)),
                  pl.BlockSpec((1, tn), lambda j: (0, j))],
        out_specs=pl.BlockSpec((rows, tn), lambda j: (0, j)),
        out_shape=jax.ShapeDtypeStruct((rows, n), F32),
        compiler_params=_params(1),
    )(cvec, w_ada, b_ada)


def _kv_from(pa, kr_even, kvg_ref, wkv_ref, k_ref, v_ref):
    ckv = _rms(pa[:, C_KV:C_KV + KV_LORA_RANK], kvg_ref[...]).astype(BF16)
    kvv = _dot(ckv, wkv_ref[...])
    kr_odd = pltpu.roll(kr_even, HALF, 1)
    lo = lax.broadcasted_iota(jnp.int32, kr_even.shape, 1) < HALF
    for p in range(N_HEADS // 2):
        kn_pair = kvv[:, p * LANES:(p + 1) * LANES]
        k_ref[:, (2 * p) * LANES:(2 * p + 1) * LANES] = jnp.where(lo, kn_pair, kr_even).astype(BF16)
        k_ref[:, (2 * p + 1) * LANES:(2 * p + 2) * LANES] = jnp.where(lo, kr_odd, kn_pair).astype(BF16)
    v_ref[...] = kvv[:, ATTN_DIM:].astype(BF16)


def _inproj_kernel(x_ref, xp_ref, xn_ref, mod_ref, tab_ref, g1_ref, wlat_ref, wcx_ref, wcb_ref,
                   wcc_ref, wga_ref, wgc_ref, kvg_ref, qg_ref, wkv_ref, wq_ref, cw_ref, cb_ref,
                   wco_ref, q_ref, k_ref, v_ref, gg_ref):
    s = pl.program_id(0)
    first = s == 0
    last = s == pl.num_programs(0) - 1
    sh = mod_ref[0, 0:1, :]
    sc = mod_ref[0, 1:2, :]
    g1 = g1_ref[...]
    xcat = jnp.concatenate([x_ref[...], xp_ref[...], xn_ref[...]], axis=0)
    hcat = (_rms(xcat, g1) * (1.0 + sc) + sh).astype(BF16)
    hb = hcat[:TM]

    pa = _dot(hb, wlat_ref[...])
    tabs = [[tab_ref[0], tab_ref[1], tab_ref[2]]]
    tabs.append([pltpu.roll(t, HALF, 1) for t in tabs[0]])
    kr = _rope_block(pa[:, C_KR:C_KR + LANES], *tabs[0])
    _kv_from(pa, kr, kvg_ref, wkv_ref, k_ref, v_ref)
    cq = _rms(pa[:, C_Q:C_Q + Q_LORA_RANK], qg_ref[...]).astype(BF16)
    qf = _dot(cq, wq_ref[...])
    for h in range(N_HEADS):
        sl = slice(h * LANES, (h + 1) * LANES)
        q_ref[:, sl] = _rope_block(qf[:, sl], *tabs[h % 2]).astype(BF16)

    x_in = _dot(hcat, wcx_ref[...])
    b_gate = _dot(hb, wcb_ref[...])
    c_gate = _dot(hcat, wcc_ref[...])
    conv = _conv3(c_gate * x_in, TM, first, last, cw_ref, cb_ref, slice(0, CONV_DIM))
    y_conv = _dot((b_gate * conv).astype(BF16), wco_ref[...])

    gg_ref[:, :D_MODEL] = _sigmoid(_dot(hb, wga_ref[...])).astype(BF16)
    gg_ref[:, D_MODEL:] = (_sigmoid(_dot(hb, wgc_ref[...])) * y_conv).astype(BF16)


def _halo_specs(tps, cols, rows=HALO):
    r = TM // rows
    nblk = BATCH * SEQ // rows
    prev = pl.BlockSpec((rows, cols), lambda s, b: (jnp.maximum((b * tps + s) * r - 1, 0), 0))
    nxt = pl.BlockSpec((rows, cols), lambda s, b: (jnp.minimum((b * tps + s + 1) * r, nblk - 1), 0))
    return prev, nxt


def _inproj(x2, mod_lat, tab, g1, w_lat, w_cx, w_cb, w_cc, w_ga, w_gc, kvg, qg, w_kv_p, w_q_p,
            cw, cb, wco):
    n = x2.shape[0]
    tps = SEQ // TM
    row = lambda s, b: (b * tps + s, 0)
    xprev, xnext = _halo_specs(tps, D_MODEL)
    return pl.pallas_call(
        _inproj_kernel,
        name="inproj",
        grid=(tps, BATCH),
        in_specs=[
            pl.BlockSpec((TM, D_MODEL), row),
            xprev,
            xnext,
            pl.BlockSpec((1, 6, D_MODEL), lambda s, b: (b, 0, 0)),
            pl.BlockSpec((3, TM, LANES), lambda s, b: (0, s, 0)),
            _const_spec((1, D_MODEL)),
            _const_spec((D_MODEL, C_LAT)),
            _const_spec((D_MODEL, CONV_DIM)),
            _const_spec((D_MODEL, CONV_DIM)),
            _const_spec((D_MODEL, CONV_DIM)),
            _const_spec((D_MODEL, D_MODEL)),
            _const_spec((D_MODEL, D_MODEL)),
            _const_spec((1, KV_LORA_RANK)),
            _const_spec((1, Q_LORA_RANK)),
            _const_spec((KV_LORA_RANK, 2 * ATTN_DIM)),
            _const_spec((Q_LORA_RANK, HEAD_COLS)),
            _const_spec((3, CONV_DIM)),
            _const_spec((1, CONV_DIM)),
            _const_spec((CONV_DIM, D_MODEL)),
        ],
        out_specs=[pl.BlockSpec((TM, HEAD_COLS), row), pl.BlockSpec((TM, HEAD_COLS), row),
                   pl.BlockSpec((TM, ATTN_DIM), row), pl.BlockSpec((TM, 2 * D_MODEL), row)],
        out_shape=[jax.ShapeDtypeStruct((n, HEAD_COLS), BF16), jax.ShapeDtypeStruct((n, HEAD_COLS), BF16),
                   jax.ShapeDtypeStruct((n, ATTN_DIM), BF16), jax.ShapeDtypeStruct((n, 2 * D_MODEL), BF16)],
        compiler_params=_params(2),
    )(x2, x2, x2, mod_lat, tab, g1, w_lat, w_cx, w_cb, w_cc, w_ga, w_gc, kvg, qg, w_kv_p, w_q_p,
      cw, cb, wco)


def _ctxproj_kernel(x_ref, mod_ref, g1_ref, win_ref, kvg_ref, wkv_ref, k_ref, v_ref):
    sh = mod_ref[0, 0:1, :]
    sc = mod_ref[0, 1:2, :]
    hb = (_rms(x_ref[...], g1_ref[...]) * (1.0 + sc) + sh).astype(BF16)
    pa = _dot(hb, win_ref[...])
    _kv_from(pa, pa[:, C_KR:C_KR + LANES], kvg_ref, wkv_ref, k_ref, v_ref)


def _ctxproj(c2, mod_ctx, g1, w_lat, kvg, w_kv_p):
    n = c2.shape[0]
    tm = CTX_LEN
    row = lambda i: (i, 0)
    return pl.pallas_call(
        _ctxproj_kernel,
        name="ctxproj",
        grid=(n // tm,),
        in_specs=[
            pl.BlockSpec((tm, D_MODEL), row),
            _const_spec((1, 6, D_MODEL)),
            _const_spec((1, D_MODEL)),
            _const_spec((D_MODEL, C_Q)),
            _const_spec((1, KV_LORA_RANK)),
            _const_spec((KV_LORA_RANK, 2 * ATTN_DIM)),
        ],
        out_specs=[pl.BlockSpec((tm, HEAD_COLS), row), pl.BlockSpec((tm, ATTN_DIM), row)],
        out_shape=[jax.ShapeDtypeStruct((n, HEAD_COLS), BF16), jax.ShapeDtypeStruct((n, ATTN_DIM), BF16)],
        compiler_params=_params(1),
    )(c2, mod_ctx, g1, w_lat, kvg, w_kv_p)


def _attn_kernel(q_ref, kl_ref, vl_ref, kc_ref, vc_ref, o_ref):
    lo = lax.broadcasted_iota(jnp.int32, (TQ, LANES), 1) < HALF
    for r in range(Q_SUB):
        rows = slice(r * TQ, (r + 1) * TQ)
        for hp in range(N_HEADS // 2):
            vsl = slice(hp * LANES, (hp + 1) * LANES)
            accs = []
            for e in range(2):
                h = 2 * hp + e
                sl = slice(h * LANES, (h + 1) * LANES)
                qh = q_ref[rows, sl]
                s_c = _dot_nt(qh, kc_ref[:, sl])
                s_l = _dot_nt(qh, kl_ref[:, sl])
                m = jnp.maximum(jnp.max(s_c, axis=-1, keepdims=True),
                                jnp.max(s_l, axis=-1, keepdims=True))
                p_c = jnp.exp2(s_c - m)
                p_l = jnp.exp2(s_l - m)
                den = jnp.sum(p_c, axis=-1, keepdims=True) + jnp.sum(p_l, axis=-1, keepdims=True)
                acc = _dot(p_c.astype(BF16), vc_ref[:, vsl]) + _dot(p_l.astype(BF16), vl_ref[:, vsl])
                accs.append(acc * (1.0 / den))
            o_ref[rows, vsl] = jnp.where(lo, accs[0], accs[1]).astype(BF16)


def _attn(q, kl, vl, kc, vc):
    tq = TQ * Q_SUB
    nq = SEQ // tq
    return pl.pallas_call(
        _attn_kernel,
        name="attn",
        grid=(BATCH, nq),
        in_specs=[
            pl.BlockSpec((tq, HEAD_COLS), lambda b, j: (b * nq + j, 0)),
            pl.BlockSpec((SEQ, HEAD_COLS), lambda b, j: (b, 0)),
            pl.BlockSpec((SEQ, ATTN_DIM), lambda b, j: (b, 0)),
            pl.BlockSpec((CTX_LEN, HEAD_COLS), lambda b, j: (b, 0)),
            pl.BlockSpec((CTX_LEN, ATTN_DIM), lambda b, j: (b, 0)),
        ],
        out_specs=pl.BlockSpec((tq, ATTN_DIM), lambda b, j: (b * nq + j, 0)),
        out_shape=jax.ShapeDtypeStruct((BATCH * SEQ, ATTN_DIM), BF16),
        compiler_params=_params(2),
    )(q, kl, vl, kc, vc)


def _halo_rows_bf16(prev_ref, next_ref):
    p = prev_ref[...].astype(F32)[HALO_BF16 - HALO:]
    n = next_ref[...].astype(F32)[:HALO]
    return jnp.concatenate([p, n], axis=0).astype(BF16)


def _post_kernel(o_ref, op_ref, on_ref, gg_ref, gp_ref, gn_ref, x_ref, xp_ref, xn_ref, mod_ref,
                 wao_ref, wo_ref, g2_ref, wup_ref, fcw_ref, fcb_ref, wdn_ref, gf_ref,
                 out_ref, act_ref):
    s = pl.program_id(0)
    first = s == 0
    last = s == pl.num_programs(0) - 1
    o = jnp.concatenate([o_ref[...], _halo_rows_bf16(op_ref, on_ref)], axis=0)
    gg = jnp.concatenate([gg_ref[...], _halo_rows_bf16(gp_ref, gn_ref)], axis=0)
    x = jnp.concatenate([x_ref[...], xp_ref[...], xn_ref[...]], axis=0)

    y_attn = _dot(o, wao_ref[...])
    m = (gg[:, :D_MODEL].astype(F32) * y_attn + gg[:, D_MODEL:].astype(F32)).astype(BF16)
    x1 = x + mod_ref[0, 2:3, :] * _dot(m, wo_ref[...])
    h2 = (_rms(x1, g2_ref[...]) * (1.0 + mod_ref[0, 4:5, :]) + mod_ref[0, 3:4, :]).astype(BF16)

    for c in range(D_FF // FF_CHUNK):
        gsl = slice(c * FF_CHUNK, (c + 1) * FF_CHUNK)
        vsl = slice(D_FF + c * FF_CHUNK, D_FF + (c + 1) * FF_CHUNK)
        gate = _conv3(_dot(h2, wup_ref[:, gsl]), TM, first, last, fcw_ref, fcb_ref, gsl)
        val = _conv3(_dot(h2, wup_ref[:, vsl]), TM, first, last, fcw_ref, fcb_ref, vsl)
        act_ref[:, gsl] = (gate * _sigmoid(gate) * val).astype(BF16)

    y = _dot(act_ref[...], wdn_ref[...])
    x2 = x1[:TM] + mod_ref[0, 5:6, :] * y
    out_ref[...] = _rms(x2, gf_ref[...])


def _post(o, gg, x2, mod_lat, wao, wo, g2, wup, fcw, fcb, wdn, gf):
    n = x2.shape[0]
    tps = SEQ // TM
    row = lambda s, b: (b * tps + s, 0)
    oprev, onext = _halo_specs(tps, ATTN_DIM, HALO_BF16)
    gprev, gnext = _halo_specs(tps, 2 * D_MODEL, HALO_BF16)
    xprev, xnext = _halo_specs(tps, D_MODEL)
    return pl.pallas_call(
        _post_kernel,
        name="post",
        grid=(tps, BATCH),
        in_specs=[
            pl.BlockSpec((TM, ATTN_DIM), row), oprev, onext,
            pl.BlockSpec((TM, 2 * D_MODEL), row), gprev, gnext,
            pl.BlockSpec((TM, D_MODEL), row), xprev, xnext,
            pl.BlockSpec((1, 6, D_MODEL), lambda s, b: (b, 0, 0)),
            _const_spec((ATTN_DIM, D_MODEL)),
            _const_spec((D_MODEL, D_MODEL)),
            _const_spec((1, D_MODEL)),
            _const_spec((D_MODEL, 2 * D_FF)),
            _const_spec((3, 2 * D_FF)),
            _const_spec((1, 2 * D_FF)),
            _const_spec((D_FF, D_MODEL)),
            _const_spec((1, D_MODEL)),
        ],
        out_specs=pl.BlockSpec((TM, D_MODEL), row),
        out_shape=jax.ShapeDtypeStruct((n, D_MODEL), F32),
        scratch_shapes=[pltpu.VMEM((TM, D_FF), BF16)],
        compiler_params=_params(2),
    )(o, o, o, gg, gg, gg, x2, x2, x2, mod_lat, wao, wo, g2, wup, fcw, fcb, wdn, gf)


def _rope_tables():
    rows = SEQ // GRID_W
    row = jnp.repeat(jnp.arange(rows), GRID_W).astype(F32)
    col = jnp.tile(jnp.arange(GRID_W), rows).astype(F32)
    axis_dim = QK_ROPE_DIM // 2
    inv = ROPE_THETA ** (-jnp.arange(0, axis_dim, 2, dtype=F32) / axis_dim)
    ang = jnp.concatenate([row[:, None] * inv, col[:, None] * inv], axis=-1)
    j = jnp.arange(QK_ROPE_DIM)
    idx = (j // axis_dim) * ROPE_HALF + j % ROPE_HALF
    first_half = ((j % axis_dim) // ROPE_HALF) == 0
    cos = jnp.cos(ang)[:, idx]
    sin = jnp.sin(ang)[:, idx]
    s0 = jnp.where(first_half, -sin, 0.0)
    s1 = jnp.where(first_half, 0.0, sin)
    pad = jnp.zeros((SEQ, LANES - QK_NOPE_DIM - QK_ROPE_DIM), F32)

    def even(nope_val, rope_part):
        return jnp.concatenate([jnp.full((SEQ, QK_NOPE_DIM), nope_val, F32), rope_part, pad], axis=1)

    return jnp.stack([even(1.0, cos), even(0.0, s0), even(0.0, s1)])


def _pack_w_lat(w):
    z = lambda n: jnp.zeros((D_MODEL, n), BF16)
    kr_end = KV_LORA_RANK + QK_ROPE_DIM
    return jnp.concatenate([w[:, :KV_LORA_RANK].astype(BF16), z(QK_NOPE_DIM),
                            w[:, KV_LORA_RANK:kr_end].astype(BF16),
                            z(LANES - QK_NOPE_DIM - QK_ROPE_DIM),
                            w[:, kr_end:O_CX].astype(BF16)], axis=1)


def _pack_w_ukv(w):
    w = w.astype(BF16).reshape(KV_LORA_RANK, N_HEADS, QK_NOPE_DIM + V_HEAD_DIM)
    return jnp.concatenate([w[..., :QK_NOPE_DIM].reshape(KV_LORA_RANK, N_HEADS * QK_NOPE_DIM),
                            w[..., QK_NOPE_DIM:].reshape(KV_LORA_RANK, ATTN_DIM)], axis=1)


def _pack_w_uq(w):
    qscale = (QK_NOPE_DIM + QK_ROPE_DIM) ** -0.5 * LOG2E
    w = (w * qscale).astype(BF16).reshape(Q_LORA_RANK, N_HEADS // 2, 2, QK_NOPE_DIM + QK_ROPE_DIM)
    pad = jnp.zeros((Q_LORA_RANK, N_HEADS // 2, LANES - QK_NOPE_DIM - QK_ROPE_DIM), BF16)
    ev, od = w[:, :, 0], w[:, :, 1]
    ev = jnp.concatenate([ev, pad], axis=-1)
    od = jnp.concatenate([od[..., QK_NOPE_DIM:], pad, od[..., :QK_NOPE_DIM]], axis=-1)
    return jnp.stack([ev, od], axis=2).reshape(Q_LORA_RANK, HEAD_COLS)


def kernel(x, c, ctx, c_ctx, w_ada, b_ada, norm1_g, w_in, q_norm_g, kv_norm_g, w_uq, w_ukv,
           conv_w, conv_b, w_attn_out, w_conv_out, w_o, norm2_g, w_up, ffn_conv_w, ffn_conv_b,
           w_down, final_g):
    assert x.shape == (BATCH, SEQ, D_MODEL) and ctx.shape == (BATCH, CTX_LEN, D_MODEL)
    assert w_ada.shape[0] == 1, "single-layer block"

    cvec = jnp.concatenate([c, c_ctx[None, :], jnp.zeros((16 - BATCH - 1, D_MODEL), F32)], axis=0)
    mod = _ada(cvec, w_ada.reshape(D_MODEL, 6 * D_MODEL), b_ada.reshape(1, 6 * D_MODEL))
    mod_lat = mod[:BATCH].reshape(BATCH, 6, D_MODEL)
    mod_ctx = mod[BATCH:BATCH + 1].reshape(1, 6, D_MODEL)

    tab = _rope_tables()
    w_in2 = w_in.reshape(D_MODEL, O_END)
    w_lat = _pack_w_lat(w_in2)
    w_cx, w_cb, w_cc, w_ga, w_gc = (w_in2[:, a:b].astype(BF16) for a, b in
                                    ((O_CX, O_CB), (O_CB, O_CC), (O_CC, O_GA), (O_GA, O_GC), (O_GC, O_END)))
    w_kv_p = _pack_w_ukv(w_ukv.reshape(KV_LORA_RANK, -1))
    w_q_p = _pack_w_uq(w_uq.reshape(Q_LORA_RANK, -1))
    g1 = norm1_g.reshape(1, D_MODEL)
    kvg = kv_norm_g.reshape(1, KV_LORA_RANK)
    qg = q_norm_g.reshape(1, Q_LORA_RANK)

    x2 = x.reshape(BATCH * SEQ, D_MODEL)
    c2 = ctx.reshape(BATCH * CTX_LEN, D_MODEL)

    kc, vc = _ctxproj(c2, mod_ctx, g1, w_lat, kvg, w_kv_p)
    q, kl, vl, gg = _inproj(x2, mod_lat, tab, g1, w_lat, w_cx, w_cb, w_cc, w_ga, w_gc, kvg, qg,
                            w_kv_p, w_q_p,
                            conv_w.reshape(3, CONV_DIM), conv_b.reshape(1, CONV_DIM),
                            w_conv_out.reshape(CONV_DIM, D_MODEL).astype(BF16))
    o = _attn(q, kl, vl, kc, vc)
    out = _post(o, gg, x2, mod_lat,
                w_attn_out.reshape(ATTN_DIM, D_MODEL).astype(BF16),
                w_o.reshape(D_MODEL, D_MODEL).astype(BF16), norm2_g.reshape(1, D_MODEL),
                w_up.reshape(D_MODEL, 2 * D_FF).astype(BF16), ffn_conv_w.reshape(3, 2 * D_FF),
                ffn_conv_b.reshape(1, 2 * D_FF), w_down.reshape(D_FF, D_MODEL).astype(BF16),
                final_g.reshape(1, D_MODEL))
    return out.reshape(BATCH, SEQ, D_MODEL)
```

```python
import functools

import jax
import jax.numpy as jnp
from jax import lax
from jax.experimental import pallas as pl
from jax.experimental.pallas import tpu as pltpu

D_MODEL = 1024
BATCH = 8
SEQ = 2048
GRID_W = 64
CTX_LEN = 256
N_HEADS = 8
QK_NOPE_DIM = 64
QK_ROPE_DIM = 32
V_HEAD_DIM = 64
Q_LORA_RANK = 384
KV_LORA_RANK = 256
ROPE_THETA = 10000.0
CONV_DIM = 512
D_FF = 2816
EPS = 1e-6
ATTN_DIM = N_HEADS * V_HEAD_DIM

LANES = 128
HALF = LANES // 2
HEAD_COLS = N_HEADS * LANES
ROPE_HALF = QK_ROPE_DIM // 4
HALO = 8
HALO_BF16 = 16

C_KV = 0
C_KR = C_KV + KV_LORA_RANK
C_Q = C_KR + LANES
C_LAT = C_Q + Q_LORA_RANK
O_CX = KV_LORA_RANK + QK_ROPE_DIM + Q_LORA_RANK
O_CB = O_CX + CONV_DIM
O_CC = O_CB + CONV_DIM
O_GA = O_CC + CONV_DIM
O_GC = O_GA + D_MODEL
O_END = O_GC + D_MODEL

LOG2E = 1.4426950408889634

VMEM_LIMIT = 56 * 1024 * 1024

TM = 512
TQ = 512
Q_SUB = 1
FF_CHUNK = 256

BF16 = jnp.bfloat16
F32 = jnp.float32


def _dot(a, b):
    return jnp.dot(a, b, preferred_element_type=F32)


def _dot_nt(a, b):
    return lax.dot_general(a, b, (((1,), (1,)), ((), ())), preferred_element_type=F32)


def _rms(x, g):
    return x * lax.rsqrt(jnp.mean(x * x, axis=-1, keepdims=True) + EPS) * g


def _sigmoid(x):
    return 1.0 / (1.0 + jnp.exp(-x))


def _rope_block(xh, c, s0, s1):
    return (xh * c + pltpu.roll(xh, LANES - ROPE_HALF, 1) * s0
            + pltpu.roll(xh, ROPE_HALF, 1) * s1)


def _conv3(u_all, m, first, last, w_ref, b_ref, sl):
    u = u_all[:m]
    prev_row = jnp.where(first, 0.0, u_all[m + HALO - 1:m + HALO])
    next_row = jnp.where(last, 0.0, u_all[m + HALO:m + HALO + 1])
    row = lax.broadcasted_iota(jnp.int32, u.shape, 0)
    u_dn = jnp.where(row == 0, prev_row, pltpu.roll(u, 1, 0))
    u_up = jnp.where(row == m - 1, next_row, pltpu.roll(u, m - 1, 0))
    return b_ref[:, sl] + u_dn * w_ref[0:1, sl] + u * w_ref[1:2, sl] + u_up * w_ref[2:3, sl]


def _const_spec(shape):
    nd = len(shape)
    return pl.BlockSpec(shape, lambda *_: (0,) * nd, pipeline_mode=pl.Buffered(1))


def _params(n_grid):
    return pltpu.CompilerParams(dimension_semantics=("arbitrary",) * n_grid,
                                vmem_limit_bytes=VMEM_LIMIT)


def _ada_kernel(c_ref, w_ref, b_ref, o_ref):
    c = c_ref[...]
    s = (c * _sigmoid(c)).astype(BF16)
    o_ref[...] = _dot(s, w_ref[...].astype(BF16)) + b_ref[...]


def _ada(cvec, w_ada, b_ada):
    rows = cvec.shape[0]
    n = w_ada.shape[1]
    tn = 1024
    return pl.pallas_call(
        _ada_kernel,
        name="ada",
        grid=(n // tn,),
        in_specs=[pl.BlockSpec((rows, D_MODEL), lambda j: (0, 0)),
                  pl.BlockSpec((D_MODEL, tn), lambda j: (0, j)),
                  pl.BlockSpec((1, tn), lambda j: (0, j))],
        out_specs=pl.BlockSpec((rows, tn), lambda j: (0, j)),
        out_shape=jax.ShapeDtypeStruct((rows, n), F32),
        compiler_params=_params(1),
    )(cvec, w_ada, b_ada)


def _kv_from(pa, kr, kvg_ref, wknt_ref, wv_ref, kt_ref, v_ref):
    m = pa.shape[0]
    ckv = _rms(pa[:, C_KV:C_KV + KV_LORA_RANK], kvg_ref[...]).astype(BF16)
    knt = _dot_nt(wknt_ref[...], ckv).astype(BF16)
    krt = kr.T[QK_NOPE_DIM:QK_NOPE_DIM + QK_ROPE_DIM].astype(BF16)
    pad = jnp.zeros((LANES - QK_NOPE_DIM - QK_ROPE_DIM, m), BF16)
    for h in range(N_HEADS):
        r0 = h * LANES
        kt_ref[r0:r0 + QK_NOPE_DIM, :] = knt[h * QK_NOPE_DIM:(h + 1) * QK_NOPE_DIM]
        kt_ref[r0 + QK_NOPE_DIM:r0 + QK_NOPE_DIM + QK_ROPE_DIM, :] = krt
        kt_ref[r0 + QK_NOPE_DIM + QK_ROPE_DIM:r0 + LANES, :] = pad
    v_ref[...] = _dot(ckv, wv_ref[...]).astype(BF16)


def _inproj_kernel(x_ref, xp_ref, xn_ref, mod_ref, tab_ref, g1_ref, wlat_ref, wcx_ref, wcb_ref,
                   wcc_ref, wga_ref, wgc_ref, kvg_ref, qg_ref, wknt_ref, wv_ref, wq_ref, cw_ref,
                   cb_ref, wco_ref, q_ref, kt_ref, v_ref, gg_ref):
    s = pl.program_id(0)
    first = s == 0
    last = s == pl.num_programs(0) - 1
    sh = mod_ref[0, 0:1, :]
    sc = mod_ref[0, 1:2, :]
    g1 = g1_ref[...]
    xcat = jnp.concatenate([x_ref[...], xp_ref[...], xn_ref[...]], axis=0)
    hcat = (_rms(xcat, g1) * (1.0 + sc) + sh).astype(BF16)
    hb = hcat[:TM]

    pa = _dot(hb, wlat_ref[...])
    tabs = (tab_ref[0], tab_ref[1], tab_ref[2])
    kr = _rope_block(pa[:, C_KR:C_KR + LANES], *tabs)
    _kv_from(pa, kr, kvg_ref, wknt_ref, wv_ref, kt_ref, v_ref)
    cq = _rms(pa[:, C_Q:C_Q + Q_LORA_RANK], qg_ref[...]).astype(BF16)
    qf = _dot(cq, wq_ref[...])
    for h in range(N_HEADS):
        sl = slice(h * LANES, (h + 1) * LANES)
        q_ref[:, sl] = _rope_block(qf[:, sl], *tabs).astype(BF16)

    x_in = _dot(hcat, wcx_ref[...])
    b_gate = _dot(hb, wcb_ref[...])
    c_gate = _dot(hcat, wcc_ref[...])
    conv = _conv3(c_gate * x_in, TM, first, last, cw_ref, cb_ref, slice(0, CONV_DIM))
    y_conv = _dot((b_gate * conv).astype(BF16), wco_ref[...])

    gg_ref[:, :D_MODEL] = _sigmoid(_dot(hb, wga_ref[...])).astype(BF16)
    gg_ref[:, D_MODEL:] = (_sigmoid(_dot(hb, wgc_ref[...])) * y_conv).astype(BF16)


def _halo_specs(tps, cols, rows=HALO):
    r = TM // rows
    nblk = BATCH * SEQ // rows
    prev = pl.BlockSpec((rows, cols), lambda s, b: (jnp.maximum((b * tps + s) * r - 1, 0), 0))
    nxt = pl.BlockSpec((rows, cols), lambda s, b: (jnp.minimum((b * tps + s + 1) * r, nblk - 1), 0))
    return prev, nxt


def _inproj(x2, mod_lat, tab, g1, w_lat, w_cx, w_cb, w_cc, w_ga, w_gc, kvg, qg, w_knt, w_v, w_q_p,
            cw, cb, wco):
    n = x2.shape[0]
    tps = SEQ // TM
    row = lambda s, b: (b * tps + s, 0)
    xprev, xnext = _halo_specs(tps, D_MODEL)
    return pl.pallas_call(
        _inproj_kernel,
        name="inproj",
        grid=(tps, BATCH),
        in_specs=[
            pl.BlockSpec((TM, D_MODEL), row),
            xprev,
            xnext,
            pl.BlockSpec((1, 6, D_MODEL), lambda s, b: (b, 0, 0)),
            pl.BlockSpec((3, TM, LANES), lambda s, b: (0, s, 0)),
            _const_spec((1, D_MODEL)),
            _const_spec((D_MODEL, C_LAT)),
            _const_spec((D_MODEL, CONV_DIM)),
            _const_spec((D_MODEL, CONV_DIM)),
            _const_spec((D_MODEL, CONV_DIM)),
            _const_spec((D_MODEL, D_MODEL)),
            _const_spec((D_MODEL, D_MODEL)),
            _const_spec((1, KV_LORA_RANK)),
            _const_spec((1, Q_LORA_RANK)),
            _const_spec((N_HEADS * QK_NOPE_DIM, KV_LORA_RANK)),
            _const_spec((KV_LORA_RANK, ATTN_DIM)),
            _const_spec((Q_LORA_RANK, HEAD_COLS)),
            _const_spec((3, CONV_DIM)),
            _const_spec((1, CONV_DIM)),
            _const_spec((CONV_DIM, D_MODEL)),
        ],
        out_specs=[pl.BlockSpec((TM, HEAD_COLS), row), pl.BlockSpec((HEAD_COLS, TM), lambda s, b: (b, s)),
                   pl.BlockSpec((TM, ATTN_DIM), row), pl.BlockSpec((TM, 2 * D_MODEL), row)],
        out_shape=[jax.ShapeDtypeStruct((n, HEAD_COLS), BF16),
                   jax.ShapeDtypeStruct((BATCH * HEAD_COLS, SEQ), BF16),
                   jax.ShapeDtypeStruct((n, ATTN_DIM), BF16), jax.ShapeDtypeStruct((n, 2 * D_MODEL), BF16)],
        compiler_params=_params(2),
    )(x2, x2, x2, mod_lat, tab, g1, w_lat, w_cx, w_cb, w_cc, w_ga, w_gc, kvg, qg, w_knt, w_v, w_q_p,
      cw, cb, wco)


def _ctxproj_kernel(x_ref, mod_ref, g1_ref, win_ref, kvg_ref, wknt_ref, wv_ref, kt_ref, v_ref):
    sh = mod_ref[0, 0:1, :]
    sc = mod_ref[0, 1:2, :]
    hb = (_rms(x_ref[...], g1_ref[...]) * (1.0 + sc) + sh).astype(BF16)
    pa = _dot(hb, win_ref[...])
    _kv_from(pa, pa[:, C_KR:C_KR + LANES], kvg_ref, wknt_ref, wv_ref, kt_ref, v_ref)


def _ctxproj(c2, mod_ctx, g1, w_lat, kvg, w_knt, w_v):
    n = c2.shape[0]
    tm = CTX_LEN
    row = lambda i: (i, 0)
    return pl.pallas_call(
        _ctxproj_kernel,
        name="ctxproj",
        grid=(n // tm,),
        in_specs=[
            pl.BlockSpec((tm, D_MODEL), row),
            _const_spec((1, 6, D_MODEL)),
            _const_spec((1, D_MODEL)),
            _const_spec((D_MODEL, C_Q)),
            _const_spec((1, KV_LORA_RANK)),
            _const_spec((N_HEADS * QK_NOPE_DIM, KV_LORA_RANK)),
            _const_spec((KV_LORA_RANK, ATTN_DIM)),
        ],
        out_specs=[pl.BlockSpec((HEAD_COLS, tm), row), pl.BlockSpec((tm, ATTN_DIM), row)],
        out_shape=[jax.ShapeDtypeStruct((BATCH * HEAD_COLS, tm), BF16),
                   jax.ShapeDtypeStruct((n, ATTN_DIM), BF16)],
        compiler_params=_params(1),
    )(c2, mod_ctx, g1, w_lat, kvg, w_knt, w_v)


def _attn_kernel(q_ref, klt_ref, vl_ref, kct_ref, vc_ref, o_ref):
    lo = lax.broadcasted_iota(jnp.int32, (TQ, LANES), 1) < HALF
    for r in range(Q_SUB):
        rows = slice(r * TQ, (r + 1) * TQ)
        for hp in range(N_HEADS // 2):
            vsl = slice(hp * LANES, (hp + 1) * LANES)
            accs = []
            for e in range(2):
                h = 2 * hp + e
                sl = slice(h * LANES, (h + 1) * LANES)
                qh = q_ref[rows, sl]
                s_c = _dot(qh, kct_ref[sl, :])
                s_l = _dot(qh, klt_ref[sl, :])
                m = jnp.maximum(jnp.max(s_c, axis=-1, keepdims=True),
                                jnp.max(s_l, axis=-1, keepdims=True))
                p_c = jnp.exp2(s_c - m)
                p_l = jnp.exp2(s_l - m)
                den = jnp.sum(p_c, axis=-1, keepdims=True) + jnp.sum(p_l, axis=-1, keepdims=True)
                acc = _dot(p_c.astype(BF16), vc_ref[:, vsl]) + _dot(p_l.astype(BF16), vl_ref[:, vsl])
                accs.append(acc * (1.0 / den))
            o_ref[rows, vsl] = jnp.where(lo, accs[0], accs[1]).astype(BF16)


def _attn(q, kl, vl, kc, vc):
    tq = TQ * Q_SUB
    nq = SEQ // tq
    return pl.pallas_call(
        _attn_kernel,
        name="attn",
        grid=(BATCH, nq),
        in_specs=[
            pl.BlockSpec((tq, HEAD_COLS), lambda b, j: (b * nq + j, 0)),
            pl.BlockSpec((HEAD_COLS, SEQ), lambda b, j: (b, 0)),
            pl.BlockSpec((SEQ, ATTN_DIM), lambda b, j: (b, 0)),
            pl.BlockSpec((HEAD_COLS, CTX_LEN), lambda b, j: (b, 0)),
            pl.BlockSpec((CTX_LEN, ATTN_DIM), lambda b, j: (b, 0)),
        ],
        out_specs=pl.BlockSpec((tq, ATTN_DIM), lambda b, j: (b * nq + j, 0)),
        out_shape=jax.ShapeDtypeStruct((BATCH * SEQ, ATTN_DIM), BF16),
        compiler_params=_params(2),
    )(q, kl, vl, kc, vc)


def _halo_rows_bf16(prev_ref, next_ref):
    p = prev_ref[...].astype(F32)[HALO_BF16 - HALO:]
    n = next_ref[...].astype(F32)[:HALO]
    return jnp.concatenate([p, n], axis=0).astype(BF16)


def _post_kernel(o_ref, op_ref, on_ref, gg_ref, gp_ref, gn_ref, x_ref, xp_ref, xn_ref, mod_ref,
                 wao_ref, wo_ref, g2_ref, wup_ref, fcw_ref, fcb_ref, wdn_ref, gf_ref,
                 out_ref, act_ref):
    s = pl.program_id(0)
    first = s == 0
    last = s == pl.num_programs(0) - 1
    o = jnp.concatenate([o_ref[...], _halo_rows_bf16(op_ref, on_ref)], axis=0)
    gg = jnp.concatenate([gg_ref[...], _halo_rows_bf16(gp_ref, gn_ref)], axis=0)
    x = jnp.concatenate([x_ref[...], xp_ref[...], xn_ref[...]], axis=0)

    y_attn = _dot(o, wao_ref[...])
    m = (gg[:, :D_MODEL].astype(F32) * y_attn + gg[:, D_MODEL:].astype(F32)).astype(BF16)
    x1 = x + mod_ref[0, 2:3, :] * _dot(m, wo_ref[...])
    h2 = (_rms(x1, g2_ref[...]) * (1.0 + mod_ref[0, 4:5, :]) + mod_ref[0, 3:4, :]).astype(BF16)

    for c in range(D_FF // FF_CHUNK):
        sl = slice(2 * c * FF_CHUNK, 2 * (c + 1) * FF_CHUNK)
        u = _conv3(_dot(h2, wup_ref[:, sl]), TM, first, last, fcw_ref, fcb_ref, sl)
        gate, val = u[:, :FF_CHUNK], u[:, FF_CHUNK:]
        act_ref[:, c * FF_CHUNK:(c + 1) * FF_CHUNK] = (gate * _sigmoid(gate) * val).astype(BF16)

    y = _dot(act_ref[...], wdn_ref[...])
    x2 = x1[:TM] + mod_ref[0, 5:6, :] * y
    out_ref[...] = _rms(x2, gf_ref[...])


def _post(o, gg, x2, mod_lat, wao, wo, g2, wup, fcw, fcb, wdn, gf):
    n = x2.shape[0]
    tps = SEQ // TM
    row = lambda s, b: (b * tps + s, 0)
    oprev, onext = _halo_specs(tps, ATTN_DIM, HALO_BF16)
    gprev, gnext = _halo_specs(tps, 2 * D_MODEL, HALO_BF16)
    xprev, xnext = _halo_specs(tps, D_MODEL)
    return pl.pallas_call(
        _post_kernel,
        name="post",
        grid=(tps, BATCH),
        in_specs=[
            pl.BlockSpec((TM, ATTN_DIM), row), oprev, onext,
            pl.BlockSpec((TM, 2 * D_MODEL), row), gprev, gnext,
            pl.BlockSpec((TM, D_MODEL), row), xprev, xnext,
            pl.BlockSpec((1, 6, D_MODEL), lambda s, b: (b, 0, 0)),
            _const_spec((ATTN_DIM, D_MODEL)),
            _const_spec((D_MODEL, D_MODEL)),
            _const_spec((1, D_MODEL)),
            _const_spec((D_MODEL, 2 * D_FF)),
            _const_spec((3, 2 * D_FF)),
            _const_spec((1, 2 * D_FF)),
            _const_spec((D_FF, D_MODEL)),
            _const_spec((1, D_MODEL)),
        ],
        out_specs=pl.BlockSpec((TM, D_MODEL), row),
        out_shape=jax.ShapeDtypeStruct((n, D_MODEL), F32),
        scratch_shapes=[pltpu.VMEM((TM, D_FF), BF16)],
        compiler_params=_params(2),
    )(o, o, o, gg, gg, gg, x2, x2, x2, mod_lat, wao, wo, g2, wup, fcw, fcb, wdn, gf)


def _rope_tables():
    rows = SEQ // GRID_W
    row = jnp.repeat(jnp.arange(rows), GRID_W).astype(F32)
    col = jnp.tile(jnp.arange(GRID_W), rows).astype(F32)
    axis_dim = QK_ROPE_DIM // 2
    inv = ROPE_THETA ** (-jnp.arange(0, axis_dim, 2, dtype=F32) / axis_dim)
    ang = jnp.concatenate([row[:, None] * inv, col[:, None] * inv], axis=-1)
    j = jnp.arange(QK_ROPE_DIM)
    idx = (j // axis_dim) * ROPE_HALF + j % ROPE_HALF
    first_half = ((j % axis_dim) // ROPE_HALF) == 0
    cos = jnp.cos(ang)[:, idx]
    sin = jnp.sin(ang)[:, idx]
    s0 = jnp.where(first_half, -sin, 0.0)
    s1 = jnp.where(first_half, 0.0, sin)
    pad = jnp.zeros((SEQ, LANES - QK_NOPE_DIM - QK_ROPE_DIM), F32)

    def even(nope_val, rope_part):
        return jnp.concatenate([jnp.full((SEQ, QK_NOPE_DIM), nope_val, F32), rope_part, pad], axis=1)

    return jnp.stack([even(1.0, cos), even(0.0, s0), even(0.0, s1)])


def _pack_w_lat(w):
    z = lambda n: jnp.zeros((D_MODEL, n), BF16)
    kr_end = KV_LORA_RANK + QK_ROPE_DIM
    return jnp.concatenate([w[:, :KV_LORA_RANK].astype(BF16), z(QK_NOPE_DIM),
                            w[:, KV_LORA_RANK:kr_end].astype(BF16),
                            z(LANES - QK_NOPE_DIM - QK_ROPE_DIM),
                            w[:, kr_end:O_CX].astype(BF16)], axis=1)


def _pack_w_ukv(w):
    w = w.astype(BF16).reshape(KV_LORA_RANK, N_HEADS, QK_NOPE_DIM + V_HEAD_DIM)
    return (w[..., :QK_NOPE_DIM].reshape(KV_LORA_RANK, N_HEADS * QK_NOPE_DIM).T,
            w[..., QK_NOPE_DIM:].reshape(KV_LORA_RANK, ATTN_DIM))


def _pack_w_uq(w):
    qscale = (QK_NOPE_DIM + QK_ROPE_DIM) ** -0.5 * LOG2E
    w = (w * qscale).astype(BF16).reshape(Q_LORA_RANK, N_HEADS, QK_NOPE_DIM + QK_ROPE_DIM)
    pad = jnp.zeros((Q_LORA_RANK, N_HEADS, LANES - QK_NOPE_DIM - QK_ROPE_DIM), BF16)
    return jnp.concatenate([w, pad], axis=-1).reshape(Q_LORA_RANK, HEAD_COLS)


def _pack_ffn_cols(a):
    r = a.shape[0]
    a = a.reshape(r, 2, D_FF // FF_CHUNK, FF_CHUNK)
    return jnp.swapaxes(a, 1, 2).reshape(r, 2 * D_FF)


def kernel(x, c, ctx, c_ctx, w_ada, b_ada, norm1_g, w_in, q_norm_g, kv_norm_g, w_uq, w_ukv,
           conv_w, conv_b, w_attn_out, w_conv_out, w_o, norm2_g, w_up, ffn_conv_w, ffn_conv_b,
           w_down, final_g):
    assert x.shape == (BATCH, SEQ, D_MODEL) and ctx.shape == (BATCH, CTX_LEN, D_MODEL)
    assert w_ada.shape[0] == 1, "single-layer block"

    cvec = jnp.concatenate([c, c_ctx[None, :], jnp.zeros((16 - BATCH - 1, D_MODEL), F32)], axis=0)
    mod = _ada(cvec, w_ada.reshape(D_MODEL, 6 * D_MODEL), b_ada.reshape(1, 6 * D_MODEL))
    mod_lat = mod[:BATCH].reshape(BATCH, 6, D_MODEL)
    mod_ctx = mod[BATCH:BATCH + 1].reshape(1, 6, D_MODEL)

    tab = _rope_tables()
    w_in2 = w_in.reshape(D_MODEL, O_END)
    w_lat = _pack_w_lat(w_in2)
    w_cx, w_cb, w_cc, w_ga, w_gc = (w_in2[:, a:b].astype(BF16) for a, b in
                                    ((O_CX, O_CB), (O_CB, O_CC), (O_CC, O_GA), (O_GA, O_GC), (O_GC, O_END)))
    w_knt, w_v = _pack_w_ukv(w_ukv.reshape(KV_LORA_RANK, -1))
    w_q_p = _pack_w_uq(w_uq.reshape(Q_LORA_RANK, -1))
    g1 = norm1_g.reshape(1, D_MODEL)
    kvg = kv_norm_g.reshape(1, KV_LORA_RANK)
    qg = q_norm_g.reshape(1, Q_LORA_RANK)

    x2 = x.reshape(BATCH * SEQ, D_MODEL)
    c2 = ctx.reshape(BATCH * CTX_LEN, D_MODEL)

    kc, vc = _ctxproj(c2, mod_ctx, g1, w_lat, kvg, w_knt, w_v)
    q, kl, vl, gg = _inproj(x2, mod_lat, tab, g1, w_lat, w_cx, w_cb, w_cc, w_ga, w_gc, kvg, qg,
                            w_knt, w_v, w_q_p,
                            conv_w.reshape(3, CONV_DIM), conv_b.reshape(1, CONV_DIM),
                            w_conv_out.reshape(CONV_DIM, D_MODEL).astype(BF16))
    o = _attn(q, kl, vl, kc, vc)
    out = _post(o, gg, x2, mod_lat,
                w_attn_out.reshape(ATTN_DIM, D_MODEL).astype(BF16),
                w_o.reshape(D_MODEL, D_MODEL).astype(BF16), norm2_g.reshape(1, D_MODEL),
                _pack_ffn_cols(w_up.reshape(D_MODEL, 2 * D_FF).astype(BF16)),
                _pack_ffn_cols(ffn_conv_w.reshape(3, 2 * D_FF)),
                _pack_ffn_cols(ffn_conv_b.reshape(1, 2 * D_FF)),
                w_down.reshape(D_FF, D_MODEL).astype(BF16),
                final_g.reshape(1, D_MODEL))
    return out.reshape(BATCH, SEQ, D_MODEL)
```

```python
import functools

import jax
import jax.numpy as jnp
from jax import lax
from jax.experimental import pallas as pl
from jax.experimental.pallas import tpu as pltpu

D_MODEL = 1024
BATCH = 8
SEQ = 2048
GRID_W = 64
CTX_LEN = 256
N_HEADS = 8
QK_NOPE_DIM = 64
QK_ROPE_DIM = 32
V_HEAD_DIM = 64
Q_LORA_RANK = 384
KV_LORA_RANK = 256
ROPE_THETA = 10000.0
CONV_DIM = 512
D_FF = 2816
EPS = 1e-6
ATTN_DIM = N_HEADS * V_HEAD_DIM

LANES = 128
HALF = LANES // 2
HEAD_COLS = N_HEADS * LANES
ROPE_HALF = QK_ROPE_DIM // 4
HALO = 8
HALO_BF16 = 16

C_KV = 0
C_KR = C_KV + KV_LORA_RANK
C_Q = C_KR + LANES
C_LAT = C_Q + Q_LORA_RANK
O_CX = KV_LORA_RANK + QK_ROPE_DIM + Q_LORA_RANK
O_CB = O_CX + CONV_DIM
O_CC = O_CB + CONV_DIM
O_GA = O_CC + CONV_DIM
O_GC = O_GA + D_MODEL
O_END = O_GC + D_MODEL

LOG2E = 1.4426950408889634

VMEM_LIMIT = 56 * 1024 * 1024

TM = 512
TQ = 512
FF_CHUNK = 256

BF16 = jnp.bfloat16
F32 = jnp.float32


def _dot(a, b):
    return jnp.dot(a, b, preferred_element_type=F32)


def _dot_nt(a, b):
    return lax.dot_general(a, b, (((1,), (1,)), ((), ())), preferred_element_type=F32)


def _rms(x, g):
    return x * lax.rsqrt(jnp.mean(x * x, axis=-1, keepdims=True) + EPS) * g


def _sigmoid(x):
    return 1.0 / (1.0 + jnp.exp(-x))


def _rope_block(xh, c, s0, s1):
    return (xh * c + pltpu.roll(xh, LANES - ROPE_HALF, 1) * s0
            + pltpu.roll(xh, ROPE_HALF, 1) * s1)


def _conv3(u_all, m, first, last, w_ref, b_ref, sl):
    u = u_all[:m]
    prev_row = jnp.where(first, 0.0, u_all[m + HALO - 1:m + HALO])
    next_row = jnp.where(last, 0.0, u_all[m + HALO:m + HALO + 1])
    row = lax.broadcasted_iota(jnp.int32, u.shape, 0)
    u_dn = jnp.where(row == 0, prev_row, pltpu.roll(u, 1, 0))
    u_up = jnp.where(row == m - 1, next_row, pltpu.roll(u, m - 1, 0))
    return b_ref[:, sl] + u_dn * w_ref[0:1, sl] + u * w_ref[1:2, sl] + u_up * w_ref[2:3, sl]


def _const_spec(shape):
    nd = len(shape)
    return pl.BlockSpec(shape, lambda *_: (0,) * nd, pipeline_mode=pl.Buffered(1))


def _params(n_grid):
    return pltpu.CompilerParams(dimension_semantics=("arbitrary",) * n_grid,
                                vmem_limit_bytes=VMEM_LIMIT)


def _ada_kernel(c_ref, w_ref, b_ref, o_ref):
    c = c_ref[...]
    s = (c * _sigmoid(c)).astype(BF16)
    o_ref[...] = _dot(s, w_ref[...].astype(BF16)) + b_ref[...]


def _ada(cvec, w_ada, b_ada):
    rows = cvec.shape[0]
    n = w_ada.shape[1]
    tn = 1024
    return pl.pallas_call(
        _ada_kernel,
        name="ada",
        grid=(n // tn,),
        in_specs=[pl.BlockSpec((rows, D_MODEL), lambda j: (0, 0)),
                  pl.BlockSpec((D_MODEL, tn), lambda j: (0, j)),
                  pl.BlockSpec((1, tn), lambda j: (0, j))],
        out_specs=pl.BlockSpec((rows, tn), lambda j: (0, j)),
        out_shape=jax.ShapeDtypeStruct((rows, n), F32),
        compiler_params=_params(1),
    )(cvec, w_ada, b_ada)


def _kv_from(pa, kr_even, kvg_ref, wkv_ref, k_ref, v_ref):
    ckv = _rms(pa[:, C_KV:C_KV + KV_LORA_RANK], kvg_ref[...]).astype(BF16)
    kvv = _dot(ckv, wkv_ref[...])
    kr_odd = pltpu.roll(kr_even, HALF, 1)
    lo = lax.broadcasted_iota(jnp.int32, kr_even.shape, 1) < HALF
    for p in range(N_HEADS // 2):
        kn_pair = kvv[:, p * LANES:(p + 1) * LANES]
        k_ref[:, (2 * p) * LANES:(2 * p + 1) * LANES] = jnp.where(lo, kn_pair, kr_even).astype(BF16)
        k_ref[:, (2 * p + 1) * LANES:(2 * p + 2) * LANES] = jnp.where(lo, kr_odd, kn_pair).astype(BF16)
    v_ref[...] = kvv[:, ATTN_DIM:].astype(BF16)


def _inproj_kernel(x_ref, xp_ref, xn_ref, mod_ref, tab_ref, g1_ref, wlat_ref, wcx_ref, wcb_ref,
                   wcc_ref, wga_ref, wgc_ref, kvg_ref, qg_ref, wkv_ref, wq_ref, cw_ref, cb_ref,
                   wco_ref, q_ref, k_ref, v_ref, gg_ref):
    s = pl.program_id(0)
    first = s == 0
    last = s == pl.num_programs(0) - 1
    sh = mod_ref[0, 0:1, :]
    sc = mod_ref[0, 1:2, :]
    g1 = g1_ref[...]
    xcat = jnp.concatenate([x_ref[...], xp_ref[...], xn_ref[...]], axis=0)
    hcat = (_rms(xcat, g1) * (1.0 + sc) + sh).astype(BF16)
    hb = hcat[:TM]

    pa = _dot(hb, wlat_ref[...])
    x_in = _dot(hcat, wcx_ref[...])
    b_gate = _dot(hb, wcb_ref[...])
    c_gate = _dot(hcat, wcc_ref[...])

    tabs = [[tab_ref[0], tab_ref[1], tab_ref[2]]]
    tabs.append([pltpu.roll(t, HALF, 1) for t in tabs[0]])
    kr = _rope_block(pa[:, C_KR:C_KR + LANES], *tabs[0])
    _kv_from(pa, kr, kvg_ref, wkv_ref, k_ref, v_ref)
    cq = _rms(pa[:, C_Q:C_Q + Q_LORA_RANK], qg_ref[...]).astype(BF16)
    qf = _dot(cq, wq_ref[...])
    for h in range(N_HEADS):
        sl = slice(h * LANES, (h + 1) * LANES)
        q_ref[:, sl] = _rope_block(qf[:, sl], *tabs[h % 2]).astype(BF16)

    gg_ref[:, :D_MODEL] = _sigmoid(_dot(hb, wga_ref[...])).astype(BF16)

    conv = _conv3(c_gate * x_in, TM, first, last, cw_ref, cb_ref, slice(0, CONV_DIM))
    y_conv = _dot((b_gate * conv).astype(BF16), wco_ref[...])
    gg_ref[:, D_MODEL:] = (_sigmoid(_dot(hb, wgc_ref[...])) * y_conv).astype(BF16)


def _halo_specs(tps, cols, rows=HALO):
    r = TM // rows
    nblk = BATCH * SEQ // rows
    prev = pl.BlockSpec((rows, cols), lambda s, b: (jnp.maximum((b * tps + s) * r - 1, 0), 0))
    nxt = pl.BlockSpec((rows, cols), lambda s, b: (jnp.minimum((b * tps + s + 1) * r, nblk - 1), 0))
    return prev, nxt


def _inproj(x2, mod_lat, tab, g1, w_lat, w_cx, w_cb, w_cc, w_ga, w_gc, kvg, qg, w_kv_p, w_q_p,
            cw, cb, wco):
    n = x2.shape[0]
    tps = SEQ // TM
    row = lambda s, b: (b * tps + s, 0)
    xprev, xnext = _halo_specs(tps, D_MODEL)
    return pl.pallas_call(
        _inproj_kernel,
        name="inproj",
        grid=(tps, BATCH),
        in_specs=[
            pl.BlockSpec((TM, D_MODEL), row),
            xprev,
            xnext,
            pl.BlockSpec((1, 6, D_MODEL), lambda s, b: (b, 0, 0)),
            pl.BlockSpec((3, TM, LANES), lambda s, b: (0, s, 0)),
            _const_spec((1, D_MODEL)),
            _const_spec((D_MODEL, C_LAT)),
            _const_spec((D_MODEL, CONV_DIM)),
            _const_spec((D_MODEL, CONV_DIM)),
            _const_spec((D_MODEL, CONV_DIM)),
            _const_spec((D_MODEL, D_MODEL)),
            _const_spec((D_MODEL, D_MODEL)),
            _const_spec((1, KV_LORA_RANK)),
            _const_spec((1, Q_LORA_RANK)),
            _const_spec((KV_LORA_RANK, 2 * ATTN_DIM)),
            _const_spec((Q_LORA_RANK, HEAD_COLS)),
            _const_spec((3, CONV_DIM)),
            _const_spec((1, CONV_DIM)),
            _const_spec((CONV_DIM, D_MODEL)),
        ],
        out_specs=[pl.BlockSpec((TM, HEAD_COLS), row), pl.BlockSpec((TM, HEAD_COLS), row),
                   pl.BlockSpec((TM, ATTN_DIM), row), pl.BlockSpec((TM, 2 * D_MODEL), row)],
        out_shape=[jax.ShapeDtypeStruct((n, HEAD_COLS), BF16), jax.ShapeDtypeStruct((n, HEAD_COLS), BF16),
                   jax.ShapeDtypeStruct((n, ATTN_DIM), BF16), jax.ShapeDtypeStruct((n, 2 * D_MODEL), BF16)],
        compiler_params=_params(2),
    )(x2, x2, x2, mod_lat, tab, g1, w_lat, w_cx, w_cb, w_cc, w_ga, w_gc, kvg, qg, w_kv_p, w_q_p,
      cw, cb, wco)


def _ctxproj_kernel(x_ref, mod_ref, g1_ref, win_ref, kvg_ref, wkv_ref, k_ref, v_ref):
    sh = mod_ref[0, 0:1, :]
    sc = mod_ref[0, 1:2, :]
    hb = (_rms(x_ref[...], g1_ref[...]) * (1.0 + sc) + sh).astype(BF16)
    pa = _dot(hb, win_ref[...])
    _kv_from(pa, pa[:, C_KR:C_KR + LANES], kvg_ref, wkv_ref, k_ref, v_ref)


def _ctxproj(c2, mod_ctx, g1, w_lat, kvg, w_kv_p):
    n = c2.shape[0]
    tm = CTX_LEN
    row = lambda i: (i, 0)
    return pl.pallas_call(
        _ctxproj_kernel,
        name="ctxproj",
        grid=(n // tm,),
        in_specs=[
            pl.BlockSpec((tm, D_MODEL), row),
            _const_spec((1, 6, D_MODEL)),
            _const_spec((1, D_MODEL)),
            _const_spec((D_MODEL, C_Q)),
            _const_spec((1, KV_LORA_RANK)),
            _const_spec((KV_LORA_RANK, 2 * ATTN_DIM)),
        ],
        out_specs=[pl.BlockSpec((tm, HEAD_COLS), row), pl.BlockSpec((tm, ATTN_DIM), row)],
        out_shape=[jax.ShapeDtypeStruct((n, HEAD_COLS), BF16), jax.ShapeDtypeStruct((n, ATTN_DIM), BF16)],
        compiler_params=_params(1),
    )(c2, mod_ctx, g1, w_lat, kvg, w_kv_p)


def _attn_unit(qh, kc, kl, vc, vl):
    s_c = _dot_nt(qh, kc)
    s_l = _dot_nt(qh, kl)
    m = jnp.maximum(jnp.max(s_c, axis=-1, keepdims=True), jnp.max(s_l, axis=-1, keepdims=True))
    p_c = jnp.exp2(s_c - m)
    p_l = jnp.exp2(s_l - m)
    den = jnp.sum(p_c, axis=-1, keepdims=True) + jnp.sum(p_l, axis=-1, keepdims=True)
    acc = _dot(p_c.astype(BF16), vc) + _dot(p_l.astype(BF16), vl)
    return acc * (1.0 / den)


def _attn_kernel(q_ref, kl_ref, vl_ref, kc_ref, vc_ref, o_ref):
    lo = lax.broadcasted_iota(jnp.int32, (TQ, LANES), 1) < HALF
    accs = []
    for h in range(N_HEADS):
        sl = slice(h * LANES, (h + 1) * LANES)
        vsl = slice((h // 2) * LANES, (h // 2 + 1) * LANES)
        nsplit = 2 if h in (0, N_HEADS - 1) else 1
        rows = TQ // nsplit
        parts = [_attn_unit(q_ref[r * rows:(r + 1) * rows, sl], kc_ref[:, sl], kl_ref[:, sl],
                            vc_ref[:, vsl], vl_ref[:, vsl]) for r in range(nsplit)]
        accs.append(parts[0] if nsplit == 1 else jnp.concatenate(parts, axis=0))
        if h % 2 == 1:
            o_ref[:, vsl] = jnp.where(lo, accs[h - 1], accs[h]).astype(BF16)


def _attn(q, kl, vl, kc, vc):
    nq = SEQ // TQ
    return pl.pallas_call(
        _attn_kernel,
        name="attn",
        grid=(BATCH, nq),
        in_specs=[
            pl.BlockSpec((TQ, HEAD_COLS), lambda b, j: (b * nq + j, 0)),
            pl.BlockSpec((SEQ, HEAD_COLS), lambda b, j: (b, 0)),
            pl.BlockSpec((SEQ, ATTN_DIM), lambda b, j: (b, 0)),
            pl.BlockSpec((CTX_LEN, HEAD_COLS), lambda b, j: (b, 0)),
            pl.BlockSpec((CTX_LEN, ATTN_DIM), lambda b, j: (b, 0)),
        ],
        out_specs=pl.BlockSpec((TQ, ATTN_DIM), lambda b, j: (b * nq + j, 0)),
        out_shape=jax.ShapeDtypeStruct((BATCH * SEQ, ATTN_DIM), BF16),
        compiler_params=_params(2),
    )(q, kl, vl, kc, vc)


def _halo_rows_bf16(prev_ref, next_ref):
    p = prev_ref[...].astype(F32)[HALO_BF16 - HALO:]
    n = next_ref[...].astype(F32)[:HALO]
    return jnp.concatenate([p, n], axis=0).astype(BF16)


def _post_kernel(o_ref, op_ref, on_ref, gg_ref, gp_ref, gn_ref, x_ref, xp_ref, xn_ref, mod_ref,
                 wao_ref, wo_ref, g2_ref, wup_ref, fcw_ref, fcb_ref, wdn_ref, gf_ref,
                 out_ref, h2_ref, act_ref):
    s = pl.program_id(0)
    first = s == 0
    last = s == pl.num_programs(0) - 1
    o = jnp.concatenate([o_ref[...], _halo_rows_bf16(op_ref, on_ref)], axis=0)
    gg = jnp.concatenate([gg_ref[...], _halo_rows_bf16(gp_ref, gn_ref)], axis=0)
    x = jnp.concatenate([x_ref[...], xp_ref[...], xn_ref[...]], axis=0)

    y_attn = _dot(o, wao_ref[...])
    m = (gg[:, :D_MODEL].astype(F32) * y_attn + gg[:, D_MODEL:].astype(F32)).astype(BF16)
    x1 = x + mod_ref[0, 2:3, :] * _dot(m, wo_ref[...])
    h2_ref[...] = (_rms(x1, g2_ref[...]) * (1.0 + mod_ref[0, 4:5, :]) + mod_ref[0, 3:4, :]).astype(BF16)

    for c in range(D_FF // FF_CHUNK):
        gsl = slice(c * FF_CHUNK, (c + 1) * FF_CHUNK)
        vsl = slice(D_FF + c * FF_CHUNK, D_FF + (c + 1) * FF_CHUNK)
        gate = _conv3(_dot(h2_ref[...], wup_ref[:, gsl]), TM, first, last, fcw_ref, fcb_ref, gsl)
        val = _conv3(_dot(h2_ref[...], wup_ref[:, vsl]), TM, first, last, fcw_ref, fcb_ref, vsl)
        act_ref[:, gsl] = (gate * _sigmoid(gate) * val).astype(BF16)

    y = _dot(act_ref[...], wdn_ref[...])
    x2 = x1[:TM] + mod_ref[0, 5:6, :] * y
    out_ref[...] = _rms(x2, gf_ref[...])


def _post(o, gg, x2, mod_lat, wao, wo, g2, wup, fcw, fcb, wdn, gf):
    n = x2.shape[0]
    tps = SEQ // TM
    row = lambda s, b: (b * tps + s, 0)
    oprev, onext = _halo_specs(tps, ATTN_DIM, HALO_BF16)
    gprev, gnext = _halo_specs(tps, 2 * D_MODEL, HALO_BF16)
    xprev, xnext = _halo_specs(tps, D_MODEL)
    return pl.pallas_call(
        _post_kernel,
        name="post",
        grid=(tps, BATCH),
        in_specs=[
            pl.BlockSpec((TM, ATTN_DIM), row), oprev, onext,
            pl.BlockSpec((TM, 2 * D_MODEL), row), gprev, gnext,
            pl.BlockSpec((TM, D_MODEL), row), xprev, xnext,
            pl.BlockSpec((1, 6, D_MODEL), lambda s, b: (b, 0, 0)),
            _const_spec((ATTN_DIM, D_MODEL)),
            _const_spec((D_MODEL, D_MODEL)),
            _const_spec((1, D_MODEL)),
            _const_spec((D_MODEL, 2 * D_FF)),
            _const_spec((3, 2 * D_FF)),
            _const_spec((1, 2 * D_FF)),
            _const_spec((D_FF, D_MODEL)),
            _const_spec((1, D_MODEL)),
        ],
        out_specs=pl.BlockSpec((TM, D_MODEL), row),
        out_shape=jax.ShapeDtypeStruct((n, D_MODEL), F32),
        scratch_shapes=[pltpu.VMEM((TM + 2 * HALO, D_MODEL), BF16), pltpu.VMEM((TM, D_FF), BF16)],
        compiler_params=_params(2),
    )(o, o, o, gg, gg, gg, x2, x2, x2, mod_lat, wao, wo, g2, wup, fcw, fcb, wdn, gf)


def _rope_tables():
    rows = SEQ // GRID_W
    row = jnp.repeat(jnp.arange(rows), GRID_W).astype(F32)
    col = jnp.tile(jnp.arange(GRID_W), rows).astype(F32)
    axis_dim = QK_ROPE_DIM // 2
    inv = ROPE_THETA ** (-jnp.arange(0, axis_dim, 2, dtype=F32) / axis_dim)
    ang = jnp.concatenate([row[:, None] * inv, col[:, None] * inv], axis=-1)
    j = jnp.arange(QK_ROPE_DIM)
    idx = (j // axis_dim) * ROPE_HALF + j % ROPE_HALF
    first_half = ((j % axis_dim) // ROPE_HALF) == 0
    cos = jnp.cos(ang)[:, idx]
    sin = jnp.sin(ang)[:, idx]
    s0 = jnp.where(first_half, -sin, 0.0)
    s1 = jnp.where(first_half, 0.0, sin)
    pad = jnp.zeros((SEQ, LANES - QK_NOPE_DIM - QK_ROPE_DIM), F32)

    def even(nope_val, rope_part):
        return jnp.concatenate([jnp.full((SEQ, QK_NOPE_DIM), nope_val, F32), rope_part, pad], axis=1)

    return jnp.stack([even(1.0, cos), even(0.0, s0), even(0.0, s1)])


def _pack_w_lat(w):
    z = lambda n: jnp.zeros((D_MODEL, n), BF16)
    kr_end = KV_LORA_RANK + QK_ROPE_DIM
    return jnp.concatenate([w[:, :KV_LORA_RANK].astype(BF16), z(QK_NOPE_DIM),
                            w[:, KV_LORA_RANK:kr_end].astype(BF16),
                            z(LANES - QK_NOPE_DIM - QK_ROPE_DIM),
                            w[:, kr_end:O_CX].astype(BF16)], axis=1)


def _pack_w_ukv(w):
    w = w.astype(BF16).reshape(KV_LORA_RANK, N_HEADS, QK_NOPE_DIM + V_HEAD_DIM)
    return jnp.concatenate([w[..., :QK_NOPE_DIM].reshape(KV_LORA_RANK, N_HEADS * QK_NOPE_DIM),
                            w[..., QK_NOPE_DIM:].reshape(KV_LORA_RANK, ATTN_DIM)], axis=1)


def _pack_w_uq(w):
    qscale = (QK_NOPE_DIM + QK_ROPE_DIM) ** -0.5 * LOG2E
    w = (w * qscale).astype(BF16).reshape(Q_LORA_RANK, N_HEADS // 2, 2, QK_NOPE_DIM + QK_ROPE_DIM)
    pad = jnp.zeros((Q_LORA_RANK, N_HEADS // 2, LANES - QK_NOPE_DIM - QK_ROPE_DIM), BF16)
    ev, od = w[:, :, 0], w[:, :, 1]
    ev = jnp.concatenate([ev, pad], axis=-1)
    od = jnp.concatenate([od[..., QK_NOPE_DIM:], pad, od[..., :QK_NOPE_DIM]], axis=-1)
    return jnp.stack([ev, od], axis=2).reshape(Q_LORA_RANK, HEAD_COLS)


def kernel(x, c, ctx, c_ctx, w_ada, b_ada, norm1_g, w_in, q_norm_g, kv_norm_g, w_uq, w_ukv,
           conv_w, conv_b, w_attn_out, w_conv_out, w_o, norm2_g, w_up, ffn_conv_w, ffn_conv_b,
           w_down, final_g):
    assert x.shape == (BATCH, SEQ, D_MODEL) and ctx.shape == (BATCH, CTX_LEN, D_MODEL)
    assert w_ada.shape[0] == 1, "single-layer block"

    cvec = jnp.concatenate([c, c_ctx[None, :], jnp.zeros((16 - BATCH - 1, D_MODEL), F32)], axis=0)
    mod = _ada(cvec, w_ada.reshape(D_MODEL, 6 * D_MODEL), b_ada.reshape(1, 6 * D_MODEL))
    mod_lat = mod[:BATCH].reshape(BATCH, 6, D_MODEL)
    mod_ctx = mod[BATCH:BATCH + 1].reshape(1, 6, D_MODEL)

    tab = _rope_tables()
    w_in2 = w_in.reshape(D_MODEL, O_END)
    w_lat = _pack_w_lat(w_in2)
    w_cx, w_cb, w_cc, w_ga, w_gc = (w_in2[:, a:b].astype(BF16) for a, b in
                                    ((O_CX, O_CB), (O_CB, O_CC), (O_CC, O_GA), (O_GA, O_GC), (O_GC, O_END)))
    w_kv_p = _pack_w_ukv(w_ukv.reshape(KV_LORA_RANK, -1))
    w_q_p = _pack_w_uq(w_uq.reshape(Q_LORA_RANK, -1))
    g1 = norm1_g.reshape(1, D_MODEL)
    kvg = kv_norm_g.reshape(1, KV_LORA_RANK)
    qg = q_norm_g.reshape(1, Q_LORA_RANK)

    x2 = x.reshape(BATCH * SEQ, D_MODEL)
    c2 = ctx.reshape(BATCH * CTX_LEN, D_MODEL)

    kc, vc = _ctxproj(c2, mod_ctx, g1, w_lat, kvg, w_kv_p)
    q, kl, vl, gg = _inproj(x2, mod_lat, tab, g1, w_lat, w_cx, w_cb, w_cc, w_ga, w_gc, kvg, qg,
                            w_kv_p, w_q_p,
                            conv_w.reshape(3, CONV_DIM), conv_b.reshape(1, CONV_DIM),
                            w_conv_out.reshape(CONV_DIM, D_MODEL).astype(BF16))
    o = _attn(q, kl, vl, kc, vc)
    out = _post(o, gg, x2, mod_lat,
                w_attn_out.reshape(ATTN_DIM, D_MODEL).astype(BF16),
                w_o.reshape(D_MODEL, D_MODEL).astype(BF16), norm2_g.reshape(1, D_MODEL),
                w_up.reshape(D_MODEL, 2 * D_FF).astype(BF16), ffn_conv_w.reshape(3, 2 * D_FF),
                ffn_conv_b.reshape(1, 2 * D_FF), w_down.reshape(D_FF, D_MODEL).astype(BF16),
                final_g.reshape(1, D_MODEL))
    return out.reshape(BATCH, SEQ, D_MODEL)
```

```python
import functools

import jax
import jax.numpy as jnp
from jax import lax
from jax.experimental import pallas as pl
from jax.experimental.pallas import tpu as pltpu

D_MODEL = 1024
BATCH = 8
SEQ = 2048
GRID_W = 64
CTX_LEN = 256
N_HEADS = 8
QK_NOPE_DIM = 64
QK_ROPE_DIM = 32
V_HEAD_DIM = 64
Q_LORA_RANK = 384
KV_LORA_RANK = 256
ROPE_THETA = 10000.0
CONV_DIM = 512
D_FF = 2816
EPS = 1e-6
ATTN_DIM = N_HEADS * V_HEAD_DIM

LANES = 128
HALF = LANES // 2
HEAD_COLS = N_HEADS * LANES
ROPE_HALF = QK_ROPE_DIM // 4
HALO = 8
HALO_BF16 = 16
CAST_ROWS = 16

C_KV = 0
C_KR = C_KV + KV_LORA_RANK
C_Q = C_KR + LANES
C_LAT = C_Q + Q_LORA_RANK
O_CX = KV_LORA_RANK + QK_ROPE_DIM + Q_LORA_RANK
O_CB = O_CX + CONV_DIM
O_CC = O_CB + CONV_DIM
O_GA = O_CC + CONV_DIM
O_GC = O_GA + D_MODEL
O_END = O_GC + D_MODEL

LOG2E = 1.4426950408889634

VMEM_LIMIT = 56 * 1024 * 1024

TM = 512
TQ = 512
FF_CHUNK = 256

BF16 = jnp.bfloat16
F32 = jnp.float32


def _dot(a, b):
    return jnp.dot(a, b, preferred_element_type=F32)


def _dot_nt(a, b):
    return lax.dot_general(a, b, (((1,), (1,)), ((), ())), preferred_element_type=F32)


def _rms(x, g):
    return x * lax.rsqrt(jnp.mean(x * x, axis=-1, keepdims=True) + EPS) * g


def _sigmoid(x):
    return 1.0 / (1.0 + jnp.exp(-x))


def _rope_block(xh, c, s0, s1):
    return (xh * c + pltpu.roll(xh, LANES - ROPE_HALF, 1) * s0
            + pltpu.roll(xh, ROPE_HALF, 1) * s1)


def _conv3(u_all, m, first, last, w_ref, b_ref, sl):
    u = u_all[:m]
    ext = jnp.concatenate([jnp.where(first, 0.0, u_all[m:m + HALO]), u,
                           jnp.where(last, 0.0, u_all[m + HALO:])], axis=0)
    u_dn = pltpu.roll(ext, 1, 0)[HALO:HALO + m]
    u_up = pltpu.roll(ext, m + 2 * HALO - 1, 0)[HALO:HALO + m]
    return b_ref[:, sl] + u_dn * w_ref[0:1, sl] + u * w_ref[1:2, sl] + u_up * w_ref[2:3, sl]


def _const_spec(shape):
    nd = len(shape)
    return pl.BlockSpec(shape, lambda *_: (0,) * nd, pipeline_mode=pl.Buffered(1))


def _params(n_grid):
    return pltpu.CompilerParams(dimension_semantics=("arbitrary",) * n_grid,
                                vmem_limit_bytes=VMEM_LIMIT)


def _ada_kernel(c_ref, w_ref, b_ref, o_ref):
    c = c_ref[...]
    s = (c * _sigmoid(c)).astype(BF16)
    o_ref[...] = _dot(s, w_ref[...].astype(BF16)) + b_ref[...]


def _ada(cvec, w_ada, b_ada):
    rows = cvec.shape[0]
    n = w_ada.shape[1]
    tn = 1024
    return pl.pallas_call(
        _ada_kernel,
        name="ada",
        grid=(n // tn,),
        in_specs=[pl.BlockSpec((rows, D_MODEL), lambda j: (0, 0)),
                  pl.BlockSpec((D_MODEL, tn), lambda j: (0, j)),
                  pl.BlockSpec((1, tn), lambda j: (0, j))],
        out_specs=pl.BlockSpec((rows, tn), lambda j: (0, j)),
        out_shape=jax.ShapeDtypeStruct((rows, n), F32),
        compiler_params=_params(1),
    )(cvec, w_ada, b_ada)


def _kv_from(pa, kr_even, kvg_ref, wkv_ref, k_ref, v_ref):
    ckv = _rms(pa[:, C_KV:C_KV + KV_LORA_RANK], kvg_ref[...]).astype(BF16)
    kvv = _dot(ckv, wkv_ref[...])
    kr_odd = pltpu.roll(kr_even, HALF, 1)
    lo = lax.broadcasted_iota(jnp.int32, kr_even.shape, 1) < HALF
    for p in range(N_HEADS // 2):
        kn_pair = kvv[:, p * LANES:(p + 1) * LANES]
        k_ref[:, (2 * p) * LANES:(2 * p + 1) * LANES] = jnp.where(lo, kn_pair, kr_even).astype(BF16)
        k_ref[:, (2 * p + 1) * LANES:(2 * p + 2) * LANES] = jnp.where(lo, kr_odd, kn_pair).astype(BF16)
    v_ref[...] = kvv[:, ATTN_DIM:].astype(BF16)


def _inproj_kernel(x_ref, xp_ref, xn_ref, mod_ref, tab_ref, g1_ref, wlat_ref, wcx_ref, wcb_ref,
                   wcc_ref, wga_ref, wgc_ref, kvg_ref, qg_ref, wkv_ref, wq_ref, cw_ref, cb_ref,
                   wco_ref, q_ref, k_ref, v_ref, gg_ref):
    s = pl.program_id(0)
    first = s == 0
    last = s == pl.num_programs(0) - 1
    sh = mod_ref[0, 0:1, :]
    sc = mod_ref[0, 1:2, :]
    g1 = g1_ref[...]
    xcat = jnp.concatenate([x_ref[...], xp_ref[...], xn_ref[...]], axis=0)
    hcat = (_rms(xcat, g1) * (1.0 + sc) + sh).astype(BF16)
    hb = hcat[:TM]

    pa = _dot(hb, wlat_ref[...])
    x_in = _dot(hcat, wcx_ref[...])
    b_gate = _dot(hb, wcb_ref[...])
    c_gate = _dot(hcat, wcc_ref[...])

    tabs = [[tab_ref[0], tab_ref[1], tab_ref[2]]]
    tabs.append([pltpu.roll(t, HALF, 1) for t in tabs[0]])
    kr = _rope_block(pa[:, C_KR:C_KR + LANES], *tabs[0])
    _kv_from(pa, kr, kvg_ref, wkv_ref, k_ref, v_ref)
    cq = _rms(pa[:, C_Q:C_Q + Q_LORA_RANK], qg_ref[...]).astype(BF16)
    qf = _dot(cq, wq_ref[...])
    for h in range(N_HEADS):
        sl = slice(h * LANES, (h + 1) * LANES)
        q_ref[:, sl] = _rope_block(qf[:, sl], *tabs[h % 2]).astype(BF16)

    gg_ref[:, :D_MODEL] = _sigmoid(_dot(hb, wga_ref[...])).astype(BF16)

    conv = _conv3(c_gate * x_in, TM, first, last, cw_ref, cb_ref, slice(0, CONV_DIM))
    y_conv = _dot((b_gate * conv).astype(BF16), wco_ref[...])
    gg_ref[:, D_MODEL:] = (_sigmoid(_dot(hb, wgc_ref[...])) * y_conv).astype(BF16)


def _halo_specs(tps, cols, rows=HALO):
    r = TM // rows
    nblk = BATCH * SEQ // rows
    prev = pl.BlockSpec((rows, cols), lambda s, b: (jnp.maximum((b * tps + s) * r - 1, 0), 0))
    nxt = pl.BlockSpec((rows, cols), lambda s, b: (jnp.minimum((b * tps + s + 1) * r, nblk - 1), 0))
    return prev, nxt


def _inproj(x2, mod_lat, tab, g1, w_lat, w_cx, w_cb, w_cc, w_ga, w_gc, kvg, qg, w_kv_p, w_q_p,
            cw, cb, wco):
    n = x2.shape[0]
    tps = SEQ // TM
    row = lambda s, b: (b * tps + s, 0)
    xprev, xnext = _halo_specs(tps, D_MODEL)
    return pl.pallas_call(
        _inproj_kernel,
        name="inproj",
        grid=(tps, BATCH),
        in_specs=[
            pl.BlockSpec((TM, D_MODEL), row),
            xprev,
            xnext,
            pl.BlockSpec((1, 6, D_MODEL), lambda s, b: (b, 0, 0)),
            pl.BlockSpec((3, TM, LANES), lambda s, b: (0, s, 0)),
            _const_spec((1, D_MODEL)),
            _const_spec((D_MODEL, C_LAT)),
            _const_spec((D_MODEL, CONV_DIM)),
            _const_spec((D_MODEL, CONV_DIM)),
            _const_spec((D_MODEL, CONV_DIM)),
            _const_spec((D_MODEL, D_MODEL)),
            _const_spec((D_MODEL, D_MODEL)),
            _const_spec((1, KV_LORA_RANK)),
            _const_spec((1, Q_LORA_RANK)),
            _const_spec((KV_LORA_RANK, 2 * ATTN_DIM)),
            _const_spec((Q_LORA_RANK, HEAD_COLS)),
            _const_spec((3, CONV_DIM)),
            _const_spec((1, CONV_DIM)),
            _const_spec((CONV_DIM, D_MODEL)),
        ],
        out_specs=[pl.BlockSpec((TM, HEAD_COLS), row), pl.BlockSpec((TM, HEAD_COLS), row),
                   pl.BlockSpec((TM, ATTN_DIM), row), pl.BlockSpec((TM, 2 * D_MODEL), row)],
        out_shape=[jax.ShapeDtypeStruct((n, HEAD_COLS), BF16), jax.ShapeDtypeStruct((n, HEAD_COLS), BF16),
                   jax.ShapeDtypeStruct((n, ATTN_DIM), BF16), jax.ShapeDtypeStruct((n, 2 * D_MODEL), BF16)],
        compiler_params=_params(2),
    )(x2, x2, x2, mod_lat, tab, g1, w_lat, w_cx, w_cb, w_cc, w_ga, w_gc, kvg, qg, w_kv_p, w_q_p,
      cw, cb, wco)


def _ctxproj_kernel(x_ref, mod_ref, g1_ref, win_ref, kvg_ref, wkv_ref, k_ref, v_ref):
    sh = mod_ref[0, 0:1, :]
    sc = mod_ref[0, 1:2, :]
    hb = (_rms(x_ref[...], g1_ref[...]) * (1.0 + sc) + sh).astype(BF16)
    pa = _dot(hb, win_ref[...])
    _kv_from(pa, pa[:, C_KR:C_KR + LANES], kvg_ref, wkv_ref, k_ref, v_ref)


def _ctxproj(c2, mod_ctx, g1, w_lat, kvg, w_kv_p):
    n = c2.shape[0]
    tm = CTX_LEN
    row = lambda i: (i, 0)
    return pl.pallas_call(
        _ctxproj_kernel,
        name="ctxproj",
        grid=(n // tm,),
        in_specs=[
            pl.BlockSpec((tm, D_MODEL), row),
            _const_spec((1, 6, D_MODEL)),
            _const_spec((1, D_MODEL)),
            _const_spec((D_MODEL, C_Q)),
            _const_spec((1, KV_LORA_RANK)),
            _const_spec((KV_LORA_RANK, 2 * ATTN_DIM)),
        ],
        out_specs=[pl.BlockSpec((tm, HEAD_COLS), row), pl.BlockSpec((tm, ATTN_DIM), row)],
        out_shape=[jax.ShapeDtypeStruct((n, HEAD_COLS), BF16), jax.ShapeDtypeStruct((n, ATTN_DIM), BF16)],
        compiler_params=_params(1),
    )(c2, mod_ctx, g1, w_lat, kvg, w_kv_p)


def _attn_unit(qh, kc, kl, vc, vl):
    s_c = _dot_nt(qh, kc)
    s_l = _dot_nt(qh, kl)
    m = jnp.maximum(jnp.max(s_c, axis=-1, keepdims=True), jnp.max(s_l, axis=-1, keepdims=True))
    p_c = jnp.exp2(s_c - m)
    p_l = jnp.exp2(s_l - m)
    den = jnp.sum(p_c, axis=-1, keepdims=True) + jnp.sum(p_l, axis=-1, keepdims=True)
    acc = _dot(p_c.astype(BF16), vc) + _dot(p_l.astype(BF16), vl)
    return acc * (1.0 / den)


def _attn_kernel(q_ref, kl_ref, vl_ref, kc_ref, vc_ref, *rest):
    n_cast = len(rest) // 2
    for src, dst in zip(rest[:n_cast], rest[n_cast + 1:]):
        dst[...] = src[...].astype(BF16)
    o_ref = rest[n_cast]
    lo = lax.broadcasted_iota(jnp.int32, (TQ, LANES), 1) < HALF
    accs = []
    for h in range(N_HEADS):
        sl = slice(h * LANES, (h + 1) * LANES)
        vsl = slice((h // 2) * LANES, (h // 2 + 1) * LANES)
        nsplit = 2 if h in (0, N_HEADS - 1) else 1
        rows = TQ // nsplit
        parts = [_attn_unit(q_ref[r * rows:(r + 1) * rows, sl], kc_ref[:, sl], kl_ref[:, sl],
                            vc_ref[:, vsl], vl_ref[:, vsl]) for r in range(nsplit)]
        accs.append(parts[0] if nsplit == 1 else jnp.concatenate(parts, axis=0))
        if h % 2 == 1:
            o_ref[:, vsl] = jnp.where(lo, accs[h - 1], accs[h]).astype(BF16)


def _attn(q, kl, vl, kc, vc, cast_weights):
    nq = SEQ // TQ
    steps = BATCH * nq
    cast_specs = []
    for w in cast_weights:
        rows, cols = w.shape
        blk = next(r for r in range(CAST_ROWS, rows + 1, CAST_ROWS)
                   if rows % r == 0 and r * steps >= rows)
        nblk = rows // blk
        cast_specs.append(pl.BlockSpec((blk, cols), functools.partial(
            lambda b, j, nblk: (jnp.minimum(b * nq + j, nblk - 1), 0), nblk=nblk)))
    outs = pl.pallas_call(
        _attn_kernel,
        name="attn",
        grid=(BATCH, nq),
        in_specs=[
            pl.BlockSpec((TQ, HEAD_COLS), lambda b, j: (b * nq + j, 0)),
            pl.BlockSpec((SEQ, HEAD_COLS), lambda b, j: (b, 0)),
            pl.BlockSpec((SEQ, ATTN_DIM), lambda b, j: (b, 0)),
            pl.BlockSpec((CTX_LEN, HEAD_COLS), lambda b, j: (b, 0)),
            pl.BlockSpec((CTX_LEN, ATTN_DIM), lambda b, j: (b, 0)),
        ] + cast_specs,
        out_specs=[pl.BlockSpec((TQ, ATTN_DIM), lambda b, j: (b * nq + j, 0))] + cast_specs,
        out_shape=[jax.ShapeDtypeStruct((BATCH * SEQ, ATTN_DIM), BF16)]
                  + [jax.ShapeDtypeStruct(w.shape, BF16) for w in cast_weights],
        compiler_params=_params(2),
    )(q, kl, vl, kc, vc, *cast_weights)
    return outs[0], outs[1:]


def _halo_rows_bf16(prev_ref, next_ref):
    p = prev_ref[...].astype(F32)[HALO_BF16 - HALO:]
    n = next_ref[...].astype(F32)[:HALO]
    return jnp.concatenate([p, n], axis=0).astype(BF16)


def _post_kernel(o_ref, op_ref, on_ref, gg_ref, gp_ref, gn_ref, x_ref, xp_ref, xn_ref, mod_ref,
                 wao_ref, wo_ref, g2_ref, wup_ref, fcw_ref, fcb_ref, wdn_ref, gf_ref,
                 out_ref, act_ref):
    s = pl.program_id(0)
    first = s == 0
    last = s == pl.num_programs(0) - 1
    o = jnp.concatenate([o_ref[...], _halo_rows_bf16(op_ref, on_ref)], axis=0)
    gg = jnp.concatenate([gg_ref[...], _halo_rows_bf16(gp_ref, gn_ref)], axis=0)
    x = jnp.concatenate([x_ref[...], xp_ref[...], xn_ref[...]], axis=0)

    y_attn = _dot(o, wao_ref[...])
    m = (gg[:, :D_MODEL].astype(F32) * y_attn + gg[:, D_MODEL:].astype(F32)).astype(BF16)
    x1 = x + mod_ref[0, 2:3, :] * _dot(m, wo_ref[...])
    h2 = (_rms(x1, g2_ref[...]) * (1.0 + mod_ref[0, 4:5, :]) + mod_ref[0, 3:4, :]).astype(BF16)

    for c in range(D_FF // FF_CHUNK):
        gsl = slice(c * FF_CHUNK, (c + 1) * FF_CHUNK)
        vsl = slice(D_FF + c * FF_CHUNK, D_FF + (c + 1) * FF_CHUNK)
        gate = _conv3(_dot(h2, wup_ref[:, gsl]), TM, first, last, fcw_ref, fcb_ref, gsl)
        val = _conv3(_dot(h2, wup_ref[:, vsl]), TM, first, last, fcw_ref, fcb_ref, vsl)
        act_ref[:, gsl] = (gate * _sigmoid(gate) * val).astype(BF16)

    y = _dot(act_ref[...], wdn_ref[...])
    x2 = x1[:TM] + mod_ref[0, 5:6, :] * y
    out_ref[...] = _rms(x2, gf_ref[...])


def _post(o, gg, x2, mod_lat, wao, wo, g2, wup, fcw, fcb, wdn, gf):
    n = x2.shape[0]
    tps = SEQ // TM
    row = lambda s, b: (b * tps + s, 0)
    oprev, onext = _halo_specs(tps, ATTN_DIM, HALO_BF16)
    gprev, gnext = _halo_specs(tps, 2 * D_MODEL, HALO_BF16)
    xprev, xnext = _halo_specs(tps, D_MODEL)
    return pl.pallas_call(
        _post_kernel,
        name="post",
        grid=(tps, BATCH),
        in_specs=[
            pl.BlockSpec((TM, ATTN_DIM), row), oprev, onext,
            pl.BlockSpec((TM, 2 * D_MODEL), row), gprev, gnext,
            pl.BlockSpec((TM, D_MODEL), row), xprev, xnext,
            pl.BlockSpec((1, 6, D_MODEL), lambda s, b: (b, 0, 0)),
            _const_spec((ATTN_DIM, D_MODEL)),
            _const_spec((D_MODEL, D_MODEL)),
            _const_spec((1, D_MODEL)),
            _const_spec((D_MODEL, 2 * D_FF)),
            _const_spec((3, 2 * D_FF)),
            _const_spec((1, 2 * D_FF)),
            _const_spec((D_FF, D_MODEL)),
            _const_spec((1, D_MODEL)),
        ],
        out_specs=pl.BlockSpec((TM, D_MODEL), row),
        out_shape=jax.ShapeDtypeStruct((n, D_MODEL), F32),
        scratch_shapes=[pltpu.VMEM((TM, D_FF), BF16)],
        compiler_params=_params(2),
    )(o, o, o, gg, gg, gg, x2, x2, x2, mod_lat, wao, wo, g2, wup, fcw, fcb, wdn, gf)


def _rope_tables():
    rows = SEQ // GRID_W
    row = jnp.repeat(jnp.arange(rows), GRID_W).astype(F32)
    col = jnp.tile(jnp.arange(GRID_W), rows).astype(F32)
    axis_dim = QK_ROPE_DIM // 2
    inv = ROPE_THETA ** (-jnp.arange(0, axis_dim, 2, dtype=F32) / axis_dim)
    ang = jnp.concatenate([row[:, None] * inv, col[:, None] * inv], axis=-1)
    j = jnp.arange(QK_ROPE_DIM)
    idx = (j // axis_dim) * ROPE_HALF + j % ROPE_HALF
    first_half = ((j % axis_dim) // ROPE_HALF) == 0
    cos = jnp.cos(ang)[:, idx]
    sin = jnp.sin(ang)[:, idx]
    s0 = jnp.where(first_half, -sin, 0.0)
    s1 = jnp.where(first_half, 0.0, sin)
    pad = jnp.zeros((SEQ, LANES - QK_NOPE_DIM - QK_ROPE_DIM), F32)

    def even(nope_val, rope_part):
        return jnp.concatenate([jnp.full((SEQ, QK_NOPE_DIM), nope_val, F32), rope_part, pad], axis=1)

    return jnp.stack([even(1.0, cos), even(0.0, s0), even(0.0, s1)])


def _split_w_in_kernel(w_ref, lat_ref, cx_ref, cb_ref, cc_ref, ga_ref, gc_ref):
    rows = w_ref.shape[0]
    kr_end = KV_LORA_RANK + QK_ROPE_DIM
    z = lambda n: jnp.zeros((rows, n), F32)
    lat = jnp.concatenate([w_ref[:, :KV_LORA_RANK], z(QK_NOPE_DIM), w_ref[:, KV_LORA_RANK:kr_end],
                           z(LANES - QK_NOPE_DIM - QK_ROPE_DIM), w_ref[:, kr_end:O_CX]], axis=1)
    lat_ref[...] = lat.astype(BF16)
    for ref, (a, b) in ((cx_ref, (O_CX, O_CB)), (cb_ref, (O_CB, O_CC)), (cc_ref, (O_CC, O_GA)),
                        (ga_ref, (O_GA, O_GC)), (gc_ref, (O_GC, O_END))):
        ref[...] = w_ref[:, a:b].astype(BF16)


def _split_w_in(w):
    rows = 128
    widths = (C_LAT, CONV_DIM, CONV_DIM, CONV_DIM, D_MODEL, D_MODEL)
    return pl.pallas_call(
        _split_w_in_kernel,
        name="split_w_in",
        grid=(D_MODEL // rows,),
        in_specs=[pl.BlockSpec((rows, O_END), lambda i: (i, 0))],
        out_specs=[pl.BlockSpec((rows, n), lambda i: (i, 0)) for n in widths],
        out_shape=[jax.ShapeDtypeStruct((D_MODEL, n), BF16) for n in widths],
        compiler_params=_params(1),
    )(w)


def _pack_w_ukv(w):
    w = w.astype(BF16).reshape(KV_LORA_RANK, N_HEADS, QK_NOPE_DIM + V_HEAD_DIM)
    return jnp.concatenate([w[..., :QK_NOPE_DIM].reshape(KV_LORA_RANK, N_HEADS * QK_NOPE_DIM),
                            w[..., QK_NOPE_DIM:].reshape(KV_LORA_RANK, ATTN_DIM)], axis=1)


def _pack_w_uq(w):
    qscale = (QK_NOPE_DIM + QK_ROPE_DIM) ** -0.5 * LOG2E
    w = (w * qscale).astype(BF16).reshape(Q_LORA_RANK, N_HEADS // 2, 2, QK_NOPE_DIM + QK_ROPE_DIM)
    pad = jnp.zeros((Q_LORA_RANK, N_HEADS // 2, LANES - QK_NOPE_DIM - QK_ROPE_DIM), BF16)
    ev, od = w[:, :, 0], w[:, :, 1]
    ev = jnp.concatenate([ev, pad], axis=-1)
    od = jnp.concatenate([od[..., QK_NOPE_DIM:], pad, od[..., :QK_NOPE_DIM]], axis=-1)
    return jnp.stack([ev, od], axis=2).reshape(Q_LORA_RANK, HEAD_COLS)


def kernel(x, c, ctx, c_ctx, w_ada, b_ada, norm1_g, w_in, q_norm_g, kv_norm_g, w_uq, w_ukv,
           conv_w, conv_b, w_attn_out, w_conv_out, w_o, norm2_g, w_up, ffn_conv_w, ffn_conv_b,
           w_down, final_g):
    assert x.shape == (BATCH, SEQ, D_MODEL) and ctx.shape == (BATCH, CTX_LEN, D_MODEL)
    assert w_ada.shape[0] == 1, "single-layer block"

    cvec = jnp.concatenate([c, c_ctx[None, :], jnp.zeros((16 - BATCH - 1, D_MODEL), F32)], axis=0)
    mod = _ada(cvec, w_ada.reshape(D_MODEL, 6 * D_MODEL), b_ada.reshape(1, 6 * D_MODEL))
    mod_lat = mod[:BATCH].reshape(BATCH, 6, D_MODEL)
    mod_ctx = mod[BATCH:BATCH + 1].reshape(1, 6, D_MODEL)

    tab = _rope_tables()
    w_lat, w_cx, w_cb, w_cc, w_ga, w_gc = _split_w_in(w_in.reshape(D_MODEL, O_END))
    w_kv_p = _pack_w_ukv(w_ukv.reshape(KV_LORA_RANK, -1))
    w_q_p = _pack_w_uq(w_uq.reshape(Q_LORA_RANK, -1))
    g1 = norm1_g.reshape(1, D_MODEL)
    kvg = kv_norm_g.reshape(1, KV_LORA_RANK)
    qg = q_norm_g.reshape(1, Q_LORA_RANK)

    x2 = x.reshape(BATCH * SEQ, D_MODEL)
    c2 = ctx.reshape(BATCH * CTX_LEN, D_MODEL)

    kc, vc = _ctxproj(c2, mod_ctx, g1, w_lat, kvg, w_kv_p)
    q, kl, vl, gg = _inproj(x2, mod_lat, tab, g1, w_lat, w_cx, w_cb, w_cc, w_ga, w_gc, kvg, qg,
                            w_kv_p, w_q_p,
                            conv_w.reshape(3, CONV_DIM), conv_b.reshape(1, CONV_DIM),
                            w_conv_out.reshape(CONV_DIM, D_MODEL).astype(BF16))
    o, (wao, wo, wup, wdn) = _attn(q, kl, vl, kc, vc, (
        w_attn_out.reshape(ATTN_DIM, D_MODEL), w_o.reshape(D_MODEL, D_MODEL),
        w_up.reshape(D_MODEL, 2 * D_FF), w_down.reshape(D_FF, D_MODEL)))
    out = _post(o, gg, x2, mod_lat, wao, wo, norm2_g.reshape(1, D_MODEL), wup,
                ffn_conv_w.reshape(3, 2 * D_FF), ffn_conv_b.reshape(1, 2 * D_FF), wdn,
                final_g.reshape(1, D_MODEL))
    return out.reshape(BATCH, SEQ, D_MODEL)
```

```python
import functools

import jax
import jax.numpy as jnp
from jax import lax
from jax.experimental import pallas as pl
from jax.experimental.pallas import tpu as pltpu

D_MODEL = 1024
BATCH = 8
SEQ = 2048
GRID_W = 64
CTX_LEN = 256
N_HEADS = 8
QK_NOPE_DIM = 64
QK_ROPE_DIM = 32
V_HEAD_DIM = 64
Q_LORA_RANK = 384
KV_LORA_RANK = 256
ROPE_THETA = 10000.0
CONV_DIM = 512
D_FF = 2816
EPS = 1e-6
ATTN_DIM = N_HEADS * V_HEAD_DIM

LANES = 128
HALF = LANES // 2
HEAD_COLS = N_HEADS * LANES
ROPE_HALF = QK_ROPE_DIM // 4
HALO = 8
HALO_BF16 = 16
CAST_ROWS = 16

C_KV = 0
C_KR = C_KV + KV_LORA_RANK
C_Q = C_KR + LANES
C_LAT = C_Q + Q_LORA_RANK
O_CX = KV_LORA_RANK + QK_ROPE_DIM + Q_LORA_RANK
O_CB = O_CX + CONV_DIM
O_CC = O_CB + CONV_DIM
O_GA = O_CC + CONV_DIM
O_GC = O_GA + D_MODEL
O_END = O_GC + D_MODEL

LOG2E = 1.4426950408889634

VMEM_LIMIT = 56 * 1024 * 1024

TM = 512
TQ = 512
FF_CHUNK = 256

BF16 = jnp.bfloat16
F32 = jnp.float32


def _dot(a, b):
    return jnp.dot(a, b, preferred_element_type=F32)


def _dot_nt(a, b):
    return lax.dot_general(a, b, (((1,), (1,)), ((), ())), preferred_element_type=F32)


def _rms(x, g):
    return x * lax.rsqrt(jnp.mean(x * x, axis=-1, keepdims=True) + EPS) * g


def _sigmoid(x):
    return 1.0 / (1.0 + jnp.exp(-x))


def _rope_block(xh, c, s0, s1):
    return (xh * c + pltpu.roll(xh, LANES - ROPE_HALF, 1) * s0
            + pltpu.roll(xh, ROPE_HALF, 1) * s1)


def _conv3(u_all, m, first, last, w_ref, b_ref, sl):
    u = u_all[:m]
    ext = jnp.concatenate([jnp.where(first, 0.0, u_all[m:m + HALO]), u,
                           jnp.where(last, 0.0, u_all[m + HALO:])], axis=0)
    u_dn = pltpu.roll(ext, 1, 0)[HALO:HALO + m]
    u_up = pltpu.roll(ext, m + 2 * HALO - 1, 0)[HALO:HALO + m]
    return b_ref[:, sl] + u_dn * w_ref[0:1, sl] + u * w_ref[1:2, sl] + u_up * w_ref[2:3, sl]


def _const_spec(shape):
    nd = len(shape)
    return pl.BlockSpec(shape, lambda *_: (0,) * nd, pipeline_mode=pl.Buffered(1))


def _params(n_grid):
    return pltpu.CompilerParams(dimension_semantics=("arbitrary",) * n_grid,
                                vmem_limit_bytes=VMEM_LIMIT)


def _ada_kernel(c_ref, w_ref, b_ref, o_ref):
    c = c_ref[...]
    s = (c * _sigmoid(c)).astype(BF16)
    o_ref[...] = _dot(s, w_ref[...].astype(BF16)) + b_ref[...]


def _ada(cvec, w_ada, b_ada):
    rows = cvec.shape[0]
    n = w_ada.shape[1]
    tn = 1024
    return pl.pallas_call(
        _ada_kernel,
        name="ada",
        grid=(n // tn,),
        in_specs=[pl.BlockSpec((rows, D_MODEL), lambda j: (0, 0)),
                  pl.BlockSpec((D_MODEL, tn), lambda j: (0, j)),
                  pl.BlockSpec((1, tn), lambda j: (0, j))],
        out_specs=pl.BlockSpec((rows, tn), lambda j: (0, j)),
        out_shape=jax.ShapeDtypeStruct((rows, n), F32),
        compiler_params=_params(1),
    )(cvec, w_ada, b_ada)


def _kv_from(pa, kr_even, kvg_ref, wkv_ref, k_ref, v_ref):
    ckv = _rms(pa[:, C_KV:C_KV + KV_LORA_RANK], kvg_ref[...]).astype(BF16)
    kvv = _dot(ckv, wkv_ref[...])
    kr_odd = pltpu.roll(kr_even, HALF, 1)
    lo = lax.broadcasted_iota(jnp.int32, kr_even.shape, 1) < HALF
    for p in range(N_HEADS // 2):
        kn_pair = kvv[:, p * LANES:(p + 1) * LANES]
        k_ref[:, (2 * p) * LANES:(2 * p + 1) * LANES] = jnp.where(lo, kn_pair, kr_even).astype(BF16)
        k_ref[:, (2 * p + 1) * LANES:(2 * p + 2) * LANES] = jnp.where(lo, kr_odd, kn_pair).astype(BF16)
    v_ref[...] = kvv[:, ATTN_DIM:].astype(BF16)


def _inproj_kernel(x_ref, xp_ref, xn_ref, mod_ref, tab_ref, g1_ref, wlat_ref, wcx_ref, wcb_ref,
                   wcc_ref, wga_ref, wgc_ref, kvg_ref, qg_ref, wkv_ref, wq_ref, cw_ref, cb_ref,
                   wco_ref, q_ref, k_ref, v_ref, gg_ref):
    s = pl.program_id(0)
    first = s == 0
    last = s == pl.num_programs(0) - 1
    sh = mod_ref[0, 0:1, :]
    sc = mod_ref[0, 1:2, :]
    g1 = g1_ref[...]
    xcat = jnp.concatenate([x_ref[...], xp_ref[...], xn_ref[...]], axis=0)
    hcat = (_rms(xcat, g1) * (1.0 + sc) + sh).astype(BF16)
    hb = hcat[:TM]

    pa = _dot(hb, wlat_ref[...])
    x_in = _dot(hcat, wcx_ref[...])
    b_gate = _dot(hb, wcb_ref[...])
    c_gate = _dot(hcat, wcc_ref[...])

    tabs = [[tab_ref[0], tab_ref[1], tab_ref[2]]]
    tabs.append([pltpu.roll(t, HALF, 1) for t in tabs[0]])
    kr = _rope_block(pa[:, C_KR:C_KR + LANES], *tabs[0])
    _kv_from(pa, kr, kvg_ref, wkv_ref, k_ref, v_ref)
    cq = _rms(pa[:, C_Q:C_Q + Q_LORA_RANK], qg_ref[...]).astype(BF16)
    qf = _dot(cq, wq_ref[...])
    for h in range(N_HEADS):
        sl = slice(h * LANES, (h + 1) * LANES)
        q_ref[:, sl] = _rope_block(qf[:, sl], *tabs[h % 2]).astype(BF16)

    gg_ref[:, :D_MODEL] = _sigmoid(_dot(hb, wga_ref[...])).astype(BF16)

    conv = _conv3(c_gate * x_in, TM, first, last, cw_ref, cb_ref, slice(0, CONV_DIM))
    y_conv = _dot((b_gate * conv).astype(BF16), wco_ref[...])
    gg_ref[:, D_MODEL:] = (_sigmoid(_dot(hb, wgc_ref[...])) * y_conv).astype(BF16)


def _halo_specs(tps, cols, rows=HALO):
    r = TM // rows
    nblk = BATCH * SEQ // rows
    prev = pl.BlockSpec((rows, cols), lambda s, b: (jnp.maximum((b * tps + s) * r - 1, 0), 0))
    nxt = pl.BlockSpec((rows, cols), lambda s, b: (jnp.minimum((b * tps + s + 1) * r, nblk - 1), 0))
    return prev, nxt


def _inproj(x2, mod_lat, tab, g1, w_lat, w_cx, w_cb, w_cc, w_ga, w_gc, kvg, qg, w_kv_p, w_q_p,
            cw, cb, wco):
    n = x2.shape[0]
    tps = SEQ // TM
    row = lambda s, b: (b * tps + s, 0)
    xprev, xnext = _halo_specs(tps, D_MODEL)
    return pl.pallas_call(
        _inproj_kernel,
        name="inproj",
        grid=(tps, BATCH),
        in_specs=[
            pl.BlockSpec((TM, D_MODEL), row),
            xprev,
            xnext,
            pl.BlockSpec((1, 6, D_MODEL), lambda s, b: (b, 0, 0)),
            pl.BlockSpec((3, TM, LANES), lambda s, b: (0, s, 0)),
            _const_spec((1, D_MODEL)),
            _const_spec((D_MODEL, C_LAT)),
            _const_spec((D_MODEL, CONV_DIM)),
            _const_spec((D_MODEL, CONV_DIM)),
            _const_spec((D_MODEL, CONV_DIM)),
            _const_spec((D_MODEL, D_MODEL)),
            _const_spec((D_MODEL, D_MODEL)),
            _const_spec((1, KV_LORA_RANK)),
            _const_spec((1, Q_LORA_RANK)),
            _const_spec((KV_LORA_RANK, 2 * ATTN_DIM)),
            _const_spec((Q_LORA_RANK, HEAD_COLS)),
            _const_spec((3, CONV_DIM)),
            _const_spec((1, CONV_DIM)),
            _const_spec((CONV_DIM, D_MODEL)),
        ],
        out_specs=[pl.BlockSpec((TM, HEAD_COLS), row), pl.BlockSpec((TM, HEAD_COLS), row),
                   pl.BlockSpec((TM, ATTN_DIM), row), pl.BlockSpec((TM, 2 * D_MODEL), row)],
        out_shape=[jax.ShapeDtypeStruct((n, HEAD_COLS), BF16), jax.ShapeDtypeStruct((n, HEAD_COLS), BF16),
                   jax.ShapeDtypeStruct((n, ATTN_DIM), BF16), jax.ShapeDtypeStruct((n, 2 * D_MODEL), BF16)],
        compiler_params=_params(2),
    )(x2, x2, x2, mod_lat, tab, g1, w_lat, w_cx, w_cb, w_cc, w_ga, w_gc, kvg, qg, w_kv_p, w_q_p,
      cw, cb, wco)


def _ctxproj_kernel(x_ref, mod_ref, g1_ref, win_ref, kvg_ref, wkv_ref, k_ref, v_ref):
    sh = mod_ref[0, 0:1, :]
    sc = mod_ref[0, 1:2, :]
    hb = (_rms(x_ref[...], g1_ref[...]) * (1.0 + sc) + sh).astype(BF16)
    pa = _dot(hb, win_ref[...])
    _kv_from(pa, pa[:, C_KR:C_KR + LANES], kvg_ref, wkv_ref, k_ref, v_ref)


def _ctxproj(c2, mod_ctx, g1, w_lat, kvg, w_kv_p):
    n = c2.shape[0]
    tm = CTX_LEN
    row = lambda i: (i, 0)
    return pl.pallas_call(
        _ctxproj_kernel,
        name="ctxproj",
        grid=(n // tm,),
        in_specs=[
            pl.BlockSpec((tm, D_MODEL), row),
            _const_spec((1, 6, D_MODEL)),
            _const_spec((1, D_MODEL)),
            _const_spec((D_MODEL, C_Q)),
            _const_spec((1, KV_LORA_RANK)),
            _const_spec((KV_LORA_RANK, 2 * ATTN_DIM)),
        ],
        out_specs=[pl.BlockSpec((tm, HEAD_COLS), row), pl.BlockSpec((tm, ATTN_DIM), row)],
        out_shape=[jax.ShapeDtypeStruct((n, HEAD_COLS), BF16), jax.ShapeDtypeStruct((n, ATTN_DIM), BF16)],
        compiler_params=_params(1),
    )(c2, mod_ctx, g1, w_lat, kvg, w_kv_p)


def _attn_unit(qh, kc, kl, vc, vl):
    s_c = _dot_nt(qh, kc)
    s_l = _dot_nt(qh, kl)
    m = jnp.maximum(jnp.max(s_c, axis=-1, keepdims=True), jnp.max(s_l, axis=-1, keepdims=True))
    p_c = jnp.exp2(s_c - m)
    p_l = jnp.exp2(s_l - m)
    den = jnp.sum(p_c, axis=-1, keepdims=True) + jnp.sum(p_l, axis=-1, keepdims=True)
    acc = _dot(p_c.astype(BF16), vc) + _dot(p_l.astype(BF16), vl)
    return acc * (1.0 / den)


def _attn_kernel(q_ref, kl_ref, vl_ref, kc_ref, vc_ref, *rest):
    n_cast = len(rest) // 2
    for src, dst in zip(rest[:n_cast], rest[n_cast + 1:]):
        dst[...] = src[...].astype(BF16)
    o_ref = rest[n_cast]
    lo = lax.broadcasted_iota(jnp.int32, (TQ, LANES), 1) < HALF
    accs = []
    for h in range(N_HEADS):
        sl = slice(h * LANES, (h + 1) * LANES)
        vsl = slice((h // 2) * LANES, (h // 2 + 1) * LANES)
        nsplit = 2 if h in (0, N_HEADS - 1) else 1
        rows = TQ // nsplit
        parts = [_attn_unit(q_ref[r * rows:(r + 1) * rows, sl], kc_ref[:, sl], kl_ref[:, sl],
                            vc_ref[:, vsl], vl_ref[:, vsl]) for r in range(nsplit)]
        accs.append(parts[0] if nsplit == 1 else jnp.concatenate(parts, axis=0))
        if h % 2 == 1:
            o_ref[:, vsl] = jnp.where(lo, accs[h - 1], accs[h]).astype(BF16)


def _attn(q, kl, vl, kc, vc, cast_weights):
    nq = SEQ // TQ
    steps = BATCH * nq
    cast_specs = []
    for w in cast_weights:
        rows, cols = w.shape
        blk = next(r for r in range(CAST_ROWS, rows + 1, CAST_ROWS)
                   if rows % r == 0 and r * steps >= rows)
        nblk = rows // blk
        cast_specs.append(pl.BlockSpec((blk, cols), functools.partial(
            lambda b, j, nblk: (jnp.minimum(b * nq + j, nblk - 1), 0), nblk=nblk)))
    outs = pl.pallas_call(
        _attn_kernel,
        name="attn",
        grid=(BATCH, nq),
        in_specs=[
            pl.BlockSpec((TQ, HEAD_COLS), lambda b, j: (b * nq + j, 0)),
            pl.BlockSpec((SEQ, HEAD_COLS), lambda b, j: (b, 0)),
            pl.BlockSpec((SEQ, ATTN_DIM), lambda b, j: (b, 0)),
            pl.BlockSpec((CTX_LEN, HEAD_COLS), lambda b, j: (b, 0)),
            pl.BlockSpec((CTX_LEN, ATTN_DIM), lambda b, j: (b, 0)),
        ] + cast_specs,
        out_specs=[pl.BlockSpec((TQ, ATTN_DIM), lambda b, j: (b * nq + j, 0))] + cast_specs,
        out_shape=[jax.ShapeDtypeStruct((BATCH * SEQ, ATTN_DIM), BF16)]
                  + [jax.ShapeDtypeStruct(w.shape, BF16) for w in cast_weights],
        compiler_params=_params(2),
    )(q, kl, vl, kc, vc, *cast_weights)
    return outs[0], outs[1:]


def _halo_rows_bf16(prev_ref, next_ref):
    p = prev_ref[...].astype(F32)[HALO_BF16 - HALO:]
    n = next_ref[...].astype(F32)[:HALO]
    return jnp.concatenate([p, n], axis=0).astype(BF16)


def _post_kernel(o_ref, op_ref, on_ref, gg_ref, gp_ref, gn_ref, x_ref, xp_ref, xn_ref, mod_ref,
                 wao_ref, wo_ref, g2_ref, wup_ref, fcw_ref, fcb_ref, wdn_ref, gf_ref,
                 out_ref, act_ref):
    s = pl.program_id(0)
    first = s == 0
    last = s == pl.num_programs(0) - 1
    o = jnp.concatenate([o_ref[...], _halo_rows_bf16(op_ref, on_ref)], axis=0)
    gg = jnp.concatenate([gg_ref[...], _halo_rows_bf16(gp_ref, gn_ref)], axis=0)
    x = jnp.concatenate([x_ref[...], xp_ref[...], xn_ref[...]], axis=0)

    y_attn = _dot(o, wao_ref[...])
    m = (gg[:, :D_MODEL].astype(F32) * y_attn + gg[:, D_MODEL:].astype(F32)).astype(BF16)
    x1 = x + mod_ref[0, 2:3, :] * _dot(m, wo_ref[...])
    h2 = (_rms(x1, g2_ref[...]) * (1.0 + mod_ref[0, 4:5, :]) + mod_ref[0, 3:4, :]).astype(BF16)

    for c in range(D_FF // FF_CHUNK):
        gsl = slice(c * FF_CHUNK, (c + 1) * FF_CHUNK)
        vsl = slice(D_FF + c * FF_CHUNK, D_FF + (c + 1) * FF_CHUNK)
        gate = _conv3(_dot(h2, wup_ref[:, gsl]), TM, first, last, fcw_ref, fcb_ref, gsl)
        val = _conv3(_dot(h2, wup_ref[:, vsl]), TM, first, last, fcw_ref, fcb_ref, vsl)
        act_ref[:, gsl] = (gate * _sigmoid(gate) * val).astype(BF16)

    y = _dot(act_ref[...], wdn_ref[...])
    x2 = x1[:TM] + mod_ref[0, 5:6, :] * y
    out_ref[...] = _rms(x2, gf_ref[...])


def _post(o, gg, x2, mod_lat, wao, wo, g2, wup, fcw, fcb, wdn, gf):
    n = x2.shape[0]
    tps = SEQ // TM
    row = lambda s, b: (b * tps + s, 0)
    oprev, onext = _halo_specs(tps, ATTN_DIM, HALO_BF16)
    gprev, gnext = _halo_specs(tps, 2 * D_MODEL, HALO_BF16)
    xprev, xnext = _halo_specs(tps, D_MODEL)
    return pl.pallas_call(
        _post_kernel,
        name="post",
        grid=(tps, BATCH),
        in_specs=[
            pl.BlockSpec((TM, ATTN_DIM), row), oprev, onext,
            pl.BlockSpec((TM, 2 * D_MODEL), row), gprev, gnext,
            pl.BlockSpec((TM, D_MODEL), row), xprev, xnext,
            pl.BlockSpec((1, 6, D_MODEL), lambda s, b: (b, 0, 0)),
            _const_spec((ATTN_DIM, D_MODEL)),
            _const_spec((D_MODEL, D_MODEL)),
            _const_spec((1, D_MODEL)),
            _const_spec((D_MODEL, 2 * D_FF)),
            _const_spec((3, 2 * D_FF)),
            _const_spec((1, 2 * D_FF)),
            _const_spec((D_FF, D_MODEL)),
            _const_spec((1, D_MODEL)),
        ],
        out_specs=pl.BlockSpec((TM, D_MODEL), row),
        out_shape=jax.ShapeDtypeStruct((n, D_MODEL), F32),
        scratch_shapes=[pltpu.VMEM((TM, D_FF), BF16)],
        compiler_params=_params(2),
    )(o, o, o, gg, gg, gg, x2, x2, x2, mod_lat, wao, wo, g2, wup, fcw, fcb, wdn, gf)


def _rope_tables():
    rows = SEQ // GRID_W
    row = jnp.repeat(jnp.arange(rows), GRID_W).astype(F32)
    col = jnp.tile(jnp.arange(GRID_W), rows).astype(F32)
    axis_dim = QK_ROPE_DIM // 2
    inv = ROPE_THETA ** (-jnp.arange(0, axis_dim, 2, dtype=F32) / axis_dim)
    ang = jnp.concatenate([row[:, None] * inv, col[:, None] * inv], axis=-1)
    j = jnp.arange(QK_ROPE_DIM)
    idx = (j // axis_dim) * ROPE_HALF + j % ROPE_HALF
    first_half = ((j % axis_dim) // ROPE_HALF) == 0
    cos = jnp.cos(ang)[:, idx]
    sin = jnp.sin(ang)[:, idx]
    s0 = jnp.where(first_half, -sin, 0.0)
    s1 = jnp.where(first_half, 0.0, sin)
    pad = jnp.zeros((SEQ, LANES - QK_NOPE_DIM - QK_ROPE_DIM), F32)

    def even(nope_val, rope_part):
        return jnp.concatenate([jnp.full((SEQ, QK_NOPE_DIM), nope_val, F32), rope_part, pad], axis=1)

    return jnp.stack([even(1.0, cos), even(0.0, s0), even(0.0, s1)])


def _split_w_in_kernel(wt_ref, lat_ref, cx_ref, cb_ref, cc_ref, ga_ref, gc_ref):
    kr_end = KV_LORA_RANK + QK_ROPE_DIM
    krq = wt_ref[KV_LORA_RANK:KV_LORA_RANK + LANES, :].T
    lane = lax.broadcasted_iota(jnp.int32, krq.shape, 1)
    kr = jnp.where((lane >= QK_NOPE_DIM) & (lane < QK_NOPE_DIM + QK_ROPE_DIM),
                   pltpu.roll(krq, QK_NOPE_DIM, 1), 0.0)
    lat_ref[...] = jnp.concatenate([wt_ref[:KV_LORA_RANK, :].T, kr, wt_ref[kr_end:O_CX, :].T],
                                   axis=1).astype(BF16)
    for ref, (a, b) in ((cx_ref, (O_CX, O_CB)), (cb_ref, (O_CB, O_CC)), (cc_ref, (O_CC, O_GA)),
                        (ga_ref, (O_GA, O_GC)), (gc_ref, (O_GC, O_END))):
        ref[...] = wt_ref[a:b, :].T.astype(BF16)


def _split_w_in(wt):
    widths = (C_LAT, CONV_DIM, CONV_DIM, CONV_DIM, D_MODEL, D_MODEL)
    return pl.pallas_call(
        _split_w_in_kernel,
        name="split_w_in",
        grid=(D_MODEL // LANES,),
        in_specs=[pl.BlockSpec((O_END, LANES), lambda i: (0, i))],
        out_specs=[pl.BlockSpec((LANES, n), lambda i: (i, 0)) for n in widths],
        out_shape=[jax.ShapeDtypeStruct((D_MODEL, n), BF16) for n in widths],
        compiler_params=_params(1),
    )(wt)


def _pack_w_ukv(w):
    w = w.astype(BF16).reshape(KV_LORA_RANK, N_HEADS, QK_NOPE_DIM + V_HEAD_DIM)
    return jnp.concatenate([w[..., :QK_NOPE_DIM].reshape(KV_LORA_RANK, N_HEADS * QK_NOPE_DIM),
                            w[..., QK_NOPE_DIM:].reshape(KV_LORA_RANK, ATTN_DIM)], axis=1)


def _pack_w_uq(w):
    qscale = (QK_NOPE_DIM + QK_ROPE_DIM) ** -0.5 * LOG2E
    w = (w * qscale).astype(BF16).reshape(Q_LORA_RANK, N_HEADS // 2, 2, QK_NOPE_DIM + QK_ROPE_DIM)
    pad = jnp.zeros((Q_LORA_RANK, N_HEADS // 2, LANES - QK_NOPE_DIM - QK_ROPE_DIM), BF16)
    ev, od = w[:, :, 0], w[:, :, 1]
    ev = jnp.concatenate([ev, pad], axis=-1)
    od = jnp.concatenate([od[..., QK_NOPE_DIM:], pad, od[..., :QK_NOPE_DIM]], axis=-1)
    return jnp.stack([ev, od], axis=2).reshape(Q_LORA_RANK, HEAD_COLS)


def kernel(x, c, ctx, c_ctx, w_ada, b_ada, norm1_g, w_in, q_norm_g, kv_norm_g, w_uq, w_ukv,
           conv_w, conv_b, w_attn_out, w_conv_out, w_o, norm2_g, w_up, ffn_conv_w, ffn_conv_b,
           w_down, final_g):
    assert x.shape == (BATCH, SEQ, D_MODEL) and ctx.shape == (BATCH, CTX_LEN, D_MODEL)
    assert w_ada.shape[0] == 1, "single-layer block"

    cvec = jnp.concatenate([c, c_ctx[None, :], jnp.zeros((16 - BATCH - 1, D_MODEL), F32)], axis=0)
    mod = _ada(cvec, w_ada.reshape(D_MODEL, 6 * D_MODEL), b_ada.reshape(1, 6 * D_MODEL))
    mod_lat = mod[:BATCH].reshape(BATCH, 6, D_MODEL)
    mod_ctx = mod[BATCH:BATCH + 1].reshape(1, 6, D_MODEL)

    tab = _rope_tables()
    w_lat, w_cx, w_cb, w_cc, w_ga, w_gc = _split_w_in(w_in.reshape(D_MODEL, O_END).T)
    w_kv_p = _pack_w_ukv(w_ukv.reshape(KV_LORA_RANK, -1))
    w_q_p = _pack_w_uq(w_uq.reshape(Q_LORA_RANK, -1))
    g1 = norm1_g.reshape(1, D_MODEL)
    kvg = kv_norm_g.reshape(1, KV_LORA_RANK)
    qg = q_norm_g.reshape(1, Q_LORA_RANK)

    x2 = x.reshape(BATCH * SEQ, D_MODEL)
    c2 = ctx.reshape(BATCH * CTX_LEN, D_MODEL)

    kc, vc = _ctxproj(c2, mod_ctx, g1, w_lat, kvg, w_kv_p)
    q, kl, vl, gg = _inproj(x2, mod_lat, tab, g1, w_lat, w_cx, w_cb, w_cc, w_ga, w_gc, kvg, qg,
                            w_kv_p, w_q_p,
                            conv_w.reshape(3, CONV_DIM), conv_b.reshape(1, CONV_DIM),
                            w_conv_out.reshape(CONV_DIM, D_MODEL).astype(BF16))
    o, (wao, wo, wup, wdn) = _attn(q, kl, vl, kc, vc, (
        w_attn_out.reshape(ATTN_DIM, D_MODEL), w_o.reshape(D_MODEL, D_MODEL),
        w_up.reshape(D_MODEL, 2 * D_FF), w_down.reshape(D_FF, D_MODEL)))
    out = _post(o, gg, x2, mod_lat, wao, wo, norm2_g.reshape(1, D_MODEL), wup,
                ffn_conv_w.reshape(3, 2 * D_FF), ffn_conv_b.reshape(1, 2 * D_FF), wdn,
                final_g.reshape(1, D_MODEL))
    return out.reshape(BATCH, SEQ, D_MODEL)
```

```python
import functools

import jax
import jax.numpy as jnp
from jax import lax
from jax.experimental import pallas as pl
from jax.experimental.pallas import tpu as pltpu

D_MODEL = 1024
BATCH = 8
SEQ = 2048
GRID_W = 64
CTX_LEN = 256
N_HEADS = 8
QK_NOPE_DIM = 64
QK_ROPE_DIM = 32
V_HEAD_DIM = 64
Q_LORA_RANK = 384
KV_LORA_RANK = 256
ROPE_THETA = 10000.0
CONV_DIM = 512
D_FF = 2816
EPS = 1e-6
ATTN_DIM = N_HEADS * V_HEAD_DIM

LANES = 128
HALF = LANES // 2
HEAD_COLS = N_HEADS * LANES
ROPE_HALF = QK_ROPE_DIM // 4
HALO = 8
HALO_BF16 = 16
CAST_ROWS = 16

C_KV = 0
C_KR = C_KV + KV_LORA_RANK
C_Q = C_KR + LANES
C_LAT = C_Q + Q_LORA_RANK
O_CX = KV_LORA_RANK + QK_ROPE_DIM + Q_LORA_RANK
O_CB = O_CX + CONV_DIM
O_CC = O_CB + CONV_DIM
O_GA = O_CC + CONV_DIM
O_GC = O_GA + D_MODEL
O_END = O_GC + D_MODEL

LOG2E = 1.4426950408889634

VMEM_LIMIT = 56 * 1024 * 1024

TM_IN = 1024
TM = 512
TQ = 512
FF_CHUNK = 256

BF16 = jnp.bfloat16
F32 = jnp.float32


def _dot(a, b):
    return jnp.dot(a, b, preferred_element_type=F32)


def _dot_nt(a, b):
    return lax.dot_general(a, b, (((1,), (1,)), ((), ())), preferred_element_type=F32)


def _rms(x, g):
    return x * lax.rsqrt(jnp.mean(x * x, axis=-1, keepdims=True) + EPS) * g


def _sigmoid(x):
    return 1.0 / (1.0 + jnp.exp(-x))


def _rope_block(xh, c, s0, s1):
    return (xh * c + pltpu.roll(xh, LANES - ROPE_HALF, 1) * s0
            + pltpu.roll(xh, ROPE_HALF, 1) * s1)


def _conv3(u_all, m, first, last, w_ref, b_ref, sl):
    u = u_all[:m]
    ext = jnp.concatenate([jnp.where(first, 0.0, u_all[m:m + HALO]), u,
                           jnp.where(last, 0.0, u_all[m + HALO:])], axis=0)
    u_dn = pltpu.roll(ext, 1, 0)[HALO:HALO + m]
    u_up = pltpu.roll(ext, m + 2 * HALO - 1, 0)[HALO:HALO + m]
    return b_ref[:, sl] + u_dn * w_ref[0:1, sl] + u * w_ref[1:2, sl] + u_up * w_ref[2:3, sl]


def _const_spec(shape):
    nd = len(shape)
    return pl.BlockSpec(shape, lambda *_: (0,) * nd, pipeline_mode=pl.Buffered(1))


def _params(n_grid):
    return pltpu.CompilerParams(dimension_semantics=("arbitrary",) * n_grid,
                                vmem_limit_bytes=VMEM_LIMIT)


def _ada_kernel(c_ref, w_ref, b_ref, o_ref):
    c = c_ref[...]
    s = (c * _sigmoid(c)).astype(BF16)
    o_ref[...] = _dot(s, w_ref[...].astype(BF16)) + b_ref[...]


def _ada(cvec, w_ada, b_ada):
    rows = cvec.shape[0]
    n = w_ada.shape[1]
    tn = 1024
    return pl.pallas_call(
        _ada_kernel,
        name="ada",
        grid=(n // tn,),
        in_specs=[pl.BlockSpec((rows, D_MODEL), lambda j: (0, 0)),
                  pl.BlockSpec((D_MODEL, tn), lambda j: (0, j)),
                  pl.BlockSpec((1, tn), lambda j: (0, j))],
        out_specs=pl.BlockSpec((rows, tn), lambda j: (0, j)),
        out_shape=jax.ShapeDtypeStruct((rows, n), F32),
        compiler_params=_params(1),
    )(cvec, w_ada, b_ada)


def _kv_from(pa, kr_even, kvg_ref, wkv_ref, k_ref, v_ref):
    ckv = _rms(pa[:, C_KV:C_KV + KV_LORA_RANK], kvg_ref[...]).astype(BF16)
    kvv = _dot(ckv, wkv_ref[...])
    kr_odd = pltpu.roll(kr_even, HALF, 1)
    lo = lax.broadcasted_iota(jnp.int32, kr_even.shape, 1) < HALF
    for p in range(N_HEADS // 2):
        kn_pair = kvv[:, p * LANES:(p + 1) * LANES]
        k_ref[:, (2 * p) * LANES:(2 * p + 1) * LANES] = jnp.where(lo, kn_pair, kr_even).astype(BF16)
        k_ref[:, (2 * p + 1) * LANES:(2 * p + 2) * LANES] = jnp.where(lo, kr_odd, kn_pair).astype(BF16)
    v_ref[...] = kvv[:, ATTN_DIM:].astype(BF16)


def _inproj_kernel(x_ref, xp_ref, xn_ref, mod_ref, tab_ref, g1_ref, wlat_ref, wcx_ref, wcb_ref,
                   wcc_ref, wga_ref, wgc_ref, kvg_ref, qg_ref, wkv_ref, wq_ref, cw_ref, cb_ref,
                   wco_ref, q_ref, k_ref, v_ref, gg_ref):
    s = pl.program_id(0)
    first = s == 0
    last = s == pl.num_programs(0) - 1
    sh = mod_ref[0, 0:1, :]
    sc = mod_ref[0, 1:2, :]
    g1 = g1_ref[...]
    xcat = jnp.concatenate([x_ref[...], xp_ref[...], xn_ref[...]], axis=0)
    hcat = (_rms(xcat, g1) * (1.0 + sc) + sh).astype(BF16)
    tm = x_ref.shape[0]
    hb = hcat[:tm]

    pa = _dot(hb, wlat_ref[...])
    x_in = _dot(hcat, wcx_ref[...])
    b_gate = _dot(hb, wcb_ref[...])
    c_gate = _dot(hcat, wcc_ref[...])

    tabs = [[tab_ref[0], tab_ref[1], tab_ref[2]]]
    tabs.append([pltpu.roll(t, HALF, 1) for t in tabs[0]])
    kr = _rope_block(pa[:, C_KR:C_KR + LANES], *tabs[0])
    _kv_from(pa, kr, kvg_ref, wkv_ref, k_ref, v_ref)
    cq = _rms(pa[:, C_Q:C_Q + Q_LORA_RANK], qg_ref[...]).astype(BF16)
    qf = _dot(cq, wq_ref[...])
    for h in range(N_HEADS):
        sl = slice(h * LANES, (h + 1) * LANES)
        q_ref[:, sl] = _rope_block(qf[:, sl], *tabs[h % 2]).astype(BF16)

    gg_ref[:, :D_MODEL] = _sigmoid(_dot(hb, wga_ref[...])).astype(BF16)

    conv = _conv3(c_gate * x_in, tm, first, last, cw_ref, cb_ref, slice(0, CONV_DIM))
    y_conv = _dot((b_gate * conv).astype(BF16), wco_ref[...])
    gg_ref[:, D_MODEL:] = (_sigmoid(_dot(hb, wgc_ref[...])) * y_conv).astype(BF16)


def _halo_specs(tm, cols, rows=HALO):
    tps = SEQ // tm
    r = tm // rows
    nblk = BATCH * SEQ // rows
    prev = pl.BlockSpec((rows, cols), lambda s, b: (jnp.maximum((b * tps + s) * r - 1, 0), 0))
    nxt = pl.BlockSpec((rows, cols), lambda s, b: (jnp.minimum((b * tps + s + 1) * r, nblk - 1), 0))
    return prev, nxt


def _inproj(x2, mod_lat, tab, g1, w_lat, w_cx, w_cb, w_cc, w_ga, w_gc, kvg, qg, w_kv_p, w_q_p,
            cw, cb, wco):
    n = x2.shape[0]
    tm = TM_IN
    tps = SEQ // tm
    row = lambda s, b: (b * tps + s, 0)
    xprev, xnext = _halo_specs(tm, D_MODEL)
    return pl.pallas_call(
        _inproj_kernel,
        name="inproj",
        grid=(tps, BATCH),
        in_specs=[
            pl.BlockSpec((tm, D_MODEL), row),
            xprev,
            xnext,
            pl.BlockSpec((1, 6, D_MODEL), lambda s, b: (b, 0, 0)),
            pl.BlockSpec((3, tm, LANES), lambda s, b: (0, s, 0)),
            _const_spec((1, D_MODEL)),
            _const_spec((D_MODEL, C_LAT)),
            _const_spec((D_MODEL, CONV_DIM)),
            _const_spec((D_MODEL, CONV_DIM)),
            _const_spec((D_MODEL, CONV_DIM)),
            _const_spec((D_MODEL, D_MODEL)),
            _const_spec((D_MODEL, D_MODEL)),
            _const_spec((1, KV_LORA_RANK)),
            _const_spec((1, Q_LORA_RANK)),
            _const_spec((KV_LORA_RANK, 2 * ATTN_DIM)),
            _const_spec((Q_LORA_RANK, HEAD_COLS)),
            _const_spec((3, CONV_DIM)),
            _const_spec((1, CONV_DIM)),
            _const_spec((CONV_DIM, D_MODEL)),
        ],
        out_specs=[pl.BlockSpec((tm, HEAD_COLS), row), pl.BlockSpec((tm, HEAD_COLS), row),
                   pl.BlockSpec((tm, ATTN_DIM), row), pl.BlockSpec((tm, 2 * D_MODEL), row)],
        out_shape=[jax.ShapeDtypeStruct((n, HEAD_COLS), BF16), jax.ShapeDtypeStruct((n, HEAD_COLS), BF16),
                   jax.ShapeDtypeStruct((n, ATTN_DIM), BF16), jax.ShapeDtypeStruct((n, 2 * D_MODEL), BF16)],
        compiler_params=_params(2),
    )(x2, x2, x2, mod_lat, tab, g1, w_lat, w_cx, w_cb, w_cc, w_ga, w_gc, kvg, qg, w_kv_p, w_q_p,
      cw, cb, wco)


def _ctxproj_kernel(x_ref, mod_ref, g1_ref, win_ref, kvg_ref, wkv_ref, k_ref, v_ref):
    sh = mod_ref[0, 0:1, :]
    sc = mod_ref[0, 1:2, :]
    hb = (_rms(x_ref[...], g1_ref[...]) * (1.0 + sc) + sh).astype(BF16)
    pa = _dot(hb, win_ref[...])
    _kv_from(pa, pa[:, C_KR:C_KR + LANES], kvg_ref, wkv_ref, k_ref, v_ref)


def _ctxproj(c2, mod_ctx, g1, w_lat, kvg, w_kv_p):
    n = c2.shape[0]
    tm = CTX_LEN
    row = lambda i: (i, 0)
    return pl.pallas_call(
        _ctxproj_kernel,
        name="ctxproj",
        grid=(n // tm,),
        in_specs=[
            pl.BlockSpec((tm, D_MODEL), row),
            _const_spec((1, 6, D_MODEL)),
            _const_spec((1, D_MODEL)),
            _const_spec((D_MODEL, C_Q)),
            _const_spec((1, KV_LORA_RANK)),
            _const_spec((KV_LORA_RANK, 2 * ATTN_DIM)),
        ],
        out_specs=[pl.BlockSpec((tm, HEAD_COLS), row), pl.BlockSpec((tm, ATTN_DIM), row)],
        out_shape=[jax.ShapeDtypeStruct((n, HEAD_COLS), BF16), jax.ShapeDtypeStruct((n, ATTN_DIM), BF16)],
        compiler_params=_params(1),
    )(c2, mod_ctx, g1, w_lat, kvg, w_kv_p)


def _attn_unit(qh, kc, kl, vc, vl):
    s_c = _dot_nt(qh, kc)
    s_l = _dot_nt(qh, kl)
    m = jnp.maximum(jnp.max(s_c, axis=-1, keepdims=True), jnp.max(s_l, axis=-1, keepdims=True))
    p_c = jnp.exp2(s_c - m)
    p_l = jnp.exp2(s_l - m)
    den = jnp.sum(p_c, axis=-1, keepdims=True) + jnp.sum(p_l, axis=-1, keepdims=True)
    acc = _dot(p_c.astype(BF16), vc) + _dot(p_l.astype(BF16), vl)
    return acc * (1.0 / den)


def _attn_kernel(q_ref, kl_ref, vl_ref, kc_ref, vc_ref, *rest):
    n_cast = len(rest) // 2
    for src, dst in zip(rest[:n_cast], rest[n_cast + 1:]):
        dst[...] = src[...].astype(BF16)
    o_ref = rest[n_cast]
    lo = lax.broadcasted_iota(jnp.int32, (TQ, LANES), 1) < HALF
    accs = []
    for h in range(N_HEADS):
        sl = slice(h * LANES, (h + 1) * LANES)
        vsl = slice((h // 2) * LANES, (h // 2 + 1) * LANES)
        nsplit = 2 if h in (0, N_HEADS - 1) else 1
        rows = TQ // nsplit
        parts = [_attn_unit(q_ref[r * rows:(r + 1) * rows, sl], kc_ref[:, sl], kl_ref[:, sl],
                            vc_ref[:, vsl], vl_ref[:, vsl]) for r in range(nsplit)]
        accs.append(parts[0] if nsplit == 1 else jnp.concatenate(parts, axis=0))
        if h % 2 == 1:
            o_ref[:, vsl] = jnp.where(lo, accs[h - 1], accs[h]).astype(BF16)


def _attn(q, kl, vl, kc, vc, cast_weights):
    nq = SEQ // TQ
    steps = BATCH * nq
    cast_specs = []
    for w in cast_weights:
        rows, cols = w.shape
        blk = next(r for r in range(CAST_ROWS, rows + 1, CAST_ROWS)
                   if rows % r == 0 and r * steps >= rows)
        nblk = rows // blk
        cast_specs.append(pl.BlockSpec((blk, cols), functools.partial(
            lambda b, j, nblk: (jnp.minimum(b * nq + j, nblk - 1), 0), nblk=nblk)))
    outs = pl.pallas_call(
        _attn_kernel,
        name="attn",
        grid=(BATCH, nq),
        in_specs=[
            pl.BlockSpec((TQ, HEAD_COLS), lambda b, j: (b * nq + j, 0)),
            pl.BlockSpec((SEQ, HEAD_COLS), lambda b, j: (b, 0)),
            pl.BlockSpec((SEQ, ATTN_DIM), lambda b, j: (b, 0)),
            pl.BlockSpec((CTX_LEN, HEAD_COLS), lambda b, j: (b, 0)),
            pl.BlockSpec((CTX_LEN, ATTN_DIM), lambda b, j: (b, 0)),
        ] + cast_specs,
        out_specs=[pl.BlockSpec((TQ, ATTN_DIM), lambda b, j: (b * nq + j, 0))] + cast_specs,
        out_shape=[jax.ShapeDtypeStruct((BATCH * SEQ, ATTN_DIM), BF16)]
                  + [jax.ShapeDtypeStruct(w.shape, BF16) for w in cast_weights],
        compiler_params=_params(2),
    )(q, kl, vl, kc, vc, *cast_weights)
    return outs[0], outs[1:]


def _halo_rows_bf16(prev_ref, next_ref):
    p = prev_ref[...].astype(F32)[HALO_BF16 - HALO:]
    n = next_ref[...].astype(F32)[:HALO]
    return jnp.concatenate([p, n], axis=0).astype(BF16)


def _post_kernel(o_ref, op_ref, on_ref, gg_ref, gp_ref, gn_ref, x_ref, xp_ref, xn_ref, mod_ref,
                 wao_ref, wo_ref, g2_ref, wup_ref, fcw_ref, fcb_ref, wdn_ref, gf_ref,
                 out_ref, act_ref):
    s = pl.program_id(0)
    first = s == 0
    last = s == pl.num_programs(0) - 1
    hm = TM // 2
    os_ = (o_ref[:hm], jnp.concatenate([o_ref[hm:], _halo_rows_bf16(op_ref, on_ref)], axis=0))
    ggs = (gg_ref[:hm], jnp.concatenate([gg_ref[hm:], _halo_rows_bf16(gp_ref, gn_ref)], axis=0))
    xs = (x_ref[:hm], jnp.concatenate([x_ref[hm:], xp_ref[...], xn_ref[...]], axis=0))
    y_attn = [_dot(o, wao_ref[...]) for o in os_]
    ms = [(gg[:, :D_MODEL].astype(F32) * y + gg[:, D_MODEL:].astype(F32)).astype(BF16)
          for gg, y in zip(ggs, y_attn)]
    x1s = [x + mod_ref[0, 2:3, :] * _dot(m, wo_ref[...]) for x, m in zip(xs, ms)]
    h2 = jnp.concatenate(
        [(_rms(x1, g2_ref[...]) * (1.0 + mod_ref[0, 4:5, :]) + mod_ref[0, 3:4, :]).astype(BF16)
         for x1 in x1s], axis=0)

    for c in range(D_FF // FF_CHUNK):
        gsl = slice(c * FF_CHUNK, (c + 1) * FF_CHUNK)
        vsl = slice(D_FF + c * FF_CHUNK, D_FF + (c + 1) * FF_CHUNK)
        gate = _conv3(_dot(h2, wup_ref[:, gsl]), TM, first, last, fcw_ref, fcb_ref, gsl)
        val = _conv3(_dot(h2, wup_ref[:, vsl]), TM, first, last, fcw_ref, fcb_ref, vsl)
        act_ref[:, gsl] = (gate * _sigmoid(gate) * val).astype(BF16)

    ys = [_dot(act_ref[r * hm:(r + 1) * hm, :], wdn_ref[...]) for r in range(2)]
    for r in range(2):
        x2 = x1s[r][:hm] + mod_ref[0, 5:6, :] * ys[r]
        out_ref[r * hm:(r + 1) * hm, :] = _rms(x2, gf_ref[...])


def _post(o, gg, x2, mod_lat, wao, wo, g2, wup, fcw, fcb, wdn, gf):
    n = x2.shape[0]
    tps = SEQ // TM
    row = lambda s, b: (b * tps + s, 0)
    oprev, onext = _halo_specs(TM, ATTN_DIM, HALO_BF16)
    gprev, gnext = _halo_specs(TM, 2 * D_MODEL, HALO_BF16)
    xprev, xnext = _halo_specs(TM, D_MODEL)
    return pl.pallas_call(
        _post_kernel,
        name="post",
        grid=(tps, BATCH),
        in_specs=[
            pl.BlockSpec((TM, ATTN_DIM), row), oprev, onext,
            pl.BlockSpec((TM, 2 * D_MODEL), row), gprev, gnext,
            pl.BlockSpec((TM, D_MODEL), row), xprev, xnext,
            pl.BlockSpec((1, 6, D_MODEL), lambda s, b: (b, 0, 0)),
            _const_spec((ATTN_DIM, D_MODEL)),
            _const_spec((D_MODEL, D_MODEL)),
            _const_spec((1, D_MODEL)),
            _const_spec((D_MODEL, 2 * D_FF)),
            _const_spec((3, 2 * D_FF)),
            _const_spec((1, 2 * D_FF)),
            _const_spec((D_FF, D_MODEL)),
            _const_spec((1, D_MODEL)),
        ],
        out_specs=pl.BlockSpec((TM, D_MODEL), row),
        out_shape=jax.ShapeDtypeStruct((n, D_MODEL), F32),
        scratch_shapes=[pltpu.VMEM((TM, D_FF), BF16)],
        compiler_params=_params(2),
    )(o, o, o, gg, gg, gg, x2, x2, x2, mod_lat, wao, wo, g2, wup, fcw, fcb, wdn, gf)


def _rope_tables():
    rows = SEQ // GRID_W
    row = jnp.repeat(jnp.arange(rows), GRID_W).astype(F32)
    col = jnp.tile(jnp.arange(GRID_W), rows).astype(F32)
    axis_dim = QK_ROPE_DIM // 2
    inv = ROPE_THETA ** (-jnp.arange(0, axis_dim, 2, dtype=F32) / axis_dim)
    ang = jnp.concatenate([row[:, None] * inv, col[:, None] * inv], axis=-1)
    j = jnp.arange(QK_ROPE_DIM)
    idx = (j // axis_dim) * ROPE_HALF + j % ROPE_HALF
    first_half = ((j % axis_dim) // ROPE_HALF) == 0
    cos = jnp.cos(ang)[:, idx]
    sin = jnp.sin(ang)[:, idx]
    s0 = jnp.where(first_half, -sin, 0.0)
    s1 = jnp.where(first_half, 0.0, sin)
    pad = jnp.zeros((SEQ, LANES - QK_NOPE_DIM - QK_ROPE_DIM), F32)

    def even(nope_val, rope_part):
        return jnp.concatenate([jnp.full((SEQ, QK_NOPE_DIM), nope_val, F32), rope_part, pad], axis=1)

    return jnp.stack([even(1.0, cos), even(0.0, s0), even(0.0, s1)])


def _split_w_in_kernel(wt_ref, lat_ref, cx_ref, cb_ref, cc_ref, ga_ref, gc_ref):
    kr_end = KV_LORA_RANK + QK_ROPE_DIM
    krq = wt_ref[KV_LORA_RANK:KV_LORA_RANK + LANES, :].T
    lane = lax.broadcasted_iota(jnp.int32, krq.shape, 1)
    kr = jnp.where((lane >= QK_NOPE_DIM) & (lane < QK_NOPE_DIM + QK_ROPE_DIM),
                   pltpu.roll(krq, QK_NOPE_DIM, 1), 0.0)
    lat_ref[...] = jnp.concatenate([wt_ref[:KV_LORA_RANK, :].T, kr, wt_ref[kr_end:O_CX, :].T],
                                   axis=1).astype(BF16)
    for ref, (a, b) in ((cx_ref, (O_CX, O_CB)), (cb_ref, (O_CB, O_CC)), (cc_ref, (O_CC, O_GA)),
                        (ga_ref, (O_GA, O_GC)), (gc_ref, (O_GC, O_END))):
        ref[...] = wt_ref[a:b, :].T.astype(BF16)


def _split_w_in(wt):
    widths = (C_LAT, CONV_DIM, CONV_DIM, CONV_DIM, D_MODEL, D_MODEL)
    return pl.pallas_call(
        _split_w_in_kernel,
        name="split_w_in",
        grid=(D_MODEL // LANES,),
        in_specs=[pl.BlockSpec((O_END, LANES), lambda i: (0, i))],
        out_specs=[pl.BlockSpec((LANES, n), lambda i: (i, 0)) for n in widths],
        out_shape=[jax.ShapeDtypeStruct((D_MODEL, n), BF16) for n in widths],
        compiler_params=_params(1),
    )(wt)


def _pack_w_ukv(w):
    w = w.astype(BF16).reshape(KV_LORA_RANK, N_HEADS, QK_NOPE_DIM + V_HEAD_DIM)
    return jnp.concatenate([w[..., :QK_NOPE_DIM].reshape(KV_LORA_RANK, N_HEADS * QK_NOPE_DIM),
                            w[..., QK_NOPE_DIM:].reshape(KV_LORA_RANK, ATTN_DIM)], axis=1)


def _pack_w_uq(w):
    qscale = (QK_NOPE_DIM + QK_ROPE_DIM) ** -0.5 * LOG2E
    w = (w * qscale).astype(BF16).reshape(Q_LORA_RANK, N_HEADS // 2, 2, QK_NOPE_DIM + QK_ROPE_DIM)
    pad = jnp.zeros((Q_LORA_RANK, N_HEADS // 2, LANES - QK_NOPE_DIM - QK_ROPE_DIM), BF16)
    ev, od = w[:, :, 0], w[:, :, 1]
    ev = jnp.concatenate([ev, pad], axis=-1)
    od = jnp.concatenate([od[..., QK_NOPE_DIM:], pad, od[..., :QK_NOPE_DIM]], axis=-1)
    return jnp.stack([ev, od], axis=2).reshape(Q_LORA_RANK, HEAD_COLS)


def kernel(x, c, ctx, c_ctx, w_ada, b_ada, norm1_g, w_in, q_norm_g, kv_norm_g, w_uq, w_ukv,
           conv_w, conv_b, w_attn_out, w_conv_out, w_o, norm2_g, w_up, ffn_conv_w, ffn_conv_b,
           w_down, final_g):
    assert x.shape == (BATCH, SEQ, D_MODEL) and ctx.shape == (BATCH, CTX_LEN, D_MODEL)
    assert w_ada.shape[0] == 1, "single-layer block"

    cvec = jnp.concatenate([c, c_ctx[None, :], jnp.zeros((16 - BATCH - 1, D_MODEL), F32)], axis=0)
    mod = _ada(cvec, w_ada.reshape(D_MODEL, 6 * D_MODEL), b_ada.reshape(1, 6 * D_MODEL))
    mod_lat = mod[:BATCH].reshape(BATCH, 6, D_MODEL)
    mod_ctx = mod[BATCH:BATCH + 1].reshape(1, 6, D_MODEL)

    tab = _rope_tables()
    w_lat, w_cx, w_cb, w_cc, w_ga, w_gc = _split_w_in(w_in.reshape(D_MODEL, O_END).T)
    w_kv_p = _pack_w_ukv(w_ukv.reshape(KV_LORA_RANK, -1))
    w_q_p = _pack_w_uq(w_uq.reshape(Q_LORA_RANK, -1))
    g1 = norm1_g.reshape(1, D_MODEL)
    kvg = kv_norm_g.reshape(1, KV_LORA_RANK)
    qg = q_norm_g.reshape(1, Q_LORA_RANK)

    x2 = x.reshape(BATCH * SEQ, D_MODEL)
    c2 = ctx.reshape(BATCH * CTX_LEN, D_MODEL)

    kc, vc = _ctxproj(c2, mod_ctx, g1, w_lat, kvg, w_kv_p)
    q, kl, vl, gg = _inproj(x2, mod_lat, tab, g1, w_lat, w_cx, w_cb, w_cc, w_ga, w_gc, kvg, qg,
                            w_kv_p, w_q_p,
                            conv_w.reshape(3, CONV_DIM), conv_b.reshape(1, CONV_DIM),
                            w_conv_out.reshape(CONV_DIM, D_MODEL).astype(BF16))
    o, (wao, wo, wup, wdn) = _attn(q, kl, vl, kc, vc, (
        w_attn_out.reshape(ATTN_DIM, D_MODEL), w_o.reshape(D_MODEL, D_MODEL),
        w_up.reshape(D_MODEL, 2 * D_FF), w_down.reshape(D_FF, D_MODEL)))
    out = _post(o, gg, x2, mod_lat, wao, wo, norm2_g.reshape(1, D_MODEL), wup,
                ffn_conv_w.reshape(3, 2 * D_FF), ffn_conv_b.reshape(1, 2 * D_FF), wdn,
                final_g.reshape(1, D_MODEL))
    return out.reshape(BATCH, SEQ, D_MODEL)
```

```python
import functools

import jax
import jax.numpy as jnp
from jax import lax
from jax.experimental import pallas as pl
from jax.experimental.pallas import tpu as pltpu

D_MODEL = 1024
BATCH = 8
SEQ = 2048
GRID_W = 64
CTX_LEN = 256
N_HEADS = 8
QK_NOPE_DIM = 64
QK_ROPE_DIM = 32
V_HEAD_DIM = 64
Q_LORA_RANK = 384
KV_LORA_RANK = 256
ROPE_THETA = 10000.0
CONV_DIM = 512
D_FF = 2816
EPS = 1e-6
ATTN_DIM = N_HEADS * V_HEAD_DIM

LANES = 128
HALF = LANES // 2
HEAD_COLS = N_HEADS * LANES
ROPE_HALF = QK_ROPE_DIM // 4
HALO = 8
HALO_BF16 = 16
CAST_ROWS = 16
DEN_ROWS = 16

C_KV = 0
C_KR = C_KV + KV_LORA_RANK
C_Q = C_KR + LANES
C_LAT = C_Q + Q_LORA_RANK
O_CX = KV_LORA_RANK + QK_ROPE_DIM + Q_LORA_RANK
O_CB = O_CX + CONV_DIM
O_CC = O_CB + CONV_DIM
O_GA = O_CC + CONV_DIM
O_GC = O_GA + D_MODEL
O_END = O_GC + D_MODEL

LOG2E = 1.4426950408889634

VMEM_LIMIT = 56 * 1024 * 1024

TM_IN = 1024
TM = 512
TQ = 512
Q_SUB = 2
FF_CHUNK = 256

BF16 = jnp.bfloat16
F32 = jnp.float32


def _dot(a, b):
    return jnp.dot(a, b, preferred_element_type=F32)


def _dot_nt(a, b):
    return lax.dot_general(a, b, (((1,), (1,)), ((), ())), preferred_element_type=F32)


def _rms(x, g):
    return x * lax.rsqrt(jnp.mean(x * x, axis=-1, keepdims=True) + EPS) * g


def _sigmoid(x):
    return 1.0 / (1.0 + jnp.exp(-x))


def _rope_block(xh, c, s0, s1):
    return (xh * c + pltpu.roll(xh, LANES - ROPE_HALF, 1) * s0
            + pltpu.roll(xh, ROPE_HALF, 1) * s1)


def _conv3(u_all, m, first, last, w_ref, b_ref, sl):
    u = u_all[:m]
    ext = jnp.concatenate([jnp.where(first, 0.0, u_all[m:m + HALO]), u,
                           jnp.where(last, 0.0, u_all[m + HALO:])], axis=0)
    u_dn = pltpu.roll(ext, 1, 0)[HALO:HALO + m]
    u_up = pltpu.roll(ext, m + 2 * HALO - 1, 0)[HALO:HALO + m]
    return b_ref[:, sl] + u_dn * w_ref[0:1, sl] + u * w_ref[1:2, sl] + u_up * w_ref[2:3, sl]


def _const_spec(shape):
    nd = len(shape)
    return pl.BlockSpec(shape, lambda *_: (0,) * nd, pipeline_mode=pl.Buffered(1))


def _params(n_grid):
    return pltpu.CompilerParams(dimension_semantics=("arbitrary",) * n_grid,
                                vmem_limit_bytes=VMEM_LIMIT)


def _ada_kernel(c_ref, w_ref, b_ref, o_ref):
    c = c_ref[...]
    s = (c * _sigmoid(c)).astype(BF16)
    o_ref[...] = _dot(s, w_ref[...].astype(BF16)) + b_ref[...]


def _ada(cvec, w_ada, b_ada):
    rows = cvec.shape[0]
    n = w_ada.shape[1]
    tn = 1024
    return pl.pallas_call(
        _ada_kernel,
        name="ada",
        grid=(n // tn,),
        in_specs=[pl.BlockSpec((rows, D_MODEL), lambda j: (0, 0)),
                  pl.BlockSpec((D_MODEL, tn), lambda j: (0, j)),
                  pl.BlockSpec((1, tn), lambda j: (0, j))],
        out_specs=pl.BlockSpec((rows, tn), lambda j: (0, j)),
        out_shape=jax.ShapeDtypeStruct((rows, n), F32),
        compiler_params=_params(1),
    )(cvec, w_ada, b_ada)


def _kv_from(pa, kr_even, kvg_ref, wkn_ref, wvt_ref, k_ref, vt_ref):
    ckv = _rms(pa[:, C_KV:C_KV + KV_LORA_RANK], kvg_ref[...]).astype(BF16)
    kn = _dot(ckv, wkn_ref[...])
    kr_odd = pltpu.roll(kr_even, HALF, 1)
    lo = lax.broadcasted_iota(jnp.int32, kr_even.shape, 1) < HALF
    for p in range(N_HEADS // 2):
        kn_pair = kn[:, p * LANES:(p + 1) * LANES]
        k_ref[:, (2 * p) * LANES:(2 * p + 1) * LANES] = jnp.where(lo, kn_pair, kr_even).astype(BF16)
        k_ref[:, (2 * p + 1) * LANES:(2 * p + 2) * LANES] = jnp.where(lo, kr_odd, kn_pair).astype(BF16)
    vt_ref[...] = _dot_nt(wvt_ref[...], ckv).astype(BF16)


def _inproj_kernel(x_ref, xp_ref, xn_ref, mod_ref, tab_ref, g1_ref, wlat_ref, wcx_ref, wcb_ref,
                   wcc_ref, wga_ref, wgc_ref, kvg_ref, qg_ref, wkn_ref, wvt_ref, wq_ref, cw_ref,
                   cb_ref, wco_ref, q_ref, k_ref, vt_ref, gg_ref):
    s = pl.program_id(0)
    first = s == 0
    last = s == pl.num_programs(0) - 1
    sh = mod_ref[0, 0:1, :]
    sc = mod_ref[0, 1:2, :]
    g1 = g1_ref[...]
    xcat = jnp.concatenate([x_ref[...], xp_ref[...], xn_ref[...]], axis=0)
    hcat = (_rms(xcat, g1) * (1.0 + sc) + sh).astype(BF16)
    tm = x_ref.shape[0]
    hb = hcat[:tm]

    pa = _dot(hb, wlat_ref[...])
    x_in = _dot(hcat, wcx_ref[...])
    b_gate = _dot(hb, wcb_ref[...])
    c_gate = _dot(hcat, wcc_ref[...])

    tabs = [[tab_ref[0], tab_ref[1], tab_ref[2]]]
    tabs.append([pltpu.roll(t, HALF, 1) for t in tabs[0]])
    kr = _rope_block(pa[:, C_KR:C_KR + LANES], *tabs[0])
    _kv_from(pa, kr, kvg_ref, wkn_ref, wvt_ref, k_ref, vt_ref)
    cq = _rms(pa[:, C_Q:C_Q + Q_LORA_RANK], qg_ref[...]).astype(BF16)
    qf = _dot(cq, wq_ref[...])
    for h in range(N_HEADS):
        sl = slice(h * LANES, (h + 1) * LANES)
        q_ref[sl, :] = _rope_block(qf[:, sl], *tabs[h % 2]).T.astype(BF16)

    gg_ref[:, :D_MODEL] = _sigmoid(_dot(hb, wga_ref[...])).astype(BF16)

    conv = _conv3(c_gate * x_in, tm, first, last, cw_ref, cb_ref, slice(0, CONV_DIM))
    y_conv = _dot((b_gate * conv).astype(BF16), wco_ref[...])
    gg_ref[:, D_MODEL:] = (_sigmoid(_dot(hb, wgc_ref[...])) * y_conv).astype(BF16)


def _halo_specs(tm, cols, rows=HALO):
    tps = SEQ // tm
    r = tm // rows
    nblk = BATCH * SEQ // rows
    prev = pl.BlockSpec((rows, cols), lambda s, b: (jnp.maximum((b * tps + s) * r - 1, 0), 0))
    nxt = pl.BlockSpec((rows, cols), lambda s, b: (jnp.minimum((b * tps + s + 1) * r, nblk - 1), 0))
    return prev, nxt


def _inproj(x2, mod_lat, tab, g1, w_lat, w_cx, w_cb, w_cc, w_ga, w_gc, kvg, qg, w_kn, w_vt, w_q_p,
            cw, cb, wco):
    n = x2.shape[0]
    tm = TM_IN
    tps = SEQ // tm
    row = lambda s, b: (b * tps + s, 0)
    xprev, xnext = _halo_specs(tm, D_MODEL)
    return pl.pallas_call(
        _inproj_kernel,
        name="inproj",
        grid=(tps, BATCH),
        in_specs=[
            pl.BlockSpec((tm, D_MODEL), row),
            xprev,
            xnext,
            pl.BlockSpec((1, 6, D_MODEL), lambda s, b: (b, 0, 0)),
            pl.BlockSpec((3, tm, LANES), lambda s, b: (0, s, 0)),
            _const_spec((1, D_MODEL)),
            _const_spec((D_MODEL, C_LAT)),
            _const_spec((D_MODEL, CONV_DIM)),
            _const_spec((D_MODEL, CONV_DIM)),
            _const_spec((D_MODEL, CONV_DIM)),
            _const_spec((D_MODEL, D_MODEL)),
            _const_spec((D_MODEL, D_MODEL)),
            _const_spec((1, KV_LORA_RANK)),
            _const_spec((1, Q_LORA_RANK)),
            _const_spec((KV_LORA_RANK, N_HEADS * QK_NOPE_DIM)),
            _const_spec((ATTN_DIM, KV_LORA_RANK)),
            _const_spec((Q_LORA_RANK, HEAD_COLS)),
            _const_spec((3, CONV_DIM)),
            _const_spec((1, CONV_DIM)),
            _const_spec((CONV_DIM, D_MODEL)),
        ],
        out_specs=[pl.BlockSpec((HEAD_COLS, tm), lambda s, b: (0, b * tps + s)),
                   pl.BlockSpec((tm, HEAD_COLS), row),
                   pl.BlockSpec((ATTN_DIM, tm), lambda s, b: (b, s)),
                   pl.BlockSpec((tm, 2 * D_MODEL), row)],
        out_shape=[jax.ShapeDtypeStruct((HEAD_COLS, n), BF16), jax.ShapeDtypeStruct((n, HEAD_COLS), BF16),
                   jax.ShapeDtypeStruct((BATCH * ATTN_DIM, SEQ), BF16),
                   jax.ShapeDtypeStruct((n, 2 * D_MODEL), BF16)],
        compiler_params=_params(2),
    )(x2, x2, x2, mod_lat, tab, g1, w_lat, w_cx, w_cb, w_cc, w_ga, w_gc, kvg, qg, w_kn, w_vt, w_q_p,
      cw, cb, wco)


def _ctxproj_kernel(x_ref, mod_ref, g1_ref, win_ref, kvg_ref, wkn_ref, wvt_ref, k_ref, vt_ref):
    sh = mod_ref[0, 0:1, :]
    sc = mod_ref[0, 1:2, :]
    hb = (_rms(x_ref[...], g1_ref[...]) * (1.0 + sc) + sh).astype(BF16)
    pa = _dot(hb, win_ref[...])
    _kv_from(pa, pa[:, C_KR:C_KR + LANES], kvg_ref, wkn_ref, wvt_ref, k_ref, vt_ref)


def _ctxproj(c2, mod_ctx, g1, w_lat, kvg, w_kn, w_vt):
    n = c2.shape[0]
    tm = CTX_LEN
    row = lambda i: (i, 0)
    return pl.pallas_call(
        _ctxproj_kernel,
        name="ctxproj",
        grid=(n // tm,),
        in_specs=[
            pl.BlockSpec((tm, D_MODEL), row),
            _const_spec((1, 6, D_MODEL)),
            _const_spec((1, D_MODEL)),
            _const_spec((D_MODEL, C_Q)),
            _const_spec((1, KV_LORA_RANK)),
            _const_spec((KV_LORA_RANK, N_HEADS * QK_NOPE_DIM)),
            _const_spec((ATTN_DIM, KV_LORA_RANK)),
        ],
        out_specs=[pl.BlockSpec((tm, HEAD_COLS), row), pl.BlockSpec((ATTN_DIM, tm), row)],
        out_shape=[jax.ShapeDtypeStruct((n, HEAD_COLS), BF16),
                   jax.ShapeDtypeStruct((BATCH * ATTN_DIM, tm), BF16)],
        compiler_params=_params(1),
    )(c2, mod_ctx, g1, w_lat, kvg, w_kn, w_vt)


def _attn_kernel(qt_ref, kl_ref, vlt_ref, kc_ref, vct_ref, *rest):
    n_cast = len(rest) // 2
    for src, dst in zip(rest[:n_cast], rest[n_cast + 1:]):
        dst[...] = src[...].astype(BF16)
    o_ref = rest[n_cast]

    def scores(unit):
        r, h = unit
        sl = slice(h * LANES, (h + 1) * LANES)
        qt = qt_ref[sl, r * TQ:(r + 1) * TQ]
        s_c, s_l = _dot(kc_ref[:, sl], qt), _dot(kl_ref[:, sl], qt)
        m = jnp.maximum(jnp.max(s_c, axis=0, keepdims=True), jnp.max(s_l, axis=0, keepdims=True))
        return s_c, s_l, m

    units = [(r, h) for r in range(Q_SUB) for h in range(N_HEADS)]
    nxt = scores(units[0])
    halves = []
    for i, (r, h) in enumerate(units):
        s_c, s_l, m = nxt
        p_c = jnp.exp2(s_c - m).astype(BF16)
        if i + 1 < len(units):
            nxt = scores(units[i + 1])
        p_l = jnp.exp2(s_l - m).astype(BF16)
        vsl = slice((h // 2) * LANES, (h // 2 + 1) * LANES)
        vct = jnp.concatenate([vct_ref[vsl, :], jnp.ones((DEN_ROWS, CTX_LEN), BF16)], axis=0)
        vlt = jnp.concatenate([vlt_ref[vsl, :], jnp.ones((DEN_ROWS, SEQ), BF16)], axis=0)
        acc = _dot(vct, p_c) + _dot(vlt, p_l)
        rows = slice((h % 2) * V_HEAD_DIM, (h % 2 + 1) * V_HEAD_DIM)
        halves.append(acc[rows] * (1.0 / acc[LANES:LANES + 1]))
        if h % 2 == 1:
            o_ref[r * TQ:(r + 1) * TQ, vsl] = jnp.concatenate(halves[-2:], axis=0).T.astype(BF16)


def _attn(q, kl, vlt, kc, vct, cast_weights):
    tq = TQ * Q_SUB
    nq = SEQ // tq
    steps = BATCH * nq
    cast_specs = []
    for w in cast_weights:
        rows, cols = w.shape
        blk = next(r for r in range(CAST_ROWS, rows + 1, CAST_ROWS)
                   if rows % r == 0 and r * steps >= rows)
        nblk = rows // blk
        cast_specs.append(pl.BlockSpec((blk, cols), functools.partial(
            lambda b, j, nblk: (jnp.minimum(b * nq + j, nblk - 1), 0), nblk=nblk)))
    outs = pl.pallas_call(
        _attn_kernel,
        name="attn",
        grid=(BATCH, nq),
        in_specs=[
            pl.BlockSpec((HEAD_COLS, tq), lambda b, j: (0, b * nq + j)),
            pl.BlockSpec((SEQ, HEAD_COLS), lambda b, j: (b, 0)),
            pl.BlockSpec((ATTN_DIM, SEQ), lambda b, j: (b, 0)),
            pl.BlockSpec((CTX_LEN, HEAD_COLS), lambda b, j: (b, 0)),
            pl.BlockSpec((ATTN_DIM, CTX_LEN), lambda b, j: (b, 0)),
        ] + cast_specs,
        out_specs=[pl.BlockSpec((tq, ATTN_DIM), lambda b, j: (b * nq + j, 0))] + cast_specs,
        out_shape=[jax.ShapeDtypeStruct((BATCH * SEQ, ATTN_DIM), BF16)]
                  + [jax.ShapeDtypeStruct(w.shape, BF16) for w in cast_weights],
        compiler_params=_params(2),
    )(q, kl, vlt, kc, vct, *cast_weights)
    return outs[0], outs[1:]


def _halo_rows_bf16(prev_ref, next_ref):
    p = prev_ref[...].astype(F32)[HALO_BF16 - HALO:]
    n = next_ref[...].astype(F32)[:HALO]
    return jnp.concatenate([p, n], axis=0).astype(BF16)


def _post_kernel(o_ref, op_ref, on_ref, gg_ref, gp_ref, gn_ref, x_ref, xp_ref, xn_ref, mod_ref,
                 wao_ref, wo_ref, g2_ref, wup_ref, fcw_ref, fcb_ref, wdn_ref, gf_ref,
                 out_ref, act_ref):
    s = pl.program_id(0)
    first = s == 0
    last = s == pl.num_programs(0) - 1
    hm = TM // 2
    os_ = (o_ref[:hm], jnp.concatenate([o_ref[hm:], _halo_rows_bf16(op_ref, on_ref)], axis=0))
    ggs = (gg_ref[:hm], jnp.concatenate([gg_ref[hm:], _halo_rows_bf16(gp_ref, gn_ref)], axis=0))
    xs = (x_ref[:hm], jnp.concatenate([x_ref[hm:], xp_ref[...], xn_ref[...]], axis=0))
    y_attn = [_dot(o, wao_ref[...]) for o in os_]
    ms = [(gg[:, :D_MODEL].astype(F32) * y + gg[:, D_MODEL:].astype(F32)).astype(BF16)
          for gg, y in zip(ggs, y_attn)]
    x1s = [x + mod_ref[0, 2:3, :] * _dot(m, wo_ref[...]) for x, m in zip(xs, ms)]
    h2 = jnp.concatenate(
        [(_rms(x1, g2_ref[...]) * (1.0 + mod_ref[0, 4:5, :]) + mod_ref[0, 3:4, :]).astype(BF16)
         for x1 in x1s], axis=0)

    for c in range(D_FF // FF_CHUNK):
        gsl = slice(c * FF_CHUNK, (c + 1) * FF_CHUNK)
        vsl = slice(D_FF + c * FF_CHUNK, D_FF + (c + 1) * FF_CHUNK)
        gate = _conv3(_dot(h2, wup_ref[:, gsl]), TM, first, last, fcw_ref, fcb_ref, gsl)
        val = _conv3(_dot(h2, wup_ref[:, vsl]), TM, first, last, fcw_ref, fcb_ref, vsl)
        act_ref[:, gsl] = (gate * _sigmoid(gate) * val).astype(BF16)

    ys = [_dot(act_ref[r * hm:(r + 1) * hm, :], wdn_ref[...]) for r in range(2)]
    for r in range(2):
        x2 = x1s[r][:hm] + mod_ref[0, 5:6, :] * ys[r]
        out_ref[r * hm:(r + 1) * hm, :] = _rms(x2, gf_ref[...])


def _post(o, gg, x2, mod_lat, wao, wo, g2, wup, fcw, fcb, wdn, gf):
    n = x2.shape[0]
    tps = SEQ // TM
    row = lambda s, b: (b * tps + s, 0)
    oprev, onext = _halo_specs(TM, ATTN_DIM, HALO_BF16)
    gprev, gnext = _halo_specs(TM, 2 * D_MODEL, HALO_BF16)
    xprev, xnext = _halo_specs(TM, D_MODEL)
    return pl.pallas_call(
        _post_kernel,
        name="post",
        grid=(tps, BATCH),
        in_specs=[
            pl.BlockSpec((TM, ATTN_DIM), row), oprev, onext,
            pl.BlockSpec((TM, 2 * D_MODEL), row), gprev, gnext,
            pl.BlockSpec((TM, D_MODEL), row), xprev, xnext,
            pl.BlockSpec((1, 6, D_MODEL), lambda s, b: (b, 0, 0)),
            _const_spec((ATTN_DIM, D_MODEL)),
            _const_spec((D_MODEL, D_MODEL)),
            _const_spec((1, D_MODEL)),
            _const_spec((D_MODEL, 2 * D_FF)),
            _const_spec((3, 2 * D_FF)),
            _const_spec((1, 2 * D_FF)),
            _const_spec((D_FF, D_MODEL)),
            _const_spec((1, D_MODEL)),
        ],
        out_specs=pl.BlockSpec((TM, D_MODEL), row),
        out_shape=jax.ShapeDtypeStruct((n, D_MODEL), F32),
        scratch_shapes=[pltpu.VMEM((TM, D_FF), BF16)],
        compiler_params=_params(2),
    )(o, o, o, gg, gg, gg, x2, x2, x2, mod_lat, wao, wo, g2, wup, fcw, fcb, wdn, gf)


def _rope_tables():
    rows = SEQ // GRID_W
    row = jnp.repeat(jnp.arange(rows), GRID_W).astype(F32)
    col = jnp.tile(jnp.arange(GRID_W), rows).astype(F32)
    axis_dim = QK_ROPE_DIM // 2
    inv = ROPE_THETA ** (-jnp.arange(0, axis_dim, 2, dtype=F32) / axis_dim)
    ang = jnp.concatenate([row[:, None] * inv, col[:, None] * inv], axis=-1)
    j = jnp.arange(QK_ROPE_DIM)
    idx = (j // axis_dim) * ROPE_HALF + j % ROPE_HALF
    first_half = ((j % axis_dim) // ROPE_HALF) == 0
    cos = jnp.cos(ang)[:, idx]
    sin = jnp.sin(ang)[:, idx]
    s0 = jnp.where(first_half, -sin, 0.0)
    s1 = jnp.where(first_half, 0.0, sin)
    pad = jnp.zeros((SEQ, LANES - QK_NOPE_DIM - QK_ROPE_DIM), F32)

    def even(nope_val, rope_part):
        return jnp.concatenate([jnp.full((SEQ, QK_NOPE_DIM), nope_val, F32), rope_part, pad], axis=1)

    return jnp.stack([even(1.0, cos), even(0.0, s0), even(0.0, s1)])


def _split_w_in_kernel(wt_ref, lat_ref, cx_ref, cb_ref, cc_ref, ga_ref, gc_ref):
    kr_end = KV_LORA_RANK + QK_ROPE_DIM
    krq = wt_ref[KV_LORA_RANK:KV_LORA_RANK + LANES, :].T
    lane = lax.broadcasted_iota(jnp.int32, krq.shape, 1)
    kr = jnp.where((lane >= QK_NOPE_DIM) & (lane < QK_NOPE_DIM + QK_ROPE_DIM),
                   pltpu.roll(krq, QK_NOPE_DIM, 1), 0.0)
    lat_ref[...] = jnp.concatenate([wt_ref[:KV_LORA_RANK, :].T, kr, wt_ref[kr_end:O_CX, :].T],
                                   axis=1).astype(BF16)
    for ref, (a, b) in ((cx_ref, (O_CX, O_CB)), (cb_ref, (O_CB, O_CC)), (cc_ref, (O_CC, O_GA)),
                        (ga_ref, (O_GA, O_GC)), (gc_ref, (O_GC, O_END))):
        ref[...] = wt_ref[a:b, :].T.astype(BF16)


def _split_w_in(wt):
    widths = (C_LAT, CONV_DIM, CONV_DIM, CONV_DIM, D_MODEL, D_MODEL)
    return pl.pallas_call(
        _split_w_in_kernel,
        name="split_w_in",
        grid=(D_MODEL // LANES,),
        in_specs=[pl.BlockSpec((O_END, LANES), lambda i: (0, i))],
        out_specs=[pl.BlockSpec((LANES, n), lambda i: (i, 0)) for n in widths],
        out_shape=[jax.ShapeDtypeStruct((D_MODEL, n), BF16) for n in widths],
        compiler_params=_params(1),
    )(wt)


def _pack_w_ukv(w):
    w = w.astype(BF16).reshape(KV_LORA_RANK, N_HEADS, QK_NOPE_DIM + V_HEAD_DIM)
    return (w[..., :QK_NOPE_DIM].reshape(KV_LORA_RANK, N_HEADS * QK_NOPE_DIM),
            w[..., QK_NOPE_DIM:].reshape(KV_LORA_RANK, ATTN_DIM).T)


def _pack_w_uq(w):
    qscale = (QK_NOPE_DIM + QK_ROPE_DIM) ** -0.5 * LOG2E
    w = (w * qscale).astype(BF16).reshape(Q_LORA_RANK, N_HEADS // 2, 2, QK_NOPE_DIM + QK_ROPE_DIM)
    pad = jnp.zeros((Q_LORA_RANK, N_HEADS // 2, LANES - QK_NOPE_DIM - QK_ROPE_DIM), BF16)
    ev, od = w[:, :, 0], w[:, :, 1]
    ev = jnp.concatenate([ev, pad], axis=-1)
    od = jnp.concatenate([od[..., QK_NOPE_DIM:], pad, od[..., :QK_NOPE_DIM]], axis=-1)
    return jnp.stack([ev, od], axis=2).reshape(Q_LORA_RANK, HEAD_COLS)


def kernel(x, c, ctx, c_ctx, w_ada, b_ada, norm1_g, w_in, q_norm_g, kv_norm_g, w_uq, w_ukv,
           conv_w, conv_b, w_attn_out, w_conv_out, w_o, norm2_g, w_up, ffn_conv_w, ffn_conv_b,
           w_down, final_g):
    assert x.shape == (BATCH, SEQ, D_MODEL) and ctx.shape == (BATCH, CTX_LEN, D_MODEL)
    assert w_ada.shape[0] == 1, "single-layer block"

    cvec = jnp.concatenate([c, c_ctx[None, :], jnp.zeros((16 - BATCH - 1, D_MODEL), F32)], axis=0)
    mod = _ada(cvec, w_ada.reshape(D_MODEL, 6 * D_MODEL), b_ada.reshape(1, 6 * D_MODEL))
    mod_lat = mod[:BATCH].reshape(BATCH, 6, D_MODEL)
    mod_ctx = mod[BATCH:BATCH + 1].reshape(1, 6, D_MODEL)

    tab = _rope_tables()
    w_lat, w_cx, w_cb, w_cc, w_ga, w_gc = _split_w_in(w_in.reshape(D_MODEL, O_END).T)
    w_kn, w_vt = _pack_w_ukv(w_ukv.reshape(KV_LORA_RANK, -1))
    w_q_p = _pack_w_uq(w_uq.reshape(Q_LORA_RANK, -1))
    g1 = norm1_g.reshape(1, D_MODEL)
    kvg = kv_norm_g.reshape(1, KV_LORA_RANK)
    qg = q_norm_g.reshape(1, Q_LORA_RANK)

    x2 = x.reshape(BATCH * SEQ, D_MODEL)
    c2 = ctx.reshape(BATCH * CTX_LEN, D_MODEL)

    kc, vct = _ctxproj(c2, mod_ctx, g1, w_lat, kvg, w_kn, w_vt)
    q, kl, vlt, gg = _inproj(x2, mod_lat, tab, g1, w_lat, w_cx, w_cb, w_cc, w_ga, w_gc, kvg, qg,
                             w_kn, w_vt, w_q_p,
                            conv_w.reshape(3, CONV_DIM), conv_b.reshape(1, CONV_DIM),
                            w_conv_out.reshape(CONV_DIM, D_MODEL).astype(BF16))
    o, (wao, wo, wup, wdn) = _attn(q, kl, vlt, kc, vct, (
        w_attn_out.reshape(ATTN_DIM, D_MODEL), w_o.reshape(D_MODEL, D_MODEL),
        w_up.reshape(D_MODEL, 2 * D_FF), w_down.reshape(D_FF, D_MODEL)))
    out = _post(o, gg, x2, mod_lat, wao, wo, norm2_g.reshape(1, D_MODEL), wup,
                ffn_conv_w.reshape(3, 2 * D_FF), ffn_conv_b.reshape(1, 2 * D_FF), wdn,
                final_g.reshape(1, D_MODEL))
    return out.reshape(BATCH, SEQ, D_MODEL)
```

```python
import functools

import jax
import jax.numpy as jnp
from jax import lax
from jax.experimental import pallas as pl
from jax.experimental.pallas import tpu as pltpu

D_MODEL = 1024
BATCH = 8
SEQ = 2048
GRID_W = 64
CTX_LEN = 256
N_HEADS = 8
QK_NOPE_DIM = 64
QK_ROPE_DIM = 32
V_HEAD_DIM = 64
Q_LORA_RANK = 384
KV_LORA_RANK = 256
ROPE_THETA = 10000.0
CONV_DIM = 512
D_FF = 2816
EPS = 1e-6
ATTN_DIM = N_HEADS * V_HEAD_DIM

LANES = 128
HALF = LANES // 2
HEAD_COLS = N_HEADS * LANES
ROPE_HALF = QK_ROPE_DIM // 4
HALO = 8
HALO_BF16 = 16
CAST_ROWS = 16
DEN_ROWS = 16

C_KV = 0
C_KR = C_KV + KV_LORA_RANK
C_Q = C_KR + LANES
C_LAT = C_Q + Q_LORA_RANK
O_CX = KV_LORA_RANK + QK_ROPE_DIM + Q_LORA_RANK
O_CB = O_CX + CONV_DIM
O_CC = O_CB + CONV_DIM
O_GA = O_CC + CONV_DIM
O_GC = O_GA + D_MODEL
O_END = O_GC + D_MODEL

LOG2E = 1.4426950408889634

VMEM_LIMIT = 56 * 1024 * 1024

TM_IN = 1024
TM = 512
TQ = 1024
Q_SUB = 1
SCORES_AHEAD = 1
KEY_CHUNK = 256
FF_CHUNK = 256

BF16 = jnp.bfloat16
F32 = jnp.float32


def _dot(a, b):
    return jnp.dot(a, b, preferred_element_type=F32)


def _dot_nt(a, b):
    return lax.dot_general(a, b, (((1,), (1,)), ((), ())), preferred_element_type=F32)


def _rms(x, g):
    return x * lax.rsqrt(jnp.mean(x * x, axis=-1, keepdims=True) + EPS) * g


def _sigmoid(x):
    return 1.0 / (1.0 + jnp.exp(-x))


def _rope_block(xh, c, s0, s1):
    return (xh * c + pltpu.roll(xh, LANES - ROPE_HALF, 1) * s0
            + pltpu.roll(xh, ROPE_HALF, 1) * s1)


def _conv3(u_all, m, first, last, w_ref, b_ref, sl):
    u = u_all[:m]
    ext = jnp.concatenate([jnp.where(first, 0.0, u_all[m:m + HALO]), u,
                           jnp.where(last, 0.0, u_all[m + HALO:])], axis=0)
    u_dn = pltpu.roll(ext, 1, 0)[HALO:HALO + m]
    u_up = pltpu.roll(ext, m + 2 * HALO - 1, 0)[HALO:HALO + m]
    return b_ref[:, sl] + u_dn * w_ref[0:1, sl] + u * w_ref[1:2, sl] + u_up * w_ref[2:3, sl]


def _const_spec(shape):
    nd = len(shape)
    return pl.BlockSpec(shape, lambda *_: (0,) * nd, pipeline_mode=pl.Buffered(1))


def _params(n_grid):
    return pltpu.CompilerParams(dimension_semantics=("arbitrary",) * n_grid,
                                vmem_limit_bytes=VMEM_LIMIT)


def _ada_kernel(c_ref, w_ref, b_ref, o_ref):
    c = c_ref[...]
    s = (c * _sigmoid(c)).astype(BF16)
    o_ref[...] = _dot(s, w_ref[...].astype(BF16)) + b_ref[...]


def _ada(cvec, w_ada, b_ada):
    rows = cvec.shape[0]
    n = w_ada.shape[1]
    tn = 1024
    return pl.pallas_call(
        _ada_kernel,
        name="ada",
        grid=(n // tn,),
        in_specs=[pl.BlockSpec((rows, D_MODEL), lambda j: (0, 0)),
                  pl.BlockSpec((D_MODEL, tn), lambda j: (0, j)),
                  pl.BlockSpec((1, tn), lambda j: (0, j))],
        out_specs=pl.BlockSpec((rows, tn), lambda j: (0, j)),
        out_shape=jax.ShapeDtypeStruct((rows, n), F32),
        compiler_params=_params(1),
    )(cvec, w_ada, b_ada)


def _kv_from(pa, kr_even, kvg_ref, wkn_ref, wvt_ref, k_ref, vt_ref):
    ckv = _rms(pa[:, C_KV:C_KV + KV_LORA_RANK], kvg_ref[...]).astype(BF16)
    kn = _dot(ckv, wkn_ref[...])
    kr_odd = pltpu.roll(kr_even, HALF, 1)
    lo = lax.broadcasted_iota(jnp.int32, kr_even.shape, 1) < HALF
    for p in range(N_HEADS // 2):
        kn_pair = kn[:, p * LANES:(p + 1) * LANES]
        k_ref[:, (2 * p) * LANES:(2 * p + 1) * LANES] = jnp.where(lo, kn_pair, kr_even).astype(BF16)
        k_ref[:, (2 * p + 1) * LANES:(2 * p + 2) * LANES] = jnp.where(lo, kr_odd, kn_pair).astype(BF16)
    vt_ref[...] = _dot_nt(wvt_ref[...], ckv).astype(BF16)


def _inproj_kernel(x_ref, xp_ref, xn_ref, mod_ref, tab_ref, g1_ref, wlat_ref, wcx_ref, wcb_ref,
                   wcc_ref, wga_ref, wgc_ref, kvg_ref, qg_ref, wkn_ref, wvt_ref, wq_ref, cw_ref,
                   cb_ref, wco_ref, q_ref, k_ref, vt_ref, gg_ref):
    s = pl.program_id(0)
    first = s == 0
    last = s == pl.num_programs(0) - 1
    sh = mod_ref[0, 0:1, :]
    sc = mod_ref[0, 1:2, :]
    g1 = g1_ref[...]
    xcat = jnp.concatenate([x_ref[...], xp_ref[...], xn_ref[...]], axis=0)
    hcat = (_rms(xcat, g1) * (1.0 + sc) + sh).astype(BF16)
    tm = x_ref.shape[0]
    hb = hcat[:tm]

    pa = _dot(hb, wlat_ref[...])
    x_in = _dot(hcat, wcx_ref[...])
    b_gate = _dot(hb, wcb_ref[...])
    c_gate = _dot(hcat, wcc_ref[...])

    tabs = [[tab_ref[0], tab_ref[1], tab_ref[2]]]
    tabs.append([pltpu.roll(t, HALF, 1) for t in tabs[0]])
    kr = _rope_block(pa[:, C_KR:C_KR + LANES], *tabs[0])
    _kv_from(pa, kr, kvg_ref, wkn_ref, wvt_ref, k_ref, vt_ref)
    cq = _rms(pa[:, C_Q:C_Q + Q_LORA_RANK], qg_ref[...]).astype(BF16)
    qf = _dot(cq, wq_ref[...])
    for h in range(N_HEADS):
        sl = slice(h * LANES, (h + 1) * LANES)
        q_ref[sl, :] = _rope_block(qf[:, sl], *tabs[h % 2]).T.astype(BF16)

    gg_ref[:, :D_MODEL] = _sigmoid(_dot(hb, wga_ref[...])).astype(BF16)

    conv = _conv3(c_gate * x_in, tm, first, last, cw_ref, cb_ref, slice(0, CONV_DIM))
    y_conv = _dot((b_gate * conv).astype(BF16), wco_ref[...])
    gg_ref[:, D_MODEL:] = (_sigmoid(_dot(hb, wgc_ref[...])) * y_conv).astype(BF16)


def _halo_specs(tm, cols, rows=HALO):
    tps = SEQ // tm
    r = tm // rows
    nblk = BATCH * SEQ // rows
    prev = pl.BlockSpec((rows, cols), lambda s, b: (jnp.maximum((b * tps + s) * r - 1, 0), 0))
    nxt = pl.BlockSpec((rows, cols), lambda s, b: (jnp.minimum((b * tps + s + 1) * r, nblk - 1), 0))
    return prev, nxt


def _inproj(x2, mod_lat, tab, g1, w_lat, w_cx, w_cb, w_cc, w_ga, w_gc, kvg, qg, w_kn, w_vt, w_q_p,
            cw, cb, wco):
    n = x2.shape[0]
    tm = TM_IN
    tps = SEQ // tm
    row = lambda s, b: (b * tps + s, 0)
    xprev, xnext = _halo_specs(tm, D_MODEL)
    return pl.pallas_call(
        _inproj_kernel,
        name="inproj",
        grid=(tps, BATCH),
        in_specs=[
            pl.BlockSpec((tm, D_MODEL), row),
            xprev,
            xnext,
            pl.BlockSpec((1, 6, D_MODEL), lambda s, b: (b, 0, 0)),
            pl.BlockSpec((3, tm, LANES), lambda s, b: (0, s, 0)),
            _const_spec((1, D_MODEL)),
            _const_spec((D_MODEL, C_LAT)),
            _const_spec((D_MODEL, CONV_DIM)),
            _const_spec((D_MODEL, CONV_DIM)),
            _const_spec((D_MODEL, CONV_DIM)),
            _const_spec((D_MODEL, D_MODEL)),
            _const_spec((D_MODEL, D_MODEL)),
            _const_spec((1, KV_LORA_RANK)),
            _const_spec((1, Q_LORA_RANK)),
            _const_spec((KV_LORA_RANK, N_HEADS * QK_NOPE_DIM)),
            _const_spec((ATTN_DIM, KV_LORA_RANK)),
            _const_spec((Q_LORA_RANK, HEAD_COLS)),
            _const_spec((3, CONV_DIM)),
            _const_spec((1, CONV_DIM)),
            _const_spec((CONV_DIM, D_MODEL)),
        ],
        out_specs=[pl.BlockSpec((HEAD_COLS, tm), lambda s, b: (0, b * tps + s)),
                   pl.BlockSpec((tm, HEAD_COLS), row),
                   pl.BlockSpec((ATTN_DIM, tm), lambda s, b: (b, s)),
                   pl.BlockSpec((tm, 2 * D_MODEL), row)],
        out_shape=[jax.ShapeDtypeStruct((HEAD_COLS, n), BF16), jax.ShapeDtypeStruct((n, HEAD_COLS), BF16),
                   jax.ShapeDtypeStruct((BATCH * ATTN_DIM, SEQ), BF16),
                   jax.ShapeDtypeStruct((n, 2 * D_MODEL), BF16)],
        compiler_params=_params(2),
    )(x2, x2, x2, mod_lat, tab, g1, w_lat, w_cx, w_cb, w_cc, w_ga, w_gc, kvg, qg, w_kn, w_vt, w_q_p,
      cw, cb, wco)


def _ctxproj_kernel(x_ref, mod_ref, g1_ref, win_ref, kvg_ref, wkn_ref, wvt_ref, k_ref, vt_ref):
    sh = mod_ref[0, 0:1, :]
    sc = mod_ref[0, 1:2, :]
    hb = (_rms(x_ref[...], g1_ref[...]) * (1.0 + sc) + sh).astype(BF16)
    pa = _dot(hb, win_ref[...])
    _kv_from(pa, pa[:, C_KR:C_KR + LANES], kvg_ref, wkn_ref, wvt_ref, k_ref, vt_ref)


def _ctxproj(c2, mod_ctx, g1, w_lat, kvg, w_kn, w_vt):
    n = c2.shape[0]
    tm = CTX_LEN
    row = lambda i: (i, 0)
    return pl.pallas_call(
        _ctxproj_kernel,
        name="ctxproj",
        grid=(n // tm,),
        in_specs=[
            pl.BlockSpec((tm, D_MODEL), row),
            _const_spec((1, 6, D_MODEL)),
            _const_spec((1, D_MODEL)),
            _const_spec((D_MODEL, C_Q)),
            _const_spec((1, KV_LORA_RANK)),
            _const_spec((KV_LORA_RANK, N_HEADS * QK_NOPE_DIM)),
            _const_spec((ATTN_DIM, KV_LORA_RANK)),
        ],
        out_specs=[pl.BlockSpec((tm, HEAD_COLS), row), pl.BlockSpec((ATTN_DIM, tm), row)],
        out_shape=[jax.ShapeDtypeStruct((n, HEAD_COLS), BF16),
                   jax.ShapeDtypeStruct((BATCH * ATTN_DIM, tm), BF16)],
        compiler_params=_params(1),
    )(c2, mod_ctx, g1, w_lat, kvg, w_kn, w_vt)


def _attn_kernel(qt_ref, kl_ref, vlt_ref, kc_ref, vct_ref, *rest):
    n_cast = len(rest) // 2
    for src, dst in zip(rest[:n_cast], rest[n_cast + 1:]):
        dst[...] = src[...].astype(BF16)
    o_ref = rest[n_cast]

    def scores(unit):
        r, h = unit
        sl = slice(h * LANES, (h + 1) * LANES)
        qt = qt_ref[sl, r * TQ:(r + 1) * TQ]
        chunks = [_dot(kc_ref[:, sl], qt)] + [
            _dot(kl_ref[c * KEY_CHUNK:(c + 1) * KEY_CHUNK, sl], qt) for c in range(SEQ // KEY_CHUNK)]
        pm = None
        for s in chunks:
            cm = jnp.max(s.reshape(s.shape[0] // HALO, HALO, TQ), axis=0)
            pm = cm if pm is None else jnp.maximum(pm, cm)
        return chunks, jnp.max(pm, axis=0, keepdims=True)

    units = [(r, h) for r in range(Q_SUB) for h in range(N_HEADS)]
    ahead = [scores(u) for u in units[:SCORES_AHEAD]]
    halves = []
    for i, (r, h) in enumerate(units):
        chunks, m = ahead.pop(0)
        if i + SCORES_AHEAD < len(units):
            ahead.append(scores(units[i + SCORES_AHEAD]))
        vsl = slice((h // 2) * LANES, (h // 2 + 1) * LANES)
        vct = jnp.concatenate([vct_ref[vsl, :], jnp.ones((DEN_ROWS, CTX_LEN), BF16)], axis=0)
        vlt = jnp.concatenate([vlt_ref[vsl, :], jnp.ones((DEN_ROWS, SEQ), BF16)], axis=0)
        acc = _dot(vct, jnp.exp2(chunks[0] - m).astype(BF16))
        for c, s in enumerate(chunks[1:]):
            acc += _dot(vlt[:, c * KEY_CHUNK:(c + 1) * KEY_CHUNK], jnp.exp2(s - m).astype(BF16))
        rows = slice((h % 2) * V_HEAD_DIM, (h % 2 + 1) * V_HEAD_DIM)
        halves.append(acc[rows] * (1.0 / acc[LANES:LANES + 1]))
        if h % 2 == 1:
            o_ref[r * TQ:(r + 1) * TQ, vsl] = jnp.concatenate(halves[-2:], axis=0).T.astype(BF16)


def _attn(q, kl, vlt, kc, vct, cast_weights):
    tq = TQ * Q_SUB
    nq = SEQ // tq
    steps = BATCH * nq
    cast_specs = []
    for w in cast_weights:
        rows, cols = w.shape
        blk = next(r for r in range(CAST_ROWS, rows + 1, CAST_ROWS)
                   if rows % r == 0 and r * steps >= rows)
        nblk = rows // blk
        cast_specs.append(pl.BlockSpec((blk, cols), functools.partial(
            lambda b, j, nblk: (jnp.minimum(b * nq + j, nblk - 1), 0), nblk=nblk)))
    outs = pl.pallas_call(
        _attn_kernel,
        name="attn",
        grid=(BATCH, nq),
        in_specs=[
            pl.BlockSpec((HEAD_COLS, tq), lambda b, j: (0, b * nq + j)),
            pl.BlockSpec((SEQ, HEAD_COLS), lambda b, j: (b, 0)),
            pl.BlockSpec((ATTN_DIM, SEQ), lambda b, j: (b, 0)),
            pl.BlockSpec((CTX_LEN, HEAD_COLS), lambda b, j: (b, 0)),
            pl.BlockSpec((ATTN_DIM, CTX_LEN), lambda b, j: (b, 0)),
        ] + cast_specs,
        out_specs=[pl.BlockSpec((tq, ATTN_DIM), lambda b, j: (b * nq + j, 0))] + cast_specs,
        out_shape=[jax.ShapeDtypeStruct((BATCH * SEQ, ATTN_DIM), BF16)]
                  + [jax.ShapeDtypeStruct(w.shape, BF16) for w in cast_weights],
        compiler_params=_params(2),
    )(q, kl, vlt, kc, vct, *cast_weights)
    return outs[0], outs[1:]


def _halo_rows_bf16(prev_ref, next_ref):
    p = prev_ref[...].astype(F32)[HALO_BF16 - HALO:]
    n = next_ref[...].astype(F32)[:HALO]
    return jnp.concatenate([p, n], axis=0).astype(BF16)


def _post_kernel(o_ref, op_ref, on_ref, gg_ref, gp_ref, gn_ref, x_ref, xp_ref, xn_ref, mod_ref,
                 wao_ref, wo_ref, g2_ref, wup_ref, fcw_ref, fcb_ref, wdn_ref, gf_ref,
                 out_ref, act_ref):
    s = pl.program_id(0)
    first = s == 0
    last = s == pl.num_programs(0) - 1
    hm = TM // 2
    os_ = (o_ref[:hm], jnp.concatenate([o_ref[hm:], _halo_rows_bf16(op_ref, on_ref)], axis=0))
    ggs = (gg_ref[:hm], jnp.concatenate([gg_ref[hm:], _halo_rows_bf16(gp_ref, gn_ref)], axis=0))
    xs = (x_ref[:hm], jnp.concatenate([x_ref[hm:], xp_ref[...], xn_ref[...]], axis=0))
    y_attn = [_dot(o, wao_ref[...]) for o in os_]
    ms = [(gg[:, :D_MODEL].astype(F32) * y + gg[:, D_MODEL:].astype(F32)).astype(BF16)
          for gg, y in zip(ggs, y_attn)]
    x1s = [x + mod_ref[0, 2:3, :] * _dot(m, wo_ref[...]) for x, m in zip(xs, ms)]
    h2 = jnp.concatenate(
        [(_rms(x1, g2_ref[...]) * (1.0 + mod_ref[0, 4:5, :]) + mod_ref[0, 3:4, :]).astype(BF16)
         for x1 in x1s], axis=0)

    for c in range(D_FF // FF_CHUNK):
        gsl = slice(c * FF_CHUNK, (c + 1) * FF_CHUNK)
        vsl = slice(D_FF + c * FF_CHUNK, D_FF + (c + 1) * FF_CHUNK)
        gate = _conv3(_dot(h2, wup_ref[:, gsl]), TM, first, last, fcw_ref, fcb_ref, gsl)
        val = _conv3(_dot(h2, wup_ref[:, vsl]), TM, first, last, fcw_ref, fcb_ref, vsl)
        act_ref[:, gsl] = (gate * _sigmoid(gate) * val).astype(BF16)

    ys = [_dot(act_ref[r * hm:(r + 1) * hm, :], wdn_ref[...]) for r in range(2)]
    for r in range(2):
        x2 = x1s[r][:hm] + mod_ref[0, 5:6, :] * ys[r]
        out_ref[r * hm:(r + 1) * hm, :] = _rms(x2, gf_ref[...])


def _post(o, gg, x2, mod_lat, wao, wo, g2, wup, fcw, fcb, wdn, gf):
    n = x2.shape[0]
    tps = SEQ // TM
    row = lambda s, b: (b * tps + s, 0)
    oprev, onext = _halo_specs(TM, ATTN_DIM, HALO_BF16)
    gprev, gnext = _halo_specs(TM, 2 * D_MODEL, HALO_BF16)
    xprev, xnext = _halo_specs(TM, D_MODEL)
    return pl.pallas_call(
        _post_kernel,
        name="post",
        grid=(tps, BATCH),
        in_specs=[
            pl.BlockSpec((TM, ATTN_DIM), row), oprev, onext,
            pl.BlockSpec((TM, 2 * D_MODEL), row), gprev, gnext,
            pl.BlockSpec((TM, D_MODEL), row), xprev, xnext,
            pl.BlockSpec((1, 6, D_MODEL), lambda s, b: (b, 0, 0)),
            _const_spec((ATTN_DIM, D_MODEL)),
            _const_spec((D_MODEL, D_MODEL)),
            _const_spec((1, D_MODEL)),
            _const_spec((D_MODEL, 2 * D_FF)),
            _const_spec((3, 2 * D_FF)),
            _const_spec((1, 2 * D_FF)),
            _const_spec((D_FF, D_MODEL)),
            _const_spec((1, D_MODEL)),
        ],
        out_specs=pl.BlockSpec((TM, D_MODEL), row),
        out_shape=jax.ShapeDtypeStruct((n, D_MODEL), F32),
        scratch_shapes=[pltpu.VMEM((TM, D_FF), BF16)],
        compiler_params=_params(2),
    )(o, o, o, gg, gg, gg, x2, x2, x2, mod_lat, wao, wo, g2, wup, fcw, fcb, wdn, gf)


def _rope_tables():
    rows = SEQ // GRID_W
    row = jnp.repeat(jnp.arange(rows), GRID_W).astype(F32)
    col = jnp.tile(jnp.arange(GRID_W), rows).astype(F32)
    axis_dim = QK_ROPE_DIM // 2
    inv = ROPE_THETA ** (-jnp.arange(0, axis_dim, 2, dtype=F32) / axis_dim)
    ang = jnp.concatenate([row[:, None] * inv, col[:, None] * inv], axis=-1)
    j = jnp.arange(QK_ROPE_DIM)
    idx = (j // axis_dim) * ROPE_HALF + j % ROPE_HALF
    first_half = ((j % axis_dim) // ROPE_HALF) == 0
    cos = jnp.cos(ang)[:, idx]
    sin = jnp.sin(ang)[:, idx]
    s0 = jnp.where(first_half, -sin, 0.0)
    s1 = jnp.where(first_half, 0.0, sin)
    pad = jnp.zeros((SEQ, LANES - QK_NOPE_DIM - QK_ROPE_DIM), F32)

    def even(nope_val, rope_part):
        return jnp.concatenate([jnp.full((SEQ, QK_NOPE_DIM), nope_val, F32), rope_part, pad], axis=1)

    return jnp.stack([even(1.0, cos), even(0.0, s0), even(0.0, s1)])


def _split_w_in_kernel(wt_ref, lat_ref, cx_ref, cb_ref, cc_ref, ga_ref, gc_ref):
    kr_end = KV_LORA_RANK + QK_ROPE_DIM
    krq = wt_ref[KV_LORA_RANK:KV_LORA_RANK + LANES, :].T
    lane = lax.broadcasted_iota(jnp.int32, krq.shape, 1)
    kr = jnp.where((lane >= QK_NOPE_DIM) & (lane < QK_NOPE_DIM + QK_ROPE_DIM),
                   pltpu.roll(krq, QK_NOPE_DIM, 1), 0.0)
    lat_ref[...] = jnp.concatenate([wt_ref[:KV_LORA_RANK, :].T, kr, wt_ref[kr_end:O_CX, :].T],
                                   axis=1).astype(BF16)
    for ref, (a, b) in ((cx_ref, (O_CX, O_CB)), (cb_ref, (O_CB, O_CC)), (cc_ref, (O_CC, O_GA)),
                        (ga_ref, (O_GA, O_GC)), (gc_ref, (O_GC, O_END))):
        ref[...] = wt_ref[a:b, :].T.astype(BF16)


def _split_w_in(wt):
    widths = (C_LAT, CONV_DIM, CONV_DIM, CONV_DIM, D_MODEL, D_MODEL)
    return pl.pallas_call(
        _split_w_in_kernel,
        name="split_w_in",
        grid=(D_MODEL // LANES,),
        in_specs=[pl.BlockSpec((O_END, LANES), lambda i: (0, i))],
        out_specs=[pl.BlockSpec((LANES, n), lambda i: (i, 0)) for n in widths],
        out_shape=[jax.ShapeDtypeStruct((D_MODEL, n), BF16) for n in widths],
        compiler_params=_params(1),
    )(wt)


def _pack_w_ukv(w):
    w = w.astype(BF16).reshape(KV_LORA_RANK, N_HEADS, QK_NOPE_DIM + V_HEAD_DIM)
    return (w[..., :QK_NOPE_DIM].reshape(KV_LORA_RANK, N_HEADS * QK_NOPE_DIM),
            w[..., QK_NOPE_DIM:].reshape(KV_LORA_RANK, ATTN_DIM).T)


def _pack_w_uq(w):
    qscale = (QK_NOPE_DIM + QK_ROPE_DIM) ** -0.5 * LOG2E
    w = (w * qscale).astype(BF16).reshape(Q_LORA_RANK, N_HEADS // 2, 2, QK_NOPE_DIM + QK_ROPE_DIM)
    pad = jnp.zeros((Q_LORA_RANK, N_HEADS // 2, LANES - QK_NOPE_DIM - QK_ROPE_DIM), BF16)
    ev, od = w[:, :, 0], w[:, :, 1]
    ev = jnp.concatenate([ev, pad], axis=-1)
    od = jnp.concatenate([od[..., QK_NOPE_DIM:], pad, od[..., :QK_NOPE_DIM]], axis=-1)
    return jnp.stack([ev, od], axis=2).reshape(Q_LORA_RANK, HEAD_COLS)


def kernel(x, c, ctx, c_ctx, w_ada, b_ada, norm1_g, w_in, q_norm_g, kv_norm_g, w_uq, w_ukv,
           conv_w, conv_b, w_attn_out, w_conv_out, w_o, norm2_g, w_up, ffn_conv_w, ffn_conv_b,
           w_down, final_g):
    assert x.shape == (BATCH, SEQ, D_MODEL) and ctx.shape == (BATCH, CTX_LEN, D_MODEL)
    assert w_ada.shape[0] == 1, "single-layer block"

    cvec = jnp.concatenate([c, c_ctx[None, :], jnp.zeros((16 - BATCH - 1, D_MODEL), F32)], axis=0)
    mod = _ada(cvec, w_ada.reshape(D_MODEL, 6 * D_MODEL), b_ada.reshape(1, 6 * D_MODEL))
    mod_lat = mod[:BATCH].reshape(BATCH, 6, D_MODEL)
    mod_ctx = mod[BATCH:BATCH + 1].reshape(1, 6, D_MODEL)

    tab = _rope_tables()
    w_lat, w_cx, w_cb, w_cc, w_ga, w_gc = _split_w_in(w_in.reshape(D_MODEL, O_END).T)
    w_kn, w_vt = _pack_w_ukv(w_ukv.reshape(KV_LORA_RANK, -1))
    w_q_p = _pack_w_uq(w_uq.reshape(Q_LORA_RANK, -1))
    g1 = norm1_g.reshape(1, D_MODEL)
    kvg = kv_norm_g.reshape(1, KV_LORA_RANK)
    qg = q_norm_g.reshape(1, Q_LORA_RANK)

    x2 = x.reshape(BATCH * SEQ, D_MODEL)
    c2 = ctx.reshape(BATCH * CTX_LEN, D_MODEL)

    kc, vct = _ctxproj(c2, mod_ctx, g1, w_lat, kvg, w_kn, w_vt)
    q, kl, vlt, gg = _inproj(x2, mod_lat, tab, g1, w_lat, w_cx, w_cb, w_cc, w_ga, w_gc, kvg, qg,
                             w_kn, w_vt, w_q_p,
                            conv_w.reshape(3, CONV_DIM), conv_b.reshape(1, CONV_DIM),
                            w_conv_out.reshape(CONV_DIM, D_MODEL).astype(BF16))
    o, (wao, wo, wup, wdn) = _attn(q, kl, vlt, kc, vct, (
        w_attn_out.reshape(ATTN_DIM, D_MODEL), w_o.reshape(D_MODEL, D_MODEL),
        w_up.reshape(D_MODEL, 2 * D_FF), w_down.reshape(D_FF, D_MODEL)))
    out = _post(o, gg, x2, mod_lat, wao, wo, norm2_g.reshape(1, D_MODEL), wup,
                ffn_conv_w.reshape(3, 2 * D_FF), ffn_conv_b.reshape(1, 2 * D_FF), wdn,
                final_g.reshape(1, D_MODEL))
    return out.reshape(BATCH, SEQ, D_MODEL)
```

```python
import functools

import jax
import jax.numpy as jnp
from jax import lax
from jax.experimental import pallas as pl
from jax.experimental.pallas import tpu as pltpu

D_MODEL = 1024
BATCH = 8
SEQ = 2048
GRID_W = 64
CTX_LEN = 256
N_HEADS = 8
QK_NOPE_DIM = 64
QK_ROPE_DIM = 32
V_HEAD_DIM = 64
Q_LORA_RANK = 384
KV_LORA_RANK = 256
ROPE_THETA = 10000.0
CONV_DIM = 512
D_FF = 2816
EPS = 1e-6
ATTN_DIM = N_HEADS * V_HEAD_DIM

LANES = 128
HALF = LANES // 2
HEAD_COLS = N_HEADS * LANES
ROPE_HALF = QK_ROPE_DIM // 4
HALO = 8
HALO_BF16 = 16
CAST_ROWS = 16
DEN_ROWS = 16

C_KV = 0
C_KR = C_KV + KV_LORA_RANK
C_Q = C_KR + LANES
C_LAT = C_Q + Q_LORA_RANK
O_CX = KV_LORA_RANK + QK_ROPE_DIM + Q_LORA_RANK
O_CB = O_CX + CONV_DIM
O_CC = O_CB + CONV_DIM
O_GA = O_CC + CONV_DIM
O_GC = O_GA + D_MODEL
O_END = O_GC + D_MODEL

LOG2E = 1.4426950408889634

VMEM_LIMIT = 56 * 1024 * 1024

TM_IN = 1024
TM = 512
TQ = 1024
Q_SUB = 1
SCORES_AHEAD = 1
KEY_CHUNK = 256
FF_CHUNK = 256

BF16 = jnp.bfloat16
F32 = jnp.float32


def _dot(a, b):
    return jnp.dot(a, b, preferred_element_type=F32)


def _dot_nt(a, b):
    return lax.dot_general(a, b, (((1,), (1,)), ((), ())), preferred_element_type=F32)


def _rms(x, g):
    return x * lax.rsqrt(jnp.mean(x * x, axis=-1, keepdims=True) + EPS) * g


def _sigmoid(x):
    return 1.0 / (1.0 + jnp.exp(-x))


def _rope_block(xh, c, s0, s1):
    return (xh * c + pltpu.roll(xh, LANES - ROPE_HALF, 1) * s0
            + pltpu.roll(xh, ROPE_HALF, 1) * s1)


def _conv3(u_all, m, first, last, w_ref, b_ref, sl):
    u = u_all[:m]
    ext = jnp.concatenate([jnp.where(first, 0.0, u_all[m:m + HALO]), u,
                           jnp.where(last, 0.0, u_all[m + HALO:])], axis=0)
    u_dn = pltpu.roll(ext, 1, 0)[HALO:HALO + m]
    u_up = pltpu.roll(ext, m + 2 * HALO - 1, 0)[HALO:HALO + m]
    return b_ref[:, sl] + u_dn * w_ref[0:1, sl] + u * w_ref[1:2, sl] + u_up * w_ref[2:3, sl]


def _const_spec(shape):
    nd = len(shape)
    return pl.BlockSpec(shape, lambda *_: (0,) * nd, pipeline_mode=pl.Buffered(1))


def _params(n_grid):
    return pltpu.CompilerParams(dimension_semantics=("arbitrary",) * n_grid,
                                vmem_limit_bytes=VMEM_LIMIT)


def _ada_kernel(c_ref, w_ref, b_ref, o_ref):
    c = c_ref[...]
    s = (c * _sigmoid(c)).astype(BF16)
    o_ref[...] = _dot(s, w_ref[...].astype(BF16)) + b_ref[...]


def _ada(cvec, w_ada, b_ada):
    rows = cvec.shape[0]
    n = w_ada.shape[1]
    tn = 1024
    return pl.pallas_call(
        _ada_kernel,
        name="ada",
        grid=(n // tn,),
        in_specs=[pl.BlockSpec((rows, D_MODEL), lambda j: (0, 0)),
                  pl.BlockSpec((D_MODEL, tn), lambda j: (0, j)),
                  pl.BlockSpec((1, tn), lambda j: (0, j))],
        out_specs=pl.BlockSpec((rows, tn), lambda j: (0, j)),
        out_shape=jax.ShapeDtypeStruct((rows, n), F32),
        compiler_params=_params(1),
    )(cvec, w_ada, b_ada)


def _kv_from(pa, kr_even, kvg_ref, wkn_ref, wvt_ref, k_ref, vt_ref):
    ckv = _rms(pa[:, C_KV:C_KV + KV_LORA_RANK], kvg_ref[...]).astype(BF16)
    kn = _dot(ckv, wkn_ref[...])
    kr_odd = pltpu.roll(kr_even, HALF, 1)
    lo = lax.broadcasted_iota(jnp.int32, kr_even.shape, 1) < HALF
    for p in range(N_HEADS // 2):
        kn_pair = kn[:, p * LANES:(p + 1) * LANES]
        k_ref[:, (2 * p) * LANES:(2 * p + 1) * LANES] = jnp.where(lo, kn_pair, kr_even).astype(BF16)
        k_ref[:, (2 * p + 1) * LANES:(2 * p + 2) * LANES] = jnp.where(lo, kr_odd, kn_pair).astype(BF16)
    vt = _dot_nt(wvt_ref[...], ckv).astype(BF16)
    per_seq = vt_ref.shape[1]
    for b in range(vt.shape[1] // per_seq):
        vt_ref[b * ATTN_DIM:(b + 1) * ATTN_DIM, :] = vt[:, b * per_seq:(b + 1) * per_seq]


def _inproj_kernel(x_ref, xp_ref, xn_ref, mod_ref, tab_ref, g1_ref, wlat_ref, wcx_ref, wcb_ref,
                   wcc_ref, wga_ref, wgc_ref, kvg_ref, qg_ref, wkn_ref, wvt_ref, wq_ref, cw_ref,
                   cb_ref, wco_ref, q_ref, k_ref, vt_ref, gg_ref):
    s = pl.program_id(0)
    first = s == 0
    last = s == pl.num_programs(0) - 1
    sh = mod_ref[0, 0:1, :]
    sc = mod_ref[0, 1:2, :]
    g1 = g1_ref[...]
    xcat = jnp.concatenate([x_ref[...], xp_ref[...], xn_ref[...]], axis=0)
    hcat = (_rms(xcat, g1) * (1.0 + sc) + sh).astype(BF16)
    tm = x_ref.shape[0]
    hb = hcat[:tm]

    pa = _dot(hb, wlat_ref[...])
    x_in = _dot(hcat, wcx_ref[...])
    b_gate = _dot(hb, wcb_ref[...])
    c_gate = _dot(hcat, wcc_ref[...])

    tabs = [[tab_ref[0], tab_ref[1], tab_ref[2]]]
    tabs.append([pltpu.roll(t, HALF, 1) for t in tabs[0]])
    kr = _rope_block(pa[:, C_KR:C_KR + LANES], *tabs[0])
    _kv_from(pa, kr, kvg_ref, wkn_ref, wvt_ref, k_ref, vt_ref)
    cq = _rms(pa[:, C_Q:C_Q + Q_LORA_RANK], qg_ref[...]).astype(BF16)
    qf = _dot(cq, wq_ref[...])
    for h in range(N_HEADS):
        sl = slice(h * LANES, (h + 1) * LANES)
        q_ref[sl, :] = _rope_block(qf[:, sl], *tabs[h % 2]).T.astype(BF16)

    gg_ref[:, :D_MODEL] = _sigmoid(_dot(hb, wga_ref[...])).astype(BF16)

    conv = _conv3(c_gate * x_in, tm, first, last, cw_ref, cb_ref, slice(0, CONV_DIM))
    y_conv = _dot((b_gate * conv).astype(BF16), wco_ref[...])
    gg_ref[:, D_MODEL:] = (_sigmoid(_dot(hb, wgc_ref[...])) * y_conv).astype(BF16)


def _halo_specs(tm, cols, rows=HALO):
    tps = SEQ // tm
    r = tm // rows
    nblk = BATCH * SEQ // rows
    prev = pl.BlockSpec((rows, cols), lambda s, b: (jnp.maximum((b * tps + s) * r - 1, 0), 0))
    nxt = pl.BlockSpec((rows, cols), lambda s, b: (jnp.minimum((b * tps + s + 1) * r, nblk - 1), 0))
    return prev, nxt


def _inproj(x2, mod_lat, tab, g1, w_lat, w_cx, w_cb, w_cc, w_ga, w_gc, kvg, qg, w_kn, w_vt, w_q_p,
            cw, cb, wco):
    n = x2.shape[0]
    tm = TM_IN
    tps = SEQ // tm
    row = lambda s, b: (b * tps + s, 0)
    xprev, xnext = _halo_specs(tm, D_MODEL)
    return pl.pallas_call(
        _inproj_kernel,
        name="inproj",
        grid=(tps, BATCH),
        in_specs=[
            pl.BlockSpec((tm, D_MODEL), row),
            xprev,
            xnext,
            pl.BlockSpec((1, 6, D_MODEL), lambda s, b: (b, 0, 0)),
            pl.BlockSpec((3, tm, LANES), lambda s, b: (0, s, 0)),
            _const_spec((1, D_MODEL)),
            _const_spec((D_MODEL, C_LAT)),
            _const_spec((D_MODEL, CONV_DIM)),
            _const_spec((D_MODEL, CONV_DIM)),
            _const_spec((D_MODEL, CONV_DIM)),
            _const_spec((D_MODEL, D_MODEL)),
            _const_spec((D_MODEL, D_MODEL)),
            _const_spec((1, KV_LORA_RANK)),
            _const_spec((1, Q_LORA_RANK)),
            _const_spec((KV_LORA_RANK, N_HEADS * QK_NOPE_DIM)),
            _const_spec((ATTN_DIM, KV_LORA_RANK)),
            _const_spec((Q_LORA_RANK, HEAD_COLS)),
            _const_spec((3, CONV_DIM)),
            _const_spec((1, CONV_DIM)),
            _const_spec((CONV_DIM, D_MODEL)),
        ],
        out_specs=[pl.BlockSpec((HEAD_COLS, tm), lambda s, b: (0, b * tps + s)),
                   pl.BlockSpec((tm, HEAD_COLS), row),
                   pl.BlockSpec((ATTN_DIM, tm), lambda s, b: (b, s)),
                   pl.BlockSpec((tm, 2 * D_MODEL), row)],
        out_shape=[jax.ShapeDtypeStruct((HEAD_COLS, n), BF16), jax.ShapeDtypeStruct((n, HEAD_COLS), BF16),
                   jax.ShapeDtypeStruct((BATCH * ATTN_DIM, SEQ), BF16),
                   jax.ShapeDtypeStruct((n, 2 * D_MODEL), BF16)],
        compiler_params=_params(2),
    )(x2, x2, x2, mod_lat, tab, g1, w_lat, w_cx, w_cb, w_cc, w_ga, w_gc, kvg, qg, w_kn, w_vt, w_q_p,
      cw, cb, wco)


def _ctxproj_kernel(x_ref, mod_ref, g1_ref, win_ref, kvg_ref, wkn_ref, wvt_ref, k_ref, vt_ref):
    sh = mod_ref[0, 0:1, :]
    sc = mod_ref[0, 1:2, :]
    hb = (_rms(x_ref[...], g1_ref[...]) * (1.0 + sc) + sh).astype(BF16)
    pa = _dot(hb, win_ref[...])
    _kv_from(pa, pa[:, C_KR:C_KR + LANES], kvg_ref, wkn_ref, wvt_ref, k_ref, vt_ref)


def _ctxproj(c2, mod_ctx, g1, w_lat, kvg, w_kn, w_vt):
    n = c2.shape[0]
    seqs = 4
    tm = seqs * CTX_LEN
    row = lambda i: (i, 0)
    return pl.pallas_call(
        _ctxproj_kernel,
        name="ctxproj",
        grid=(n // tm,),
        in_specs=[
            pl.BlockSpec((tm, D_MODEL), row),
            _const_spec((1, 6, D_MODEL)),
            _const_spec((1, D_MODEL)),
            _const_spec((D_MODEL, C_Q)),
            _const_spec((1, KV_LORA_RANK)),
            _const_spec((KV_LORA_RANK, N_HEADS * QK_NOPE_DIM)),
            _const_spec((ATTN_DIM, KV_LORA_RANK)),
        ],
        out_specs=[pl.BlockSpec((tm, HEAD_COLS), row),
                   pl.BlockSpec((seqs * ATTN_DIM, CTX_LEN), row)],
        out_shape=[jax.ShapeDtypeStruct((n, HEAD_COLS), BF16),
                   jax.ShapeDtypeStruct((BATCH * ATTN_DIM, CTX_LEN), BF16)],
        compiler_params=_params(1),
    )(c2, mod_ctx, g1, w_lat, kvg, w_kn, w_vt)


def _attn_kernel(qt_ref, kl_ref, vlt_ref, kc_ref, vct_ref, *rest):
    n_cast = len(rest) // 2
    for src, dst in zip(rest[:n_cast], rest[n_cast + 1:]):
        dst[...] = src[...].astype(BF16)
    o_ref = rest[n_cast]

    def scores(unit):
        r, h = unit
        sl = slice(h * LANES, (h + 1) * LANES)
        qt = qt_ref[sl, r * TQ:(r + 1) * TQ]
        chunks = [_dot(kc_ref[:, sl], qt)] + [
            _dot(kl_ref[c * KEY_CHUNK:(c + 1) * KEY_CHUNK, sl], qt) for c in range(SEQ // KEY_CHUNK)]
        pm = None
        for s in chunks:
            cm = jnp.max(s.reshape(s.shape[0] // HALO, HALO, TQ), axis=0)
            pm = cm if pm is None else jnp.maximum(pm, cm)
        return chunks, jnp.max(pm, axis=0, keepdims=True)

    units = [(r, h) for r in range(Q_SUB) for h in range(N_HEADS)]
    ahead = [scores(u) for u in units[:SCORES_AHEAD]]
    halves = []
    for i, (r, h) in enumerate(units):
        chunks, m = ahead.pop(0)
        if i + SCORES_AHEAD < len(units):
            ahead.append(scores(units[i + SCORES_AHEAD]))
        vsl = slice((h // 2) * LANES, (h // 2 + 1) * LANES)
        vct = jnp.concatenate([vct_ref[vsl, :], jnp.ones((DEN_ROWS, CTX_LEN), BF16)], axis=0)
        vlt = jnp.concatenate([vlt_ref[vsl, :], jnp.ones((DEN_ROWS, SEQ), BF16)], axis=0)
        acc = _dot(vct, jnp.exp2(chunks[0] - m).astype(BF16))
        for c, s in enumerate(chunks[1:]):
            acc += _dot(vlt[:, c * KEY_CHUNK:(c + 1) * KEY_CHUNK], jnp.exp2(s - m).astype(BF16))
        rows = slice((h % 2) * V_HEAD_DIM, (h % 2 + 1) * V_HEAD_DIM)
        halves.append(acc[rows] * (1.0 / acc[LANES:LANES + 1]))
        if h % 2 == 1:
            o_ref[r * TQ:(r + 1) * TQ, vsl] = jnp.concatenate(halves[-2:], axis=0).T.astype(BF16)


def _attn(q, kl, vlt, kc, vct, cast_weights):
    tq = TQ * Q_SUB
    nq = SEQ // tq
    steps = BATCH * nq
    cast_specs = []
    for w in cast_weights:
        rows, cols = w.shape
        blk = next(r for r in range(CAST_ROWS, rows + 1, CAST_ROWS)
                   if rows % r == 0 and r * steps >= rows)
        nblk = rows // blk
        cast_specs.append(pl.BlockSpec((blk, cols), functools.partial(
            lambda b, j, nblk: (jnp.minimum(b * nq + j, nblk - 1), 0), nblk=nblk)))
    outs = pl.pallas_call(
        _attn_kernel,
        name="attn",
        grid=(BATCH, nq),
        in_specs=[
            pl.BlockSpec((HEAD_COLS, tq), lambda b, j: (0, b * nq + j)),
            pl.BlockSpec((SEQ, HEAD_COLS), lambda b, j: (b, 0)),
            pl.BlockSpec((ATTN_DIM, SEQ), lambda b, j: (b, 0)),
            pl.BlockSpec((CTX_LEN, HEAD_COLS), lambda b, j: (b, 0)),
            pl.BlockSpec((ATTN_DIM, CTX_LEN), lambda b, j: (b, 0)),
        ] + cast_specs,
        out_specs=[pl.BlockSpec((tq, ATTN_DIM), lambda b, j: (b * nq + j, 0))] + cast_specs,
        out_shape=[jax.ShapeDtypeStruct((BATCH * SEQ, ATTN_DIM), BF16)]
                  + [jax.ShapeDtypeStruct(w.shape, BF16) for w in cast_weights],
        compiler_params=_params(2),
    )(q, kl, vlt, kc, vct, *cast_weights)
    return outs[0], outs[1:]


def _halo_rows_bf16(prev_ref, next_ref):
    p = prev_ref[...].astype(F32)[HALO_BF16 - HALO:]
    n = next_ref[...].astype(F32)[:HALO]
    return jnp.concatenate([p, n], axis=0).astype(BF16)


def _post_kernel(o_ref, op_ref, on_ref, gg_ref, gp_ref, gn_ref, x_ref, xp_ref, xn_ref, mod_ref,
                 wao_ref, wo_ref, g2_ref, wup_ref, fcw_ref, fcb_ref, wdn_ref, gf_ref,
                 out_ref, act_ref):
    s = pl.program_id(0)
    first = s == 0
    last = s == pl.num_programs(0) - 1
    hm = TM // 2
    os_ = (o_ref[:hm], jnp.concatenate([o_ref[hm:], _halo_rows_bf16(op_ref, on_ref)], axis=0))
    ggs = (gg_ref[:hm], jnp.concatenate([gg_ref[hm:], _halo_rows_bf16(gp_ref, gn_ref)], axis=0))
    xs = (x_ref[:hm], jnp.concatenate([x_ref[hm:], xp_ref[...], xn_ref[...]], axis=0))
    y_attn = [_dot(o, wao_ref[...]) for o in os_]
    ms = [(gg[:, :D_MODEL].astype(F32) * y + gg[:, D_MODEL:].astype(F32)).astype(BF16)
          for gg, y in zip(ggs, y_attn)]
    x1s = [x + mod_ref[0, 2:3, :] * _dot(m, wo_ref[...]) for x, m in zip(xs, ms)]
    h2 = jnp.concatenate(
        [(_rms(x1, g2_ref[...]) * (1.0 + mod_ref[0, 4:5, :]) + mod_ref[0, 3:4, :]).astype(BF16)
         for x1 in x1s], axis=0)

    for c0 in range(0, D_FF, FF_CHUNK):
        c1 = min(c0 + FF_CHUNK, D_FF)
        gsl = slice(c0, c1)
        vsl = slice(D_FF + c0, D_FF + c1)
        gate = _conv3(_dot(h2, wup_ref[:, gsl]), TM, first, last, fcw_ref, fcb_ref, gsl)
        val = _conv3(_dot(h2, wup_ref[:, vsl]), TM, first, last, fcw_ref, fcb_ref, vsl)
        act_ref[:, gsl] = (gate * _sigmoid(gate) * val).astype(BF16)

    ys = [_dot(act_ref[r * hm:(r + 1) * hm, :], wdn_ref[...]) for r in range(2)]
    for r in range(2):
        x2 = x1s[r][:hm] + mod_ref[0, 5:6, :] * ys[r]
        out_ref[r * hm:(r + 1) * hm, :] = _rms(x2, gf_ref[...])


def _post(o, gg, x2, mod_lat, wao, wo, g2, wup, fcw, fcb, wdn, gf):
    n = x2.shape[0]
    tps = SEQ // TM
    row = lambda s, b: (b * tps + s, 0)
    oprev, onext = _halo_specs(TM, ATTN_DIM, HALO_BF16)
    gprev, gnext = _halo_specs(TM, 2 * D_MODEL, HALO_BF16)
    xprev, xnext = _halo_specs(TM, D_MODEL)
    return pl.pallas_call(
        _post_kernel,
        name="post",
        grid=(tps, BATCH),
        in_specs=[
            pl.BlockSpec((TM, ATTN_DIM), row), oprev, onext,
            pl.BlockSpec((TM, 2 * D_MODEL), row), gprev, gnext,
            pl.BlockSpec((TM, D_MODEL), row), xprev, xnext,
            pl.BlockSpec((1, 6, D_MODEL), lambda s, b: (b, 0, 0)),
            _const_spec((ATTN_DIM, D_MODEL)),
            _const_spec((D_MODEL, D_MODEL)),
            _const_spec((1, D_MODEL)),
            _const_spec((D_MODEL, 2 * D_FF)),
            _const_spec((3, 2 * D_FF)),
            _const_spec((1, 2 * D_FF)),
            _const_spec((D_FF, D_MODEL)),
            _const_spec((1, D_MODEL)),
        ],
        out_specs=pl.BlockSpec((TM, D_MODEL), row),
        out_shape=jax.ShapeDtypeStruct((n, D_MODEL), F32),
        scratch_shapes=[pltpu.VMEM((TM, D_FF), BF16)],
        compiler_params=_params(2),
    )(o, o, o, gg, gg, gg, x2, x2, x2, mod_lat, wao, wo, g2, wup, fcw, fcb, wdn, gf)


def _rope_tables():
    rows = SEQ // GRID_W
    row = jnp.repeat(jnp.arange(rows), GRID_W).astype(F32)
    col = jnp.tile(jnp.arange(GRID_W), rows).astype(F32)
    axis_dim = QK_ROPE_DIM // 2
    inv = ROPE_THETA ** (-jnp.arange(0, axis_dim, 2, dtype=F32) / axis_dim)
    ang = jnp.concatenate([row[:, None] * inv, col[:, None] * inv], axis=-1)
    j = jnp.arange(QK_ROPE_DIM)
    idx = (j // axis_dim) * ROPE_HALF + j % ROPE_HALF
    first_half = ((j % axis_dim) // ROPE_HALF) == 0
    cos = jnp.cos(ang)[:, idx]
    sin = jnp.sin(ang)[:, idx]
    s0 = jnp.where(first_half, -sin, 0.0)
    s1 = jnp.where(first_half, 0.0, sin)
    pad = jnp.zeros((SEQ, LANES - QK_NOPE_DIM - QK_ROPE_DIM), F32)

    def even(nope_val, rope_part):
        return jnp.concatenate([jnp.full((SEQ, QK_NOPE_DIM), nope_val, F32), rope_part, pad], axis=1)

    return jnp.stack([even(1.0, cos), even(0.0, s0), even(0.0, s1)])


def _split_w_in_kernel(wt_ref, lat_ref, cx_ref, cb_ref, cc_ref, ga_ref, gc_ref):
    kr_end = KV_LORA_RANK + QK_ROPE_DIM
    krq = wt_ref[KV_LORA_RANK:KV_LORA_RANK + LANES, :].T
    lane = lax.broadcasted_iota(jnp.int32, krq.shape, 1)
    kr = jnp.where((lane >= QK_NOPE_DIM) & (lane < QK_NOPE_DIM + QK_ROPE_DIM),
                   pltpu.roll(krq, QK_NOPE_DIM, 1), 0.0)
    lat_ref[...] = jnp.concatenate([wt_ref[:KV_LORA_RANK, :].T, kr, wt_ref[kr_end:O_CX, :].T],
                                   axis=1).astype(BF16)
    for ref, (a, b) in ((cx_ref, (O_CX, O_CB)), (cb_ref, (O_CB, O_CC)), (cc_ref, (O_CC, O_GA)),
                        (ga_ref, (O_GA, O_GC)), (gc_ref, (O_GC, O_END))):
        ref[...] = wt_ref[a:b, :].T.astype(BF16)


def _split_w_in(wt):
    widths = (C_LAT, CONV_DIM, CONV_DIM, CONV_DIM, D_MODEL, D_MODEL)
    return pl.pallas_call(
        _split_w_in_kernel,
        name="split_w_in",
        grid=(D_MODEL // LANES,),
        in_specs=[pl.BlockSpec((O_END, LANES), lambda i: (0, i))],
        out_specs=[pl.BlockSpec((LANES, n), lambda i: (i, 0)) for n in widths],
        out_shape=[jax.ShapeDtypeStruct((D_MODEL, n), BF16) for n in widths],
        compiler_params=_params(1),
    )(wt)


def _pack_w_ukv(w):
    w = w.astype(BF16).reshape(KV_LORA_RANK, N_HEADS, QK_NOPE_DIM + V_HEAD_DIM)
    return (w[..., :QK_NOPE_DIM].reshape(KV_LORA_RANK, N_HEADS * QK_NOPE_DIM),
            w[..., QK_NOPE_DIM:].reshape(KV_LORA_RANK, ATTN_DIM).T)


def _pack_w_uq(w):
    qscale = (QK_NOPE_DIM + QK_ROPE_DIM) ** -0.5 * LOG2E
    w = (w * qscale).astype(BF16).reshape(Q_LORA_RANK, N_HEADS // 2, 2, QK_NOPE_DIM + QK_ROPE_DIM)
    pad = jnp.zeros((Q_LORA_RANK, N_HEADS // 2, LANES - QK_NOPE_DIM - QK_ROPE_DIM), BF16)
    ev, od = w[:, :, 0], w[:, :, 1]
    ev = jnp.concatenate([ev, pad], axis=-1)
    od = jnp.concatenate([od[..., QK_NOPE_DIM:], pad, od[..., :QK_NOPE_DIM]], axis=-1)
    return jnp.stack([ev, od], axis=2).reshape(Q_LORA_RANK, HEAD_COLS)


def kernel(x, c, ctx, c_ctx, w_ada, b_ada, norm1_g, w_in, q_norm_g, kv_norm_g, w_uq, w_ukv,
           conv_w, conv_b, w_attn_out, w_conv_out, w_o, norm2_g, w_up, ffn_conv_w, ffn_conv_b,
           w_down, final_g):
    assert x.shape == (BATCH, SEQ, D_MODEL) and ctx.shape == (BATCH, CTX_LEN, D_MODEL)
    assert w_ada.shape[0] == 1, "single-layer block"

    cvec = jnp.concatenate([c, c_ctx[None, :], jnp.zeros((16 - BATCH - 1, D_MODEL), F32)], axis=0)
    mod = _ada(cvec, w_ada.reshape(D_MODEL, 6 * D_MODEL), b_ada.reshape(1, 6 * D_MODEL))
    mod_lat = mod[:BATCH].reshape(BATCH, 6, D_MODEL)
    mod_ctx = mod[BATCH:BATCH + 1].reshape(1, 6, D_MODEL)

    tab = _rope_tables()
    w_lat, w_cx, w_cb, w_cc, w_ga, w_gc = _split_w_in(w_in.reshape(D_MODEL, O_END).T)
    w_kn, w_vt = _pack_w_ukv(w_ukv.reshape(KV_LORA_RANK, -1))
    w_q_p = _pack_w_uq(w_uq.reshape(Q_LORA_RANK, -1))
    g1 = norm1_g.reshape(1, D_MODEL)
    kvg = kv_norm_g.reshape(1, KV_LORA_RANK)
    qg = q_norm_g.reshape(1, Q_LORA_RANK)

    x2 = x.reshape(BATCH * SEQ, D_MODEL)
    c2 = ctx.reshape(BATCH * CTX_LEN, D_MODEL)

    kc, vct = _ctxproj(c2, mod_ctx, g1, w_lat, kvg, w_kn, w_vt)
    q, kl, vlt, gg = _inproj(x2, mod_lat, tab, g1, w_lat, w_cx, w_cb, w_cc, w_ga, w_gc, kvg, qg,
                             w_kn, w_vt, w_q_p,
                            conv_w.reshape(3, CONV_DIM), conv_b.reshape(1, CONV_DIM),
                            w_conv_out.reshape(CONV_DIM, D_MODEL).astype(BF16))
    o, (wao, wo, wup, wdn) = _attn(q, kl, vlt, kc, vct, (
        w_attn_out.reshape(ATTN_DIM, D_MODEL), w_o.reshape(D_MODEL, D_MODEL),
        w_up.reshape(D_MODEL, 2 * D_FF), w_down.reshape(D_FF, D_MODEL)))
    out = _post(o, gg, x2, mod_lat, wao, wo, norm2_g.reshape(1, D_MODEL), wup,
                ffn_conv_w.reshape(3, 2 * D_FF), ffn_conv_b.reshape(1, 2 * D_FF), wdn,
                final_g.reshape(1, D_MODEL))
    return out.reshape(BATCH, SEQ, D_MODEL)
```

```python
import functools

import jax
import jax.numpy as jnp
from jax import lax
from jax.experimental import pallas as pl
from jax.experimental.pallas import tpu as pltpu

D_MODEL = 1024
BATCH = 8
SEQ = 2048
GRID_W = 64
CTX_LEN = 256
N_HEADS = 8
QK_NOPE_DIM = 64
QK_ROPE_DIM = 32
V_HEAD_DIM = 64
Q_LORA_RANK = 384
KV_LORA_RANK = 256
ROPE_THETA = 10000.0
CONV_DIM = 512
D_FF = 2816
EPS = 1e-6
ATTN_DIM = N_HEADS * V_HEAD_DIM

LANES = 128
HALF = LANES // 2
HEAD_COLS = N_HEADS * LANES
ROPE_HALF = QK_ROPE_DIM // 4
HALO = 8
HALO_BF16 = 16
CAST_ROWS = 16
DEN_ROWS = 16

C_KV = 0
C_KR = C_KV + KV_LORA_RANK
C_Q = C_KR + LANES
C_LAT = C_Q + Q_LORA_RANK
O_CX = KV_LORA_RANK + QK_ROPE_DIM + Q_LORA_RANK
O_CB = O_CX + CONV_DIM
O_CC = O_CB + CONV_DIM
O_GA = O_CC + CONV_DIM
O_GC = O_GA + D_MODEL
O_END = O_GC + D_MODEL

LOG2E = 1.4426950408889634

VMEM_LIMIT = 56 * 1024 * 1024

TM_IN = 1024
TM = 512
TQ = 1024
Q_SUB = 1
SCORES_AHEAD = 1
KEY_CHUNK = 2048
FF_CHUNK = 256

BF16 = jnp.bfloat16
F32 = jnp.float32


def _dot(a, b):
    return jnp.dot(a, b, preferred_element_type=F32)


def _dot_nt(a, b):
    return lax.dot_general(a, b, (((1,), (1,)), ((), ())), preferred_element_type=F32)


def _rms(x, g):
    return x * lax.rsqrt(jnp.mean(x * x, axis=-1, keepdims=True) + EPS) * g


def _sigmoid(x):
    return 1.0 / (1.0 + jnp.exp(-x))


def _rope_block(xh, c, s0, s1):
    return (xh * c + pltpu.roll(xh, LANES - ROPE_HALF, 1) * s0
            + pltpu.roll(xh, ROPE_HALF, 1) * s1)


def _conv3(u_all, m, first, last, w_ref, b_ref, sl):
    u = u_all[:m]
    ext = jnp.concatenate([jnp.where(first, 0.0, u_all[m:m + HALO]), u,
                           jnp.where(last, 0.0, u_all[m + HALO:])], axis=0)
    u_dn = pltpu.roll(ext, 1, 0)[HALO:HALO + m]
    u_up = pltpu.roll(ext, m + 2 * HALO - 1, 0)[HALO:HALO + m]
    return b_ref[:, sl] + u_dn * w_ref[0:1, sl] + u * w_ref[1:2, sl] + u_up * w_ref[2:3, sl]


def _const_spec(shape):
    nd = len(shape)
    return pl.BlockSpec(shape, lambda *_: (0,) * nd, pipeline_mode=pl.Buffered(1))


def _params(n_grid):
    return pltpu.CompilerParams(dimension_semantics=("arbitrary",) * n_grid,
                                vmem_limit_bytes=VMEM_LIMIT)


def _ada_kernel(c_ref, w_ref, b_ref, o_ref):
    c = c_ref[...]
    s = (c * _sigmoid(c)).astype(BF16)
    o_ref[...] = _dot(s, w_ref[...].astype(BF16)) + b_ref[...]


def _ada(cvec, w_ada, b_ada):
    rows = cvec.shape[0]
    n = w_ada.shape[1]
    tn = 1024
    return pl.pallas_call(
        _ada_kernel,
        name="ada",
        grid=(n // tn,),
        in_specs=[pl.BlockSpec((rows, D_MODEL), lambda j: (0, 0)),
                  pl.BlockSpec((D_MODEL, tn), lambda j: (0, j)),
                  pl.BlockSpec((1, tn), lambda j: (0, j))],
        out_specs=pl.BlockSpec((rows, tn), lambda j: (0, j)),
        out_shape=jax.ShapeDtypeStruct((rows, n), F32),
        compiler_params=_params(1),
    )(cvec, w_ada, b_ada)


def _kv_from(pa, kr_even, kvg_ref, wkn_ref, wvt_ref, k_ref, vt_ref):
    ckv = _rms(pa[:, C_KV:C_KV + KV_LORA_RANK], kvg_ref[...]).astype(BF16)
    kn = _dot(ckv, wkn_ref[...])
    kr_odd = pltpu.roll(kr_even, HALF, 1)
    lo = lax.broadcasted_iota(jnp.int32, kr_even.shape, 1) < HALF
    for p in range(N_HEADS // 2):
        kn_pair = kn[:, p * LANES:(p + 1) * LANES]
        k_ref[:, (2 * p) * LANES:(2 * p + 1) * LANES] = jnp.where(lo, kn_pair, kr_even).astype(BF16)
        k_ref[:, (2 * p + 1) * LANES:(2 * p + 2) * LANES] = jnp.where(lo, kr_odd, kn_pair).astype(BF16)
    vt = _dot_nt(wvt_ref[...], ckv).astype(BF16)
    per_seq = vt_ref.shape[1]
    for b in range(vt.shape[1] // per_seq):
        vt_ref[b * ATTN_DIM:(b + 1) * ATTN_DIM, :] = vt[:, b * per_seq:(b + 1) * per_seq]


def _inproj_kernel(x_ref, xp_ref, xn_ref, mod_ref, tab_ref, g1_ref, wlat_ref, wcx_ref, wcb_ref,
                   wcc_ref, wga_ref, wgc_ref, kvg_ref, qg_ref, wkn_ref, wvt_ref, wq_ref, cw_ref,
                   cb_ref, wco_ref, q_ref, k_ref, vt_ref, gg_ref):
    s = pl.program_id(0)
    first = s == 0
    last = s == pl.num_programs(0) - 1
    sh = mod_ref[0, 0:1, :]
    sc = mod_ref[0, 1:2, :]
    g1 = g1_ref[...]
    xcat = jnp.concatenate([x_ref[...], xp_ref[...], xn_ref[...]], axis=0)
    hcat = (_rms(xcat, g1) * (1.0 + sc) + sh).astype(BF16)
    tm = x_ref.shape[0]
    hb = hcat[:tm]

    pa = _dot(hb, wlat_ref[...])
    x_in = _dot(hcat, wcx_ref[...])
    b_gate = _dot(hb, wcb_ref[...])
    c_gate = _dot(hcat, wcc_ref[...])

    tabs = [[tab_ref[0], tab_ref[1], tab_ref[2]]]
    tabs.append([pltpu.roll(t, HALF, 1) for t in tabs[0]])
    kr = _rope_block(pa[:, C_KR:C_KR + LANES], *tabs[0])
    _kv_from(pa, kr, kvg_ref, wkn_ref, wvt_ref, k_ref, vt_ref)
    cq = _rms(pa[:, C_Q:C_Q + Q_LORA_RANK], qg_ref[...]).astype(BF16)
    qf = _dot(cq, wq_ref[...])
    for h in range(N_HEADS):
        sl = slice(h * LANES, (h + 1) * LANES)
        q_ref[sl, :] = _rope_block(qf[:, sl], *tabs[h % 2]).T.astype(BF16)

    gg_ref[:, :D_MODEL] = _sigmoid(_dot(hb, wga_ref[...])).astype(BF16)

    conv = _conv3(c_gate * x_in, tm, first, last, cw_ref, cb_ref, slice(0, CONV_DIM))
    y_conv = _dot((b_gate * conv).astype(BF16), wco_ref[...])
    gg_ref[:, D_MODEL:] = (_sigmoid(_dot(hb, wgc_ref[...])) * y_conv).astype(BF16)


def _halo_specs(tm, cols, rows=HALO):
    tps = SEQ // tm
    r = tm // rows
    nblk = BATCH * SEQ // rows
    prev = pl.BlockSpec((rows, cols), lambda s, b: (jnp.maximum((b * tps + s) * r - 1, 0), 0))
    nxt = pl.BlockSpec((rows, cols), lambda s, b: (jnp.minimum((b * tps + s + 1) * r, nblk - 1), 0))
    return prev, nxt


def _inproj(x2, mod_lat, tab, g1, w_lat, w_cx, w_cb, w_cc, w_ga, w_gc, kvg, qg, w_kn, w_vt, w_q_p,
            cw, cb, wco):
    n = x2.shape[0]
    tm = TM_IN
    tps = SEQ // tm
    row = lambda s, b: (b * tps + s, 0)
    xprev, xnext = _halo_specs(tm, D_MODEL)
    return pl.pallas_call(
        _inproj_kernel,
        name="inproj",
        grid=(tps, BATCH),
        in_specs=[
            pl.BlockSpec((tm, D_MODEL), row),
            xprev,
            xnext,
            pl.BlockSpec((1, 6, D_MODEL), lambda s, b: (b, 0, 0)),
            pl.BlockSpec((3, tm, LANES), lambda s, b: (0, s, 0)),
            _const_spec((1, D_MODEL)),
            _const_spec((D_MODEL, C_LAT)),
            _const_spec((D_MODEL, CONV_DIM)),
            _const_spec((D_MODEL, CONV_DIM)),
            _const_spec((D_MODEL, CONV_DIM)),
            _const_spec((D_MODEL, D_MODEL)),
            _const_spec((D_MODEL, D_MODEL)),
            _const_spec((1, KV_LORA_RANK)),
            _const_spec((1, Q_LORA_RANK)),
            _const_spec((KV_LORA_RANK, N_HEADS * QK_NOPE_DIM)),
            _const_spec((ATTN_DIM, KV_LORA_RANK)),
            _const_spec((Q_LORA_RANK, HEAD_COLS)),
            _const_spec((3, CONV_DIM)),
            _const_spec((1, CONV_DIM)),
            _const_spec((CONV_DIM, D_MODEL)),
        ],
        out_specs=[pl.BlockSpec((HEAD_COLS, tm), lambda s, b: (0, b * tps + s)),
                   pl.BlockSpec((tm, HEAD_COLS), row),
                   pl.BlockSpec((ATTN_DIM, tm), lambda s, b: (b, s)),
                   pl.BlockSpec((tm, 2 * D_MODEL), row)],
        out_shape=[jax.ShapeDtypeStruct((HEAD_COLS, n), BF16), jax.ShapeDtypeStruct((n, HEAD_COLS), BF16),
                   jax.ShapeDtypeStruct((BATCH * ATTN_DIM, SEQ), BF16),
                   jax.ShapeDtypeStruct((n, 2 * D_MODEL), BF16)],
        compiler_params=_params(2),
    )(x2, x2, x2, mod_lat, tab, g1, w_lat, w_cx, w_cb, w_cc, w_ga, w_gc, kvg, qg, w_kn, w_vt, w_q_p,
      cw, cb, wco)


def _ctxproj_kernel(x_ref, mod_ref, g1_ref, win_ref, kvg_ref, wkn_ref, wvt_ref, k_ref, vt_ref):
    sh = mod_ref[0, 0:1, :]
    sc = mod_ref[0, 1:2, :]
    hb = (_rms(x_ref[...], g1_ref[...]) * (1.0 + sc) + sh).astype(BF16)
    pa = _dot(hb, win_ref[...])
    _kv_from(pa, pa[:, C_KR:C_KR + LANES], kvg_ref, wkn_ref, wvt_ref, k_ref, vt_ref)


def _ctxproj(c2, mod_ctx, g1, w_lat, kvg, w_kn, w_vt):
    n = c2.shape[0]
    seqs = 4
    tm = seqs * CTX_LEN
    row = lambda i: (i, 0)
    return pl.pallas_call(
        _ctxproj_kernel,
        name="ctxproj",
        grid=(n // tm,),
        in_specs=[
            pl.BlockSpec((tm, D_MODEL), row),
            _const_spec((1, 6, D_MODEL)),
            _const_spec((1, D_MODEL)),
            _const_spec((D_MODEL, C_Q)),
            _const_spec((1, KV_LORA_RANK)),
            _const_spec((KV_LORA_RANK, N_HEADS * QK_NOPE_DIM)),
            _const_spec((ATTN_DIM, KV_LORA_RANK)),
        ],
        out_specs=[pl.BlockSpec((tm, HEAD_COLS), row),
                   pl.BlockSpec((seqs * ATTN_DIM, CTX_LEN), row)],
        out_shape=[jax.ShapeDtypeStruct((n, HEAD_COLS), BF16),
                   jax.ShapeDtypeStruct((BATCH * ATTN_DIM, CTX_LEN), BF16)],
        compiler_params=_params(1),
    )(c2, mod_ctx, g1, w_lat, kvg, w_kn, w_vt)


def _attn_kernel(qt_ref, kl_ref, vlt_ref, kc_ref, vct_ref, *rest):
    n_cast = len(rest) // 2
    for src, dst in zip(rest[:n_cast], rest[n_cast + 1:]):
        dst[...] = src[...].astype(BF16)
    o_ref = rest[n_cast]

    def scores(unit):
        r, h = unit
        sl = slice(h * LANES, (h + 1) * LANES)
        qt = qt_ref[sl, r * TQ:(r + 1) * TQ]
        chunks = [_dot(kc_ref[:, sl], qt)] + [
            _dot(kl_ref[c * KEY_CHUNK:(c + 1) * KEY_CHUNK, sl], qt) for c in range(SEQ // KEY_CHUNK)]
        pm = None
        for s in chunks:
            cm = jnp.max(s.reshape(s.shape[0] // HALO, HALO, TQ), axis=0)
            pm = cm if pm is None else jnp.maximum(pm, cm)
        return chunks, jnp.max(pm, axis=0, keepdims=True)

    units = [(r, h) for r in range(Q_SUB) for h in range(N_HEADS)]
    ahead = [scores(u) for u in units[:SCORES_AHEAD]]
    halves = []
    for i, (r, h) in enumerate(units):
        chunks, m = ahead.pop(0)
        if i + SCORES_AHEAD < len(units):
            ahead.append(scores(units[i + SCORES_AHEAD]))
        vsl = slice((h // 2) * LANES, (h // 2 + 1) * LANES)
        vct = jnp.concatenate([vct_ref[vsl, :], jnp.ones((DEN_ROWS, CTX_LEN), BF16)], axis=0)
        vlt = jnp.concatenate([vlt_ref[vsl, :], jnp.ones((DEN_ROWS, SEQ), BF16)], axis=0)
        acc = _dot(vct, jnp.exp2(chunks[0] - m).astype(BF16))
        for c, s in enumerate(chunks[1:]):
            acc += _dot(vlt[:, c * KEY_CHUNK:(c + 1) * KEY_CHUNK], jnp.exp2(s - m).astype(BF16))
        rows = slice((h % 2) * V_HEAD_DIM, (h % 2 + 1) * V_HEAD_DIM)
        halves.append(acc[rows] * (1.0 / acc[LANES:LANES + 1]))
        if h % 2 == 1:
            o_ref[r * TQ:(r + 1) * TQ, vsl] = jnp.concatenate(halves[-2:], axis=0).T.astype(BF16)


def _attn(q, kl, vlt, kc, vct, cast_weights):
    tq = TQ * Q_SUB
    nq = SEQ // tq
    steps = BATCH * nq
    cast_specs = []
    for w in cast_weights:
        rows, cols = w.shape
        blk = next(r for r in range(CAST_ROWS, rows + 1, CAST_ROWS)
                   if rows % r == 0 and r * steps >= rows)
        nblk = rows // blk
        cast_specs.append(pl.BlockSpec((blk, cols), functools.partial(
            lambda b, j, nblk: (jnp.minimum(b * nq + j, nblk - 1), 0), nblk=nblk)))
    outs = pl.pallas_call(
        _attn_kernel,
        name="attn",
        grid=(BATCH, nq),
        in_specs=[
            pl.BlockSpec((HEAD_COLS, tq), lambda b, j: (0, b * nq + j)),
            pl.BlockSpec((SEQ, HEAD_COLS), lambda b, j: (b, 0)),
            pl.BlockSpec((ATTN_DIM, SEQ), lambda b, j: (b, 0)),
            pl.BlockSpec((CTX_LEN, HEAD_COLS), lambda b, j: (b, 0)),
            pl.BlockSpec((ATTN_DIM, CTX_LEN), lambda b, j: (b, 0)),
        ] + cast_specs,
        out_specs=[pl.BlockSpec((tq, ATTN_DIM), lambda b, j: (b * nq + j, 0))] + cast_specs,
        out_shape=[jax.ShapeDtypeStruct((BATCH * SEQ, ATTN_DIM), BF16)]
                  + [jax.ShapeDtypeStruct(w.shape, BF16) for w in cast_weights],
        compiler_params=_params(2),
    )(q, kl, vlt, kc, vct, *cast_weights)
    return outs[0], outs[1:]


def _halo_rows_bf16(prev_ref, next_ref):
    p = prev_ref[...].astype(F32)[HALO_BF16 - HALO:]
    n = next_ref[...].astype(F32)[:HALO]
    return jnp.concatenate([p, n], axis=0).astype(BF16)


def _post_kernel(o_ref, op_ref, on_ref, gg_ref, gp_ref, gn_ref, x_ref, xp_ref, xn_ref, mod_ref,
                 wao_ref, wo_ref, g2_ref, wup_ref, fcw_ref, fcb_ref, wdn_ref, gf_ref,
                 out_ref, act_ref):
    s = pl.program_id(0)
    first = s == 0
    last = s == pl.num_programs(0) - 1
    hm = TM // 2
    os_ = (o_ref[:hm], jnp.concatenate([o_ref[hm:], _halo_rows_bf16(op_ref, on_ref)], axis=0))
    ggs = (gg_ref[:hm], jnp.concatenate([gg_ref[hm:], _halo_rows_bf16(gp_ref, gn_ref)], axis=0))
    xs = (x_ref[:hm], jnp.concatenate([x_ref[hm:], xp_ref[...], xn_ref[...]], axis=0))
    y_attn = [_dot(o, wao_ref[...]) for o in os_]
    ms = [(gg[:, :D_MODEL].astype(F32) * y + gg[:, D_MODEL:].astype(F32)).astype(BF16)
          for gg, y in zip(ggs, y_attn)]
    x1s = [x + mod_ref[0, 2:3, :] * _dot(m, wo_ref[...]) for x, m in zip(xs, ms)]
    h2 = jnp.concatenate(
        [(_rms(x1, g2_ref[...]) * (1.0 + mod_ref[0, 4:5, :]) + mod_ref[0, 3:4, :]).astype(BF16)
         for x1 in x1s], axis=0)

    for c0 in range(0, D_FF, FF_CHUNK):
        c1 = min(c0 + FF_CHUNK, D_FF)
        gsl = slice(c0, c1)
        vsl = slice(D_FF + c0, D_FF + c1)
        gate = _conv3(_dot(h2, wup_ref[:, gsl]), TM, first, last, fcw_ref, fcb_ref, gsl)
        val = _conv3(_dot(h2, wup_ref[:, vsl]), TM, first, last, fcw_ref, fcb_ref, vsl)
        act_ref[:, gsl] = (gate * _sigmoid(gate) * val).astype(BF16)

    ys = [_dot(act_ref[r * hm:(r + 1) * hm, :], wdn_ref[...]) for r in range(2)]
    for r in range(2):
        x2 = x1s[r][:hm] + mod_ref[0, 5:6, :] * ys[r]
        out_ref[r * hm:(r + 1) * hm, :] = _rms(x2, gf_ref[...])


def _post(o, gg, x2, mod_lat, wao, wo, g2, wup, fcw, fcb, wdn, gf):
    n = x2.shape[0]
    tps = SEQ // TM
    row = lambda s, b: (b * tps + s, 0)
    oprev, onext = _halo_specs(TM, ATTN_DIM, HALO_BF16)
    gprev, gnext = _halo_specs(TM, 2 * D_MODEL, HALO_BF16)
    xprev, xnext = _halo_specs(TM, D_MODEL)
    return pl.pallas_call(
        _post_kernel,
        name="post",
        grid=(tps, BATCH),
        in_specs=[
            pl.BlockSpec((TM, ATTN_DIM), row), oprev, onext,
            pl.BlockSpec((TM, 2 * D_MODEL), row), gprev, gnext,
            pl.BlockSpec((TM, D_MODEL), row), xprev, xnext,
            pl.BlockSpec((1, 6, D_MODEL), lambda s, b: (b, 0, 0)),
            _const_spec((ATTN_DIM, D_MODEL)),
            _const_spec((D_MODEL, D_MODEL)),
            _const_spec((1, D_MODEL)),
            _const_spec((D_MODEL, 2 * D_FF)),
            _const_spec((3, 2 * D_FF)),
            _const_spec((1, 2 * D_FF)),
            _const_spec((D_FF, D_MODEL)),
            _const_spec((1, D_MODEL)),
        ],
        out_specs=pl.BlockSpec((TM, D_MODEL), row),
        out_shape=jax.ShapeDtypeStruct((n, D_MODEL), F32),
        scratch_shapes=[pltpu.VMEM((TM, D_FF), BF16)],
        compiler_params=_params(2),
    )(o, o, o, gg, gg, gg, x2, x2, x2, mod_lat, wao, wo, g2, wup, fcw, fcb, wdn, gf)


def _rope_tables():
    rows = SEQ // GRID_W
    row = jnp.repeat(jnp.arange(rows), GRID_W).astype(F32)
    col = jnp.tile(jnp.arange(GRID_W), rows).astype(F32)
    axis_dim = QK_ROPE_DIM // 2
    inv = ROPE_THETA ** (-jnp.arange(0, axis_dim, 2, dtype=F32) / axis_dim)
    ang = jnp.concatenate([row[:, None] * inv, col[:, None] * inv], axis=-1)
    j = jnp.arange(QK_ROPE_DIM)
    idx = (j // axis_dim) * ROPE_HALF + j % ROPE_HALF
    first_half = ((j % axis_dim) // ROPE_HALF) == 0
    cos = jnp.cos(ang)[:, idx]
    sin = jnp.sin(ang)[:, idx]
    s0 = jnp.where(first_half, -sin, 0.0)
    s1 = jnp.where(first_half, 0.0, sin)
    pad = jnp.zeros((SEQ, LANES - QK_NOPE_DIM - QK_ROPE_DIM), F32)

    def even(nope_val, rope_part):
        return jnp.concatenate([jnp.full((SEQ, QK_NOPE_DIM), nope_val, F32), rope_part, pad], axis=1)

    return jnp.stack([even(1.0, cos), even(0.0, s0), even(0.0, s1)])


def _split_w_in_kernel(wt_ref, lat_ref, cx_ref, cb_ref, cc_ref, ga_ref, gc_ref):
    kr_end = KV_LORA_RANK + QK_ROPE_DIM
    krq = wt_ref[KV_LORA_RANK:KV_LORA_RANK + LANES, :].T
    lane = lax.broadcasted_iota(jnp.int32, krq.shape, 1)
    kr = jnp.where((lane >= QK_NOPE_DIM) & (lane < QK_NOPE_DIM + QK_ROPE_DIM),
                   pltpu.roll(krq, QK_NOPE_DIM, 1), 0.0)
    lat_ref[...] = jnp.concatenate([wt_ref[:KV_LORA_RANK, :].T, kr, wt_ref[kr_end:O_CX, :].T],
                                   axis=1).astype(BF16)
    for ref, (a, b) in ((cx_ref, (O_CX, O_CB)), (cb_ref, (O_CB, O_CC)), (cc_ref, (O_CC, O_GA)),
                        (ga_ref, (O_GA, O_GC)), (gc_ref, (O_GC, O_END))):
        ref[...] = wt_ref[a:b, :].T.astype(BF16)


def _split_w_in(wt):
    widths = (C_LAT, CONV_DIM, CONV_DIM, CONV_DIM, D_MODEL, D_MODEL)
    return pl.pallas_call(
        _split_w_in_kernel,
        name="split_w_in",
        grid=(D_MODEL // LANES,),
        in_specs=[pl.BlockSpec((O_END, LANES), lambda i: (0, i))],
        out_specs=[pl.BlockSpec((LANES, n), lambda i: (i, 0)) for n in widths],
        out_shape=[jax.ShapeDtypeStruct((D_MODEL, n), BF16) for n in widths],
        compiler_params=_params(1),
    )(wt)


def _pack_w_ukv(w):
    w = w.astype(BF16).reshape(KV_LORA_RANK, N_HEADS, QK_NOPE_DIM + V_HEAD_DIM)
    return (w[..., :QK_NOPE_DIM].reshape(KV_LORA_RANK, N_HEADS * QK_NOPE_DIM),
            w[..., QK_NOPE_DIM:].reshape(KV_LORA_RANK, ATTN_DIM).T)


def _pack_w_uq(w):
    qscale = (QK_NOPE_DIM + QK_ROPE_DIM) ** -0.5 * LOG2E
    w = (w * qscale).astype(BF16).reshape(Q_LORA_RANK, N_HEADS // 2, 2, QK_NOPE_DIM + QK_ROPE_DIM)
    pad = jnp.zeros((Q_LORA_RANK, N_HEADS // 2, LANES - QK_NOPE_DIM - QK_ROPE_DIM), BF16)
    ev, od = w[:, :, 0], w[:, :, 1]
    ev = jnp.concatenate([ev, pad], axis=-1)
    od = jnp.concatenate([od[..., QK_NOPE_DIM:], pad, od[..., :QK_NOPE_DIM]], axis=-1)
    return jnp.stack([ev, od], axis=2).reshape(Q_LORA_RANK, HEAD_COLS)


def kernel(x, c, ctx, c_ctx, w_ada, b_ada, norm1_g, w_in, q_norm_g, kv_norm_g, w_uq, w_ukv,
           conv_w, conv_b, w_attn_out, w_conv_out, w_o, norm2_g, w_up, ffn_conv_w, ffn_conv_b,
           w_down, final_g):
    assert x.shape == (BATCH, SEQ, D_MODEL) and ctx.shape == (BATCH, CTX_LEN, D_MODEL)
    assert w_ada.shape[0] == 1, "single-layer block"

    cvec = jnp.concatenate([c, c_ctx[None, :], jnp.zeros((16 - BATCH - 1, D_MODEL), F32)], axis=0)
    mod = _ada(cvec, w_ada.reshape(D_MODEL, 6 * D_MODEL), b_ada.reshape(1, 6 * D_MODEL))
    mod_lat = mod[:BATCH].reshape(BATCH, 6, D_MODEL)
    mod_ctx = mod[BATCH:BATCH + 1].reshape(1, 6, D_MODEL)

    tab = _rope_tables()
    w_lat, w_cx, w_cb, w_cc, w_ga, w_gc = _split_w_in(w_in.reshape(D_MODEL, O_END).T)
    w_kn, w_vt = _pack_w_ukv(w_ukv.reshape(KV_LORA_RANK, -1))
    w_q_p = _pack_w_uq(w_uq.reshape(Q_LORA_RANK, -1))
    g1 = norm1_g.reshape(1, D_MODEL)
    kvg = kv_norm_g.reshape(1, KV_LORA_RANK)
    qg = q_norm_g.reshape(1, Q_LORA_RANK)

    x2 = x.reshape(BATCH * SEQ, D_MODEL)
    c2 = ctx.reshape(BATCH * CTX_LEN, D_MODEL)

    kc, vct = _ctxproj(c2, mod_ctx, g1, w_lat, kvg, w_kn, w_vt)
    q, kl, vlt, gg = _inproj(x2, mod_lat, tab, g1, w_lat, w_cx, w_cb, w_cc, w_ga, w_gc, kvg, qg,
                             w_kn, w_vt, w_q_p,
                            conv_w.reshape(3, CONV_DIM), conv_b.reshape(1, CONV_DIM),
                            w_conv_out.reshape(CONV_DIM, D_MODEL).astype(BF16))
    o, (wao, wo, wup, wdn) = _attn(q, kl, vlt, kc, vct, (
        w_attn_out.reshape(ATTN_DIM, D_MODEL), w_o.reshape(D_MODEL, D_MODEL),
        w_up.reshape(D_MODEL, 2 * D_FF), w_down.reshape(D_FF, D_MODEL)))
    out = _post(o, gg, x2, mod_lat, wao, wo, norm2_g.reshape(1, D_MODEL), wup,
                ffn_conv_w.reshape(3, 2 * D_FF), ffn_conv_b.reshape(1, 2 * D_FF), wdn,
                final_g.reshape(1, D_MODEL))
    return out.reshape(BATCH, SEQ, D_MODEL)
```

```python
import functools

import jax
import jax.numpy as jnp
from jax import lax
from jax.experimental import pallas as pl
from jax.experimental.pallas import tpu as pltpu

D_MODEL = 1024
BATCH = 8
SEQ = 2048
GRID_W = 64
CTX_LEN = 256
N_HEADS = 8
QK_NOPE_DIM = 64
QK_ROPE_DIM = 32
V_HEAD_DIM = 64
Q_LORA_RANK = 384
KV_LORA_RANK = 256
ROPE_THETA = 10000.0
CONV_DIM = 512
D_FF = 2816
EPS = 1e-6
ATTN_DIM = N_HEADS * V_HEAD_DIM

LANES = 128
HALF = LANES // 2
HEAD_COLS = N_HEADS * LANES
ROPE_HALF = QK_ROPE_DIM // 4
HALO = 8
HALO_BF16 = 16
CAST_ROWS = 16
DEN_ROWS = 16

C_KV = 0
C_KR = C_KV + KV_LORA_RANK
C_Q = C_KR + LANES
C_LAT = C_Q + Q_LORA_RANK
O_CX = KV_LORA_RANK + QK_ROPE_DIM + Q_LORA_RANK
O_CB = O_CX + CONV_DIM
O_CC = O_CB + CONV_DIM
O_GA = O_CC + CONV_DIM
O_GC = O_GA + D_MODEL
O_END = O_GC + D_MODEL

LOG2E = 1.4426950408889634

VMEM_LIMIT = 56 * 1024 * 1024

TM_IN = 1024
TM = 512
TQ = 1024
FF_CHUNK = 256

BF16 = jnp.bfloat16
F32 = jnp.float32


def _dot(a, b):
    return jnp.dot(a, b, preferred_element_type=F32)


def _dot_nt(a, b):
    return lax.dot_general(a, b, (((1,), (1,)), ((), ())), preferred_element_type=F32)


def _rms(x, g):
    return x * lax.rsqrt(jnp.mean(x * x, axis=-1, keepdims=True) + EPS) * g


def _sigmoid(x):
    return 1.0 / (1.0 + jnp.exp(-x))


def _rope_block(xh, c, s0, s1):
    return (xh * c + pltpu.roll(xh, LANES - ROPE_HALF, 1) * s0
            + pltpu.roll(xh, ROPE_HALF, 1) * s1)


def _conv3(u_all, m, first, last, w_ref, b_ref, sl):
    u = u_all[:m]
    ext = jnp.concatenate([jnp.where(first, 0.0, u_all[m:m + HALO]), u,
                           jnp.where(last, 0.0, u_all[m + HALO:])], axis=0)
    u_dn = pltpu.roll(ext, 1, 0)[HALO:HALO + m]
    u_up = pltpu.roll(ext, m + 2 * HALO - 1, 0)[HALO:HALO + m]
    return b_ref[:, sl] + u_dn * w_ref[0:1, sl] + u * w_ref[1:2, sl] + u_up * w_ref[2:3, sl]


def _const_spec(shape):
    nd = len(shape)
    return pl.BlockSpec(shape, lambda *_: (0,) * nd, pipeline_mode=pl.Buffered(1))


def _params(n_grid):
    return pltpu.CompilerParams(dimension_semantics=("arbitrary",) * n_grid,
                                vmem_limit_bytes=VMEM_LIMIT)


def _ada_kernel(c_ref, w_ref, b_ref, o_ref):
    c = c_ref[...]
    s = (c * _sigmoid(c)).astype(BF16)
    o_ref[...] = _dot(s, w_ref[...].astype(BF16)) + b_ref[...]


def _ada(cvec, w_ada, b_ada):
    rows = cvec.shape[0]
    n = w_ada.shape[1]
    tn = 1024
    return pl.pallas_call(
        _ada_kernel,
        name="ada",
        grid=(n // tn,),
        in_specs=[pl.BlockSpec((rows, D_MODEL), lambda j: (0, 0)),
                  pl.BlockSpec((D_MODEL, tn), lambda j: (0, j)),
                  pl.BlockSpec((1, tn), lambda j: (0, j))],
        out_specs=pl.BlockSpec((rows, tn), lambda j: (0, j)),
        out_shape=jax.ShapeDtypeStruct((rows, n), F32),
        compiler_params=_params(1),
    )(cvec, w_ada, b_ada)


def _kv_from(pa, kr_even, kvg_ref, wkn_ref, wvt_ref, k_ref, vt_ref):
    ckv = _rms(pa[:, C_KV:C_KV + KV_LORA_RANK], kvg_ref[...]).astype(BF16)
    kn = _dot(ckv, wkn_ref[...])
    kr_odd = pltpu.roll(kr_even, HALF, 1)
    lo = lax.broadcasted_iota(jnp.int32, kr_even.shape, 1) < HALF
    for p in range(N_HEADS // 2):
        kn_pair = kn[:, p * LANES:(p + 1) * LANES]
        k_ref[:, (2 * p) * LANES:(2 * p + 1) * LANES] = jnp.where(lo, kn_pair, kr_even).astype(BF16)
        k_ref[:, (2 * p + 1) * LANES:(2 * p + 2) * LANES] = jnp.where(lo, kr_odd, kn_pair).astype(BF16)
    vt = _dot_nt(wvt_ref[...], ckv).astype(BF16)
    per_seq = vt_ref.shape[1]
    for b in range(vt.shape[1] // per_seq):
        vt_ref[b * ATTN_DIM:(b + 1) * ATTN_DIM, :] = vt[:, b * per_seq:(b + 1) * per_seq]


def _inproj_kernel(x_ref, xp_ref, xn_ref, mod_ref, tab_ref, g1_ref, wlat_ref, wcx_ref, wcb_ref,
                   wcc_ref, wga_ref, wgc_ref, kvg_ref, qg_ref, wkn_ref, wvt_ref, wq_ref, cw_ref,
                   cb_ref, wco_ref, q_ref, k_ref, vt_ref, gg_ref):
    s = pl.program_id(0)
    first = s == 0
    last = s == pl.num_programs(0) - 1
    sh = mod_ref[0, 0:1, :]
    sc = mod_ref[0, 1:2, :]
    g1 = g1_ref[...]
    xcat = jnp.concatenate([x_ref[...], xp_ref[...], xn_ref[...]], axis=0)
    hcat = (_rms(xcat, g1) * (1.0 + sc) + sh).astype(BF16)
    tm = x_ref.shape[0]
    hb = hcat[:tm]

    pa = _dot(hb, wlat_ref[...])
    x_in = _dot(hcat, wcx_ref[...])
    b_gate = _dot(hb, wcb_ref[...])
    c_gate = _dot(hcat, wcc_ref[...])

    tabs = [[tab_ref[0], tab_ref[1], tab_ref[2]]]
    tabs.append([pltpu.roll(t, HALF, 1) for t in tabs[0]])
    kr = _rope_block(pa[:, C_KR:C_KR + LANES], *tabs[0])
    _kv_from(pa, kr, kvg_ref, wkn_ref, wvt_ref, k_ref, vt_ref)
    cq = _rms(pa[:, C_Q:C_Q + Q_LORA_RANK], qg_ref[...]).astype(BF16)
    qf = _dot(cq, wq_ref[...])
    for h in range(N_HEADS):
        sl = slice(h * LANES, (h + 1) * LANES)
        q_ref[sl, :] = _rope_block(qf[:, sl], *tabs[h % 2]).T.astype(BF16)

    gg_ref[:, :D_MODEL] = _sigmoid(_dot(hb, wga_ref[...])).astype(BF16)

    conv = _conv3(c_gate * x_in, tm, first, last, cw_ref, cb_ref, slice(0, CONV_DIM))
    y_conv = _dot((b_gate * conv).astype(BF16), wco_ref[...])
    gg_ref[:, D_MODEL:] = (_sigmoid(_dot(hb, wgc_ref[...])) * y_conv).astype(BF16)


def _halo_specs(tm, cols, rows=HALO):
    tps = SEQ // tm
    r = tm // rows
    nblk = BATCH * SEQ // rows
    prev = pl.BlockSpec((rows, cols), lambda s, b: (jnp.maximum((b * tps + s) * r - 1, 0), 0))
    nxt = pl.BlockSpec((rows, cols), lambda s, b: (jnp.minimum((b * tps + s + 1) * r, nblk - 1), 0))
    return prev, nxt


def _inproj(x2, mod_lat, tab, g1, w_lat, w_cx, w_cb, w_cc, w_ga, w_gc, kvg, qg, w_kn, w_vt, w_q_p,
            cw, cb, wco):
    n = x2.shape[0]
    tm = TM_IN
    tps = SEQ // tm
    row = lambda s, b: (b * tps + s, 0)
    xprev, xnext = _halo_specs(tm, D_MODEL)
    return pl.pallas_call(
        _inproj_kernel,
        name="inproj",
        grid=(tps, BATCH),
        in_specs=[
            pl.BlockSpec((tm, D_MODEL), row),
            xprev,
            xnext,
            pl.BlockSpec((1, 6, D_MODEL), lambda s, b: (b, 0, 0)),
            pl.BlockSpec((3, tm, LANES), lambda s, b: (0, s, 0)),
            _const_spec((1, D_MODEL)),
            _const_spec((D_MODEL, C_LAT)),
            _const_spec((D_MODEL, CONV_DIM)),
            _const_spec((D_MODEL, CONV_DIM)),
            _const_spec((D_MODEL, CONV_DIM)),
            _const_spec((D_MODEL, D_MODEL)),
            _const_spec((D_MODEL, D_MODEL)),
            _const_spec((1, KV_LORA_RANK)),
            _const_spec((1, Q_LORA_RANK)),
            _const_spec((KV_LORA_RANK, N_HEADS * QK_NOPE_DIM)),
            _const_spec((ATTN_DIM, KV_LORA_RANK)),
            _const_spec((Q_LORA_RANK, HEAD_COLS)),
            _const_spec((3, CONV_DIM)),
            _const_spec((1, CONV_DIM)),
            _const_spec((CONV_DIM, D_MODEL)),
        ],
        out_specs=[pl.BlockSpec((HEAD_COLS, tm), lambda s, b: (0, b * tps + s)),
                   pl.BlockSpec((tm, HEAD_COLS), row),
                   pl.BlockSpec((ATTN_DIM, tm), lambda s, b: (b, s)),
                   pl.BlockSpec((tm, 2 * D_MODEL), row)],
        out_shape=[jax.ShapeDtypeStruct((HEAD_COLS, n), BF16), jax.ShapeDtypeStruct((n, HEAD_COLS), BF16),
                   jax.ShapeDtypeStruct((BATCH * ATTN_DIM, SEQ), BF16),
                   jax.ShapeDtypeStruct((n, 2 * D_MODEL), BF16)],
        compiler_params=_params(2),
    )(x2, x2, x2, mod_lat, tab, g1, w_lat, w_cx, w_cb, w_cc, w_ga, w_gc, kvg, qg, w_kn, w_vt, w_q_p,
      cw, cb, wco)


def _ctxproj_kernel(x_ref, mod_ref, g1_ref, win_ref, kvg_ref, wkn_ref, wvt_ref, k_ref, vt_ref):
    sh = mod_ref[0, 0:1, :]
    sc = mod_ref[0, 1:2, :]
    hb = (_rms(x_ref[...], g1_ref[...]) * (1.0 + sc) + sh).astype(BF16)
    pa = _dot(hb, win_ref[...])
    _kv_from(pa, pa[:, C_KR:C_KR + LANES], kvg_ref, wkn_ref, wvt_ref, k_ref, vt_ref)


def _ctxproj(c2, mod_ctx, g1, w_lat, kvg, w_kn, w_vt):
    n = c2.shape[0]
    seqs = 4
    tm = seqs * CTX_LEN
    row = lambda i: (i, 0)
    return pl.pallas_call(
        _ctxproj_kernel,
        name="ctxproj",
        grid=(n // tm,),
        in_specs=[
            pl.BlockSpec((tm, D_MODEL), row),
            _const_spec((1, 6, D_MODEL)),
            _const_spec((1, D_MODEL)),
            _const_spec((D_MODEL, C_Q)),
            _const_spec((1, KV_LORA_RANK)),
            _const_spec((KV_LORA_RANK, N_HEADS * QK_NOPE_DIM)),
            _const_spec((ATTN_DIM, KV_LORA_RANK)),
        ],
        out_specs=[pl.BlockSpec((tm, HEAD_COLS), row),
                   pl.BlockSpec((seqs * ATTN_DIM, CTX_LEN), row)],
        out_shape=[jax.ShapeDtypeStruct((n, HEAD_COLS), BF16),
                   jax.ShapeDtypeStruct((BATCH * ATTN_DIM, CTX_LEN), BF16)],
        compiler_params=_params(1),
    )(c2, mod_ctx, g1, w_lat, kvg, w_kn, w_vt)


def _attn_kernel(qt_ref, kl_ref, vlt_ref, kc_ref, vct_ref, *rest):
    n_cast = len(rest) // 2
    o_ref = rest[n_cast]

    def scores(h):
        sl = slice(h * LANES, (h + 1) * LANES)
        qt = qt_ref[sl, :]
        s_c, s_l = _dot(kc_ref[:, sl], qt), _dot(kl_ref[:, sl], qt)
        pm = jnp.maximum(jnp.max(s_c.reshape(CTX_LEN // HALO, HALO, TQ), axis=0),
                         jnp.max(s_l.reshape(SEQ // HALO, HALO, TQ), axis=0))
        return s_c, s_l, jnp.max(pm, axis=0, keepdims=True)

    nxt = scores(0)
    halves = []
    for h in range(N_HEADS):
        s_c, s_l, m = nxt
        if h + 1 < N_HEADS:
            nxt = scores(h + 1)
        vsl = slice((h // 2) * LANES, (h // 2 + 1) * LANES)
        vct = jnp.concatenate([vct_ref[vsl, :], jnp.ones((DEN_ROWS, CTX_LEN), BF16)], axis=0)
        vlt = jnp.concatenate([vlt_ref[vsl, :], jnp.ones((DEN_ROWS, SEQ), BF16)], axis=0)
        acc = (_dot(vct, jnp.exp2(s_c - m).astype(BF16))
               + _dot(vlt, jnp.exp2(s_l - m).astype(BF16)))
        rows = slice((h % 2) * V_HEAD_DIM, (h % 2 + 1) * V_HEAD_DIM)
        halves.append(acc[rows] * (1.0 / acc[LANES:LANES + 1]))
        if h % 2 == 1:
            o_ref[:, vsl] = jnp.concatenate(halves[-2:], axis=0).T.astype(BF16)

    for src, dst in zip(rest[:n_cast], rest[n_cast + 1:]):
        dst[...] = src[...].astype(BF16)


def _attn(q, kl, vlt, kc, vct, cast_weights):
    tq = TQ
    nq = SEQ // tq
    steps = BATCH * nq
    cast_specs = []
    for w in cast_weights:
        rows, cols = w.shape
        blk = next(r for r in range(CAST_ROWS, rows + 1, CAST_ROWS)
                   if rows % r == 0 and r * steps >= rows)
        nblk = rows // blk
        cast_specs.append(pl.BlockSpec((blk, cols), functools.partial(
            lambda b, j, nblk: (jnp.minimum(b * nq + j, nblk - 1), 0), nblk=nblk)))
    outs = pl.pallas_call(
        _attn_kernel,
        name="attn",
        grid=(BATCH, nq),
        in_specs=[
            pl.BlockSpec((HEAD_COLS, tq), lambda b, j: (0, b * nq + j)),
            pl.BlockSpec((SEQ, HEAD_COLS), lambda b, j: (b, 0)),
            pl.BlockSpec((ATTN_DIM, SEQ), lambda b, j: (b, 0)),
            pl.BlockSpec((CTX_LEN, HEAD_COLS), lambda b, j: (b, 0)),
            pl.BlockSpec((ATTN_DIM, CTX_LEN), lambda b, j: (b, 0)),
        ] + cast_specs,
        out_specs=[pl.BlockSpec((tq, ATTN_DIM), lambda b, j: (b * nq + j, 0))] + cast_specs,
        out_shape=[jax.ShapeDtypeStruct((BATCH * SEQ, ATTN_DIM), BF16)]
                  + [jax.ShapeDtypeStruct(w.shape, BF16) for w in cast_weights],
        compiler_params=_params(2),
    )(q, kl, vlt, kc, vct, *cast_weights)
    return outs[0], outs[1:]


def _halo_rows_bf16(prev_ref, next_ref):
    p = prev_ref[...].astype(F32)[HALO_BF16 - HALO:]
    n = next_ref[...].astype(F32)[:HALO]
    return jnp.concatenate([p, n], axis=0).astype(BF16)


def _post_kernel(o_ref, op_ref, on_ref, gg_ref, gp_ref, gn_ref, x_ref, xp_ref, xn_ref, mod_ref,
                 wao_ref, wo_ref, g2_ref, wup_ref, fcw_ref, fcb_ref, wdn_ref, gf_ref,
                 out_ref, act_ref):
    s = pl.program_id(0)
    first = s == 0
    last = s == pl.num_programs(0) - 1
    hm = TM // 2
    os_ = (o_ref[:hm], jnp.concatenate([o_ref[hm:], _halo_rows_bf16(op_ref, on_ref)], axis=0))
    ggs = (gg_ref[:hm], jnp.concatenate([gg_ref[hm:], _halo_rows_bf16(gp_ref, gn_ref)], axis=0))
    xs = (x_ref[:hm], jnp.concatenate([x_ref[hm:], xp_ref[...], xn_ref[...]], axis=0))
    y_attn = [_dot(o, wao_ref[...]) for o in os_]
    ms = [(gg[:, :D_MODEL].astype(F32) * y + gg[:, D_MODEL:].astype(F32)).astype(BF16)
          for gg, y in zip(ggs, y_attn)]
    x1s = [x + mod_ref[0, 2:3, :] * _dot(m, wo_ref[...]) for x, m in zip(xs, ms)]
    h2 = jnp.concatenate(
        [(_rms(x1, g2_ref[...]) * (1.0 + mod_ref[0, 4:5, :]) + mod_ref[0, 3:4, :]).astype(BF16)
         for x1 in x1s], axis=0)

    for c0 in range(0, D_FF, FF_CHUNK):
        c1 = min(c0 + FF_CHUNK, D_FF)
        gsl = slice(c0, c1)
        vsl = slice(D_FF + c0, D_FF + c1)
        gate = _conv3(_dot(h2, wup_ref[:, gsl]), TM, first, last, fcw_ref, fcb_ref, gsl)
        val = _conv3(_dot(h2, wup_ref[:, vsl]), TM, first, last, fcw_ref, fcb_ref, vsl)
        act_ref[:, gsl] = (gate * _sigmoid(gate) * val).astype(BF16)

    ys = [_dot(act_ref[r * hm:(r + 1) * hm, :], wdn_ref[...]) for r in range(2)]
    for r in range(2):
        x2 = x1s[r][:hm] + mod_ref[0, 5:6, :] * ys[r]
        out_ref[r * hm:(r + 1) * hm, :] = _rms(x2, gf_ref[...])


def _post(o, gg, x2, mod_lat, wao, wo, g2, wup, fcw, fcb, wdn, gf):
    n = x2.shape[0]
    tps = SEQ // TM
    row = lambda s, b: (b * tps + s, 0)
    oprev, onext = _halo_specs(TM, ATTN_DIM, HALO_BF16)
    gprev, gnext = _halo_specs(TM, 2 * D_MODEL, HALO_BF16)
    xprev, xnext = _halo_specs(TM, D_MODEL)
    return pl.pallas_call(
        _post_kernel,
        name="post",
        grid=(tps, BATCH),
        in_specs=[
            pl.BlockSpec((TM, ATTN_DIM), row), oprev, onext,
            pl.BlockSpec((TM, 2 * D_MODEL), row), gprev, gnext,
            pl.BlockSpec((TM, D_MODEL), row), xprev, xnext,
            pl.BlockSpec((1, 6, D_MODEL), lambda s, b: (b, 0, 0)),
            _const_spec((ATTN_DIM, D_MODEL)),
            _const_spec((D_MODEL, D_MODEL)),
            _const_spec((1, D_MODEL)),
            _const_spec((D_MODEL, 2 * D_FF)),
            _const_spec((3, 2 * D_FF)),
            _const_spec((1, 2 * D_FF)),
            _const_spec((D_FF, D_MODEL)),
            _const_spec((1, D_MODEL)),
        ],
        out_specs=pl.BlockSpec((TM, D_MODEL), row),
        out_shape=jax.ShapeDtypeStruct((n, D_MODEL), F32),
        scratch_shapes=[pltpu.VMEM((TM, D_FF), BF16)],
        compiler_params=_params(2),
    )(o, o, o, gg, gg, gg, x2, x2, x2, mod_lat, wao, wo, g2, wup, fcw, fcb, wdn, gf)


def _rope_tables():
    rows = SEQ // GRID_W
    row = jnp.repeat(jnp.arange(rows), GRID_W).astype(F32)
    col = jnp.tile(jnp.arange(GRID_W), rows).astype(F32)
    axis_dim = QK_ROPE_DIM // 2
    inv = ROPE_THETA ** (-jnp.arange(0, axis_dim, 2, dtype=F32) / axis_dim)
    ang = jnp.concatenate([row[:, None] * inv, col[:, None] * inv], axis=-1)
    j = jnp.arange(QK_ROPE_DIM)
    idx = (j // axis_dim) * ROPE_HALF + j % ROPE_HALF
    first_half = ((j % axis_dim) // ROPE_HALF) == 0
    cos = jnp.cos(ang)[:, idx]
    sin = jnp.sin(ang)[:, idx]
    s0 = jnp.where(first_half, -sin, 0.0)
    s1 = jnp.where(first_half, 0.0, sin)
    pad = jnp.zeros((SEQ, LANES - QK_NOPE_DIM - QK_ROPE_DIM), F32)

    def even(nope_val, rope_part):
        return jnp.concatenate([jnp.full((SEQ, QK_NOPE_DIM), nope_val, F32), rope_part, pad], axis=1)

    return jnp.stack([even(1.0, cos), even(0.0, s0), even(0.0, s1)])


def _split_w_in_kernel(wt_ref, lat_ref, cx_ref, cb_ref, cc_ref, ga_ref, gc_ref):
    kr_end = KV_LORA_RANK + QK_ROPE_DIM
    krq = wt_ref[KV_LORA_RANK:KV_LORA_RANK + LANES, :].T
    lane = lax.broadcasted_iota(jnp.int32, krq.shape, 1)
    kr = jnp.where((lane >= QK_NOPE_DIM) & (lane < QK_NOPE_DIM + QK_ROPE_DIM),
                   pltpu.roll(krq, QK_NOPE_DIM, 1), 0.0)
    lat_ref[...] = jnp.concatenate([wt_ref[:KV_LORA_RANK, :].T, kr, wt_ref[kr_end:O_CX, :].T],
                                   axis=1).astype(BF16)
    for ref, (a, b) in ((cx_ref, (O_CX, O_CB)), (cb_ref, (O_CB, O_CC)), (cc_ref, (O_CC, O_GA)),
                        (ga_ref, (O_GA, O_GC)), (gc_ref, (O_GC, O_END))):
        ref[...] = wt_ref[a:b, :].T.astype(BF16)


def _split_w_in(wt):
    widths = (C_LAT, CONV_DIM, CONV_DIM, CONV_DIM, D_MODEL, D_MODEL)
    return pl.pallas_call(
        _split_w_in_kernel,
        name="split_w_in",
        grid=(D_MODEL // LANES,),
        in_specs=[pl.BlockSpec((O_END, LANES), lambda i: (0, i))],
        out_specs=[pl.BlockSpec((LANES, n), lambda i: (i, 0)) for n in widths],
        out_shape=[jax.ShapeDtypeStruct((D_MODEL, n), BF16) for n in widths],
        compiler_params=_params(1),
    )(wt)


def _pack_w_ukv(w):
    w = w.astype(BF16).reshape(KV_LORA_RANK, N_HEADS, QK_NOPE_DIM + V_HEAD_DIM)
    return (w[..., :QK_NOPE_DIM].reshape(KV_LORA_RANK, N_HEADS * QK_NOPE_DIM),
            w[..., QK_NOPE_DIM:].reshape(KV_LORA_RANK, ATTN_DIM).T)


def _pack_w_uq(w):
    qscale = (QK_NOPE_DIM + QK_ROPE_DIM) ** -0.5 * LOG2E
    w = (w * qscale).astype(BF16).reshape(Q_LORA_RANK, N_HEADS // 2, 2, QK_NOPE_DIM + QK_ROPE_DIM)
    pad = jnp.zeros((Q_LORA_RANK, N_HEADS // 2, LANES - QK_NOPE_DIM - QK_ROPE_DIM), BF16)
    ev, od = w[:, :, 0], w[:, :, 1]
    ev = jnp.concatenate([ev, pad], axis=-1)
    od = jnp.concatenate([od[..., QK_NOPE_DIM:], pad, od[..., :QK_NOPE_DIM]], axis=-1)
    return jnp.stack([ev, od], axis=2).reshape(Q_LORA_RANK, HEAD_COLS)


def kernel(x, c, ctx, c_ctx, w_ada, b_ada, norm1_g, w_in, q_norm_g, kv_norm_g, w_uq, w_ukv,
           conv_w, conv_b, w_attn_out, w_conv_out, w_o, norm2_g, w_up, ffn_conv_w, ffn_conv_b,
           w_down, final_g):
    assert x.shape == (BATCH, SEQ, D_MODEL) and ctx.shape == (BATCH, CTX_LEN, D_MODEL)
    assert w_ada.shape[0] == 1, "single-layer block"

    cvec = jnp.concatenate([c, c_ctx[None, :], jnp.zeros((16 - BATCH - 1, D_MODEL), F32)], axis=0)
    mod = _ada(cvec, w_ada.reshape(D_MODEL, 6 * D_MODEL), b_ada.reshape(1, 6 * D_MODEL))
    mod_lat = mod[:BATCH].reshape(BATCH, 6, D_MODEL)
    mod_ctx = mod[BATCH:BATCH + 1].reshape(1, 6, D_MODEL)

    tab = _rope_tables()
    w_lat, w_cx, w_cb, w_cc, w_ga, w_gc = _split_w_in(w_in.reshape(D_MODEL, O_END).T)
    w_kn, w_vt = _pack_w_ukv(w_ukv.reshape(KV_LORA_RANK, -1))
    w_q_p = _pack_w_uq(w_uq.reshape(Q_LORA_RANK, -1))
    g1 = norm1_g.reshape(1, D_MODEL)
    kvg = kv_norm_g.reshape(1, KV_LORA_RANK)
    qg = q_norm_g.reshape(1, Q_LORA_RANK)

    x2 = x.reshape(BATCH * SEQ, D_MODEL)
    c2 = ctx.reshape(BATCH * CTX_LEN, D_MODEL)

    kc, vct = _ctxproj(c2, mod_ctx, g1, w_lat, kvg, w_kn, w_vt)
    q, kl, vlt, gg = _inproj(x2, mod_lat, tab, g1, w_lat, w_cx, w_cb, w_cc, w_ga, w_gc, kvg, qg,
                             w_kn, w_vt, w_q_p,
                            conv_w.reshape(3, CONV_DIM), conv_b.reshape(1, CONV_DIM),
                            w_conv_out.reshape(CONV_DIM, D_MODEL).astype(BF16))
    o, (wao, wo, wup, wdn) = _attn(q, kl, vlt, kc, vct, (
        w_attn_out.reshape(ATTN_DIM, D_MODEL), w_o.reshape(D_MODEL, D_MODEL),
        w_up.reshape(D_MODEL, 2 * D_FF), w_down.reshape(D_FF, D_MODEL)))
    out = _post(o, gg, x2, mod_lat, wao, wo, norm2_g.reshape(1, D_MODEL), wup,
                ffn_conv_w.reshape(3, 2 * D_FF), ffn_conv_b.reshape(1, 2 * D_FF), wdn,
                final_g.reshape(1, D_MODEL))
    return out.reshape(BATCH, SEQ, D_MODEL)
```

```python
import functools

import jax
import jax.numpy as jnp
import numpy as np
from jax import lax
from jax.experimental import pallas as pl
from jax.experimental.pallas import tpu as pltpu

D_MODEL = 1024
BATCH = 8
SEQ = 2048
GRID_W = 64
CTX_LEN = 256
N_HEADS = 8
QK_NOPE_DIM = 64
QK_ROPE_DIM = 32
V_HEAD_DIM = 64
Q_LORA_RANK = 384
KV_LORA_RANK = 256
ROPE_THETA = 10000.0
CONV_DIM = 512
D_FF = 2816
EPS = 1e-6
ATTN_DIM = N_HEADS * V_HEAD_DIM

LANES = 128
HALF = LANES // 2
HEAD_COLS = N_HEADS * LANES
ROPE_HALF = QK_ROPE_DIM // 4
HALO = 8
HALO_BF16 = 16
CAST_ROWS = 16
DEN_ROWS = 16

C_KV = 0
C_KR = C_KV + KV_LORA_RANK
C_Q = C_KR + LANES
C_LAT = C_Q + Q_LORA_RANK
O_CX = KV_LORA_RANK + QK_ROPE_DIM + Q_LORA_RANK
O_CB = O_CX + CONV_DIM
O_CC = O_CB + CONV_DIM
O_GA = O_CC + CONV_DIM
O_GC = O_GA + D_MODEL
O_END = O_GC + D_MODEL

LOG2E = 1.4426950408889634

VMEM_LIMIT = 56 * 1024 * 1024

TM_IN = 1024
TM = 512
TQ = 1024
FF_CHUNK = 256

BF16 = jnp.bfloat16
F32 = jnp.float32


def _dot(a, b):
    return jnp.dot(a, b, preferred_element_type=F32)


def _dot_nt(a, b):
    return lax.dot_general(a, b, (((1,), (1,)), ((), ())), preferred_element_type=F32)


def _rms(x, g):
    return x * lax.rsqrt(jnp.mean(x * x, axis=-1, keepdims=True) + EPS) * g


def _sigmoid(x):
    return 1.0 / (1.0 + jnp.exp(-x))


def _rope_block(xh, c, s0, s1):
    return (xh * c + pltpu.roll(xh, LANES - ROPE_HALF, 1) * s0
            + pltpu.roll(xh, ROPE_HALF, 1) * s1)


def _conv3(u_all, m, first, last, w_ref, b_ref, sl):
    u = u_all[:m]
    ext = jnp.concatenate([jnp.where(first, 0.0, u_all[m:m + HALO]), u,
                           jnp.where(last, 0.0, u_all[m + HALO:])], axis=0)
    u_dn = pltpu.roll(ext, 1, 0)[HALO:HALO + m]
    u_up = pltpu.roll(ext, m + 2 * HALO - 1, 0)[HALO:HALO + m]
    return b_ref[:, sl] + u_dn * w_ref[0:1, sl] + u * w_ref[1:2, sl] + u_up * w_ref[2:3, sl]


def _const_spec(shape):
    nd = len(shape)
    return pl.BlockSpec(shape, lambda *_: (0,) * nd, pipeline_mode=pl.Buffered(1))


def _params(n_grid):
    return pltpu.CompilerParams(dimension_semantics=("arbitrary",) * n_grid,
                                vmem_limit_bytes=VMEM_LIMIT)


def _ada_kernel(c_ref, w_ref, b_ref, o_ref):
    c = c_ref[...]
    s = (c * _sigmoid(c)).astype(BF16)
    o_ref[...] = _dot(s, w_ref[...].astype(BF16)) + b_ref[...]


def _ada(cvec, w_ada, b_ada):
    rows = cvec.shape[0]
    n = w_ada.shape[1]
    tn = 1024
    return pl.pallas_call(
        _ada_kernel,
        name="ada",
        grid=(n // tn,),
        in_specs=[pl.BlockSpec((rows, D_MODEL), lambda j: (0, 0)),
                  pl.BlockSpec((D_MODEL, tn), lambda j: (0, j)),
                  pl.BlockSpec((1, tn), lambda j: (0, j))],
        out_specs=pl.BlockSpec((rows, tn), lambda j: (0, j)),
        out_shape=jax.ShapeDtypeStruct((rows, n), F32),
        compiler_params=_params(1),
    )(cvec, w_ada, b_ada)


def _kv_from(pa, kr_even, kvg_ref, wkn_ref, wvt_ref, k_ref, vt_ref):
    ckv = _rms(pa[:, C_KV:C_KV + KV_LORA_RANK], kvg_ref[...]).astype(BF16)
    kn = _dot(ckv, wkn_ref[...])
    kr_odd = pltpu.roll(kr_even, HALF, 1)
    lo = lax.broadcasted_iota(jnp.int32, kr_even.shape, 1) < HALF
    for p in range(N_HEADS // 2):
        kn_pair = kn[:, p * LANES:(p + 1) * LANES]
        k_ref[:, (2 * p) * LANES:(2 * p + 1) * LANES] = jnp.where(lo, kn_pair, kr_even).astype(BF16)
        k_ref[:, (2 * p + 1) * LANES:(2 * p + 2) * LANES] = jnp.where(lo, kr_odd, kn_pair).astype(BF16)
    vt = _dot_nt(wvt_ref[...], ckv).astype(BF16)
    per_seq = vt_ref.shape[1]
    for b in range(vt.shape[1] // per_seq):
        vt_ref[b * ATTN_DIM:(b + 1) * ATTN_DIM, :] = vt[:, b * per_seq:(b + 1) * per_seq]


def _inproj_kernel(x_ref, xp_ref, xn_ref, mod_ref, tab_ref, g1_ref, wlat_ref, wcx_ref, wcb_ref,
                   wcc_ref, wga_ref, wgc_ref, kvg_ref, qg_ref, wkn_ref, wvt_ref, wq_ref, cw_ref,
                   cb_ref, wco_ref, q_ref, k_ref, vt_ref, gg_ref):
    s = pl.program_id(0)
    first = s == 0
    last = s == pl.num_programs(0) - 1
    sh = mod_ref[0, 0:1, :]
    sc = mod_ref[0, 1:2, :]
    g1 = g1_ref[...]
    xcat = jnp.concatenate([x_ref[...], xp_ref[...], xn_ref[...]], axis=0)
    hcat = (_rms(xcat, g1) * (1.0 + sc) + sh).astype(BF16)
    tm = x_ref.shape[0]
    hb = hcat[:tm]

    pa = _dot(hb, wlat_ref[...])
    x_in = _dot(hcat, wcx_ref[...])
    b_gate = _dot(hb, wcb_ref[...])
    c_gate = _dot(hcat, wcc_ref[...])

    tabs = [[tab_ref[0], tab_ref[1], tab_ref[2]]]
    tabs.append([pltpu.roll(t, HALF, 1) for t in tabs[0]])
    kr = _rope_block(pa[:, C_KR:C_KR + LANES], *tabs[0])
    _kv_from(pa, kr, kvg_ref, wkn_ref, wvt_ref, k_ref, vt_ref)
    cq = _rms(pa[:, C_Q:C_Q + Q_LORA_RANK], qg_ref[...]).astype(BF16)
    qf = _dot(cq, wq_ref[...])
    for h in range(N_HEADS):
        sl = slice(h * LANES, (h + 1) * LANES)
        q_ref[sl, :] = _rope_block(qf[:, sl], *tabs[h % 2]).T.astype(BF16)

    gg_ref[:, :D_MODEL] = _sigmoid(_dot(hb, wga_ref[...])).astype(BF16)

    conv = _conv3(c_gate * x_in, tm, first, last, cw_ref, cb_ref, slice(0, CONV_DIM))
    y_conv = _dot((b_gate * conv).astype(BF16), wco_ref[...])
    gg_ref[:, D_MODEL:] = (_sigmoid(_dot(hb, wgc_ref[...])) * y_conv).astype(BF16)


def _halo_specs(tm, cols, rows=HALO):
    tps = SEQ // tm
    r = tm // rows
    nblk = BATCH * SEQ // rows
    prev = pl.BlockSpec((rows, cols), lambda s, b: (jnp.maximum((b * tps + s) * r - 1, 0), 0))
    nxt = pl.BlockSpec((rows, cols), lambda s, b: (jnp.minimum((b * tps + s + 1) * r, nblk - 1), 0))
    return prev, nxt


def _inproj(x2, mod_lat, tab, g1, w_lat, w_cx, w_cb, w_cc, w_ga, w_gc, kvg, qg, w_kn, w_vt, w_q_p,
            cw, cb, wco):
    n = x2.shape[0]
    tm = TM_IN
    tps = SEQ // tm
    row = lambda s, b: (b * tps + s, 0)
    xprev, xnext = _halo_specs(tm, D_MODEL)
    return pl.pallas_call(
        _inproj_kernel,
        name="inproj",
        grid=(tps, BATCH),
        in_specs=[
            pl.BlockSpec((tm, D_MODEL), row),
            xprev,
            xnext,
            pl.BlockSpec((1, 6, D_MODEL), lambda s, b: (b, 0, 0)),
            pl.BlockSpec((3, tm, LANES), lambda s, b: (0, s, 0)),
            _const_spec((1, D_MODEL)),
            _const_spec((D_MODEL, C_LAT)),
            _const_spec((D_MODEL, CONV_DIM)),
            _const_spec((D_MODEL, CONV_DIM)),
            _const_spec((D_MODEL, CONV_DIM)),
            _const_spec((D_MODEL, D_MODEL)),
            _const_spec((D_MODEL, D_MODEL)),
            _const_spec((1, KV_LORA_RANK)),
            _const_spec((1, Q_LORA_RANK)),
            _const_spec((KV_LORA_RANK, N_HEADS * QK_NOPE_DIM)),
            _const_spec((ATTN_DIM, KV_LORA_RANK)),
            _const_spec((Q_LORA_RANK, HEAD_COLS)),
            _const_spec((3, CONV_DIM)),
            _const_spec((1, CONV_DIM)),
            _const_spec((CONV_DIM, D_MODEL)),
        ],
        out_specs=[pl.BlockSpec((HEAD_COLS, tm), lambda s, b: (0, b * tps + s)),
                   pl.BlockSpec((tm, HEAD_COLS), row),
                   pl.BlockSpec((ATTN_DIM, tm), lambda s, b: (b, s)),
                   pl.BlockSpec((tm, 2 * D_MODEL), row)],
        out_shape=[jax.ShapeDtypeStruct((HEAD_COLS, n), BF16), jax.ShapeDtypeStruct((n, HEAD_COLS), BF16),
                   jax.ShapeDtypeStruct((BATCH * ATTN_DIM, SEQ), BF16),
                   jax.ShapeDtypeStruct((n, 2 * D_MODEL), BF16)],
        compiler_params=_params(2),
    )(x2, x2, x2, mod_lat, tab, g1, w_lat, w_cx, w_cb, w_cc, w_ga, w_gc, kvg, qg, w_kn, w_vt, w_q_p,
      cw, cb, wco)


def _ctxproj_kernel(x_ref, mod_ref, g1_ref, win_ref, kvg_ref, wkn_ref, wvt_ref, k_ref, vt_ref):
    sh = mod_ref[0, 0:1, :]
    sc = mod_ref[0, 1:2, :]
    hb = (_rms(x_ref[...], g1_ref[...]) * (1.0 + sc) + sh).astype(BF16)
    pa = _dot(hb, win_ref[...])
    _kv_from(pa, pa[:, C_KR:C_KR + LANES], kvg_ref, wkn_ref, wvt_ref, k_ref, vt_ref)


def _ctxproj(c2, mod_ctx, g1, w_lat, kvg, w_kn, w_vt):
    n = c2.shape[0]
    seqs = 4
    tm = seqs * CTX_LEN
    row = lambda i: (i, 0)
    return pl.pallas_call(
        _ctxproj_kernel,
        name="ctxproj",
        grid=(n // tm,),
        in_specs=[
            pl.BlockSpec((tm, D_MODEL), row),
            _const_spec((1, 6, D_MODEL)),
            _const_spec((1, D_MODEL)),
            _const_spec((D_MODEL, C_Q)),
            _const_spec((1, KV_LORA_RANK)),
            _const_spec((KV_LORA_RANK, N_HEADS * QK_NOPE_DIM)),
            _const_spec((ATTN_DIM, KV_LORA_RANK)),
        ],
        out_specs=[pl.BlockSpec((tm, HEAD_COLS), row),
                   pl.BlockSpec((seqs * ATTN_DIM, CTX_LEN), row)],
        out_shape=[jax.ShapeDtypeStruct((n, HEAD_COLS), BF16),
                   jax.ShapeDtypeStruct((BATCH * ATTN_DIM, CTX_LEN), BF16)],
        compiler_params=_params(1),
    )(c2, mod_ctx, g1, w_lat, kvg, w_kn, w_vt)


def _attn_kernel(qt_ref, kl_ref, vlt_ref, kc_ref, vct_ref, *rest):
    n_cast = len(rest) // 2
    o_ref = rest[n_cast]

    def scores(h):
        sl = slice(h * LANES, (h + 1) * LANES)
        qt = qt_ref[sl, :]
        s_c, s_l = _dot(kc_ref[:, sl], qt), _dot(kl_ref[:, sl], qt)
        pm = jnp.maximum(jnp.max(s_c.reshape(CTX_LEN // HALO, HALO, TQ), axis=0),
                         jnp.max(s_l.reshape(SEQ // HALO, HALO, TQ), axis=0))
        return s_c, s_l, jnp.max(pm, axis=0, keepdims=True)

    nxt = scores(0)
    halves = []
    for h in range(N_HEADS):
        s_c, s_l, m = nxt
        if h + 1 < N_HEADS:
            nxt = scores(h + 1)
        vsl = slice((h // 2) * LANES, (h // 2 + 1) * LANES)
        vct = jnp.concatenate([vct_ref[vsl, :], jnp.ones((DEN_ROWS, CTX_LEN), BF16)], axis=0)
        vlt = jnp.concatenate([vlt_ref[vsl, :], jnp.ones((DEN_ROWS, SEQ), BF16)], axis=0)
        acc = (_dot(vct, jnp.exp2(s_c - m).astype(BF16))
               + _dot(vlt, jnp.exp2(s_l - m).astype(BF16)))
        rows = slice((h % 2) * V_HEAD_DIM, (h % 2 + 1) * V_HEAD_DIM)
        halves.append(acc[rows] * (1.0 / acc[LANES:LANES + 1]))
        if h % 2 == 1:
            o_ref[:, vsl] = jnp.concatenate(halves[-2:], axis=0).T.astype(BF16)

    for src, dst in zip(rest[:n_cast], rest[n_cast + 1:]):
        dst[...] = src[...].astype(BF16)


def _attn(q, kl, vlt, kc, vct, cast_weights):
    tq = TQ
    nq = SEQ // tq
    steps = BATCH * nq
    cast_specs = []
    for w in cast_weights:
        rows, cols = w.shape
        blk = next(r for r in range(CAST_ROWS, rows + 1, CAST_ROWS)
                   if rows % r == 0 and r * steps >= rows)
        nblk = rows // blk
        cast_specs.append(pl.BlockSpec((blk, cols), functools.partial(
            lambda b, j, nblk: (jnp.minimum(b * nq + j, nblk - 1), 0), nblk=nblk)))
    outs = pl.pallas_call(
        _attn_kernel,
        name="attn",
        grid=(BATCH, nq),
        in_specs=[
            pl.BlockSpec((HEAD_COLS, tq), lambda b, j: (0, b * nq + j)),
            pl.BlockSpec((SEQ, HEAD_COLS), lambda b, j: (b, 0)),
            pl.BlockSpec((ATTN_DIM, SEQ), lambda b, j: (b, 0)),
            pl.BlockSpec((CTX_LEN, HEAD_COLS), lambda b, j: (b, 0)),
            pl.BlockSpec((ATTN_DIM, CTX_LEN), lambda b, j: (b, 0)),
        ] + cast_specs,
        out_specs=[pl.BlockSpec((tq, ATTN_DIM), lambda b, j: (b * nq + j, 0))] + cast_specs,
        out_shape=[jax.ShapeDtypeStruct((BATCH * SEQ, ATTN_DIM), BF16)]
                  + [jax.ShapeDtypeStruct(w.shape, BF16) for w in cast_weights],
        compiler_params=_params(2),
    )(q, kl, vlt, kc, vct, *cast_weights)
    return outs[0], outs[1:]


def _halo_rows_bf16(prev_ref, next_ref):
    p = prev_ref[...].astype(F32)[HALO_BF16 - HALO:]
    n = next_ref[...].astype(F32)[:HALO]
    return jnp.concatenate([p, n], axis=0).astype(BF16)


def _post_kernel(o_ref, op_ref, on_ref, gg_ref, gp_ref, gn_ref, x_ref, xp_ref, xn_ref, mod_ref,
                 wao_ref, wo_ref, g2_ref, wup_ref, fcw_ref, fcb_ref, wdn_ref, gf_ref,
                 out_ref, act_ref):
    s = pl.program_id(0)
    first = s == 0
    last = s == pl.num_programs(0) - 1
    hm = TM // 2
    os_ = (o_ref[:hm], jnp.concatenate([o_ref[hm:], _halo_rows_bf16(op_ref, on_ref)], axis=0))
    ggs = (gg_ref[:hm], jnp.concatenate([gg_ref[hm:], _halo_rows_bf16(gp_ref, gn_ref)], axis=0))
    xs = (x_ref[:hm], jnp.concatenate([x_ref[hm:], xp_ref[...], xn_ref[...]], axis=0))
    y_attn = [_dot(o, wao_ref[...]) for o in os_]
    ms = [(gg[:, :D_MODEL].astype(F32) * y + gg[:, D_MODEL:].astype(F32)).astype(BF16)
          for gg, y in zip(ggs, y_attn)]
    x1s = [x + mod_ref[0, 2:3, :] * _dot(m, wo_ref[...]) for x, m in zip(xs, ms)]
    h2 = jnp.concatenate(
        [(_rms(x1, g2_ref[...]) * (1.0 + mod_ref[0, 4:5, :]) + mod_ref[0, 3:4, :]).astype(BF16)
         for x1 in x1s], axis=0)

    for c0 in range(0, D_FF, FF_CHUNK):
        c1 = min(c0 + FF_CHUNK, D_FF)
        gsl = slice(c0, c1)
        vsl = slice(D_FF + c0, D_FF + c1)
        gate = _conv3(_dot(h2, wup_ref[:, gsl]), TM, first, last, fcw_ref, fcb_ref, gsl)
        val = _conv3(_dot(h2, wup_ref[:, vsl]), TM, first, last, fcw_ref, fcb_ref, vsl)
        act_ref[:, gsl] = (gate * _sigmoid(gate) * val).astype(BF16)

    ys = [_dot(act_ref[r * hm:(r + 1) * hm, :], wdn_ref[...]) for r in range(2)]
    for r in range(2):
        x2 = x1s[r][:hm] + mod_ref[0, 5:6, :] * ys[r]
        out_ref[r * hm:(r + 1) * hm, :] = _rms(x2, gf_ref[...])


def _post(o, gg, x2, mod_lat, wao, wo, g2, wup, fcw, fcb, wdn, gf):
    n = x2.shape[0]
    tps = SEQ // TM
    row = lambda s, b: (b * tps + s, 0)
    oprev, onext = _halo_specs(TM, ATTN_DIM, HALO_BF16)
    gprev, gnext = _halo_specs(TM, 2 * D_MODEL, HALO_BF16)
    xprev, xnext = _halo_specs(TM, D_MODEL)
    return pl.pallas_call(
        _post_kernel,
        name="post",
        grid=(tps, BATCH),
        in_specs=[
            pl.BlockSpec((TM, ATTN_DIM), row), oprev, onext,
            pl.BlockSpec((TM, 2 * D_MODEL), row), gprev, gnext,
            pl.BlockSpec((TM, D_MODEL), row), xprev, xnext,
            pl.BlockSpec((1, 6, D_MODEL), lambda s, b: (b, 0, 0)),
            _const_spec((ATTN_DIM, D_MODEL)),
            _const_spec((D_MODEL, D_MODEL)),
            _const_spec((1, D_MODEL)),
            _const_spec((D_MODEL, 2 * D_FF)),
            _const_spec((3, 2 * D_FF)),
            _const_spec((1, 2 * D_FF)),
            _const_spec((D_FF, D_MODEL)),
            _const_spec((1, D_MODEL)),
        ],
        out_specs=pl.BlockSpec((TM, D_MODEL), row),
        out_shape=jax.ShapeDtypeStruct((n, D_MODEL), F32),
        scratch_shapes=[pltpu.VMEM((TM, D_FF), BF16)],
        compiler_params=_params(2),
    )(o, o, o, gg, gg, gg, x2, x2, x2, mod_lat, wao, wo, g2, wup, fcw, fcb, wdn, gf)


def _rope_tables():
    f32 = np.float32
    rows = SEQ // GRID_W
    row = np.repeat(np.arange(rows), GRID_W).astype(f32)
    col = np.tile(np.arange(GRID_W), rows).astype(f32)
    axis_dim = QK_ROPE_DIM // 2
    inv = np.power(f32(ROPE_THETA), -np.arange(0, axis_dim, 2, dtype=f32) / f32(axis_dim)).astype(f32)
    ang = np.concatenate([row[:, None] * inv, col[:, None] * inv], axis=-1)
    j = np.arange(QK_ROPE_DIM)
    idx = (j // axis_dim) * ROPE_HALF + j % ROPE_HALF
    first_half = ((j % axis_dim) // ROPE_HALF) == 0
    cos = np.cos(ang).astype(f32)[:, idx]
    sin = np.sin(ang).astype(f32)[:, idx]
    s0 = np.where(first_half, -sin, f32(0))
    s1 = np.where(first_half, f32(0), sin)
    pad = np.zeros((SEQ, LANES - QK_NOPE_DIM - QK_ROPE_DIM), f32)

    def even(nope_val, rope_part):
        return np.concatenate([np.full((SEQ, QK_NOPE_DIM), nope_val, f32), rope_part, pad], axis=1)

    return jnp.asarray(np.stack([even(1.0, cos), even(0.0, s0), even(0.0, s1)]).astype(f32))


def _split_w_in_kernel(wt_ref, lat_ref, cx_ref, cb_ref, cc_ref, ga_ref, gc_ref):
    kr_end = KV_LORA_RANK + QK_ROPE_DIM
    krq = wt_ref[KV_LORA_RANK:KV_LORA_RANK + LANES, :].T
    lane = lax.broadcasted_iota(jnp.int32, krq.shape, 1)
    kr = jnp.where((lane >= QK_NOPE_DIM) & (lane < QK_NOPE_DIM + QK_ROPE_DIM),
                   pltpu.roll(krq, QK_NOPE_DIM, 1), 0.0)
    lat_ref[...] = jnp.concatenate([wt_ref[:KV_LORA_RANK, :].T, kr, wt_ref[kr_end:O_CX, :].T],
                                   axis=1).astype(BF16)
    for ref, (a, b) in ((cx_ref, (O_CX, O_CB)), (cb_ref, (O_CB, O_CC)), (cc_ref, (O_CC, O_GA)),
                        (ga_ref, (O_GA, O_GC)), (gc_ref, (O_GC, O_END))):
        ref[...] = wt_ref[a:b, :].T.astype(BF16)


def _split_w_in(wt):
    widths = (C_LAT, CONV_DIM, CONV_DIM, CONV_DIM, D_MODEL, D_MODEL)
    return pl.pallas_call(
        _split_w_in_kernel,
        name="split_w_in",
        grid=(D_MODEL // LANES,),
        in_specs=[pl.BlockSpec((O_END, LANES), lambda i: (0, i))],
        out_specs=[pl.BlockSpec((LANES, n), lambda i: (i, 0)) for n in widths],
        out_shape=[jax.ShapeDtypeStruct((D_MODEL, n), BF16) for n in widths],
        compiler_params=_params(1),
    )(wt)


def _pack_w_ukv(w):
    w = w.astype(BF16).reshape(KV_LORA_RANK, N_HEADS, QK_NOPE_DIM + V_HEAD_DIM)
    return (w[..., :QK_NOPE_DIM].reshape(KV_LORA_RANK, N_HEADS * QK_NOPE_DIM),
            w[..., QK_NOPE_DIM:].reshape(KV_LORA_RANK, ATTN_DIM).T)


def _pack_w_uq(w):
    qscale = (QK_NOPE_DIM + QK_ROPE_DIM) ** -0.5 * LOG2E
    w = (w * qscale).astype(BF16).reshape(Q_LORA_RANK, N_HEADS // 2, 2, QK_NOPE_DIM + QK_ROPE_DIM)
    pad = jnp.zeros((Q_LORA_RANK, N_HEADS // 2, LANES - QK_NOPE_DIM - QK_ROPE_DIM), BF16)
    ev, od = w[:, :, 0], w[:, :, 1]
    ev = jnp.concatenate([ev, pad], axis=-1)
    od = jnp.concatenate([od[..., QK_NOPE_DIM:], pad, od[..., :QK_NOPE_DIM]], axis=-1)
    return jnp.stack([ev, od], axis=2).reshape(Q_LORA_RANK, HEAD_COLS)


def kernel(x, c, ctx, c_ctx, w_ada, b_ada, norm1_g, w_in, q_norm_g, kv_norm_g, w_uq, w_ukv,
           conv_w, conv_b, w_attn_out, w_conv_out, w_o, norm2_g, w_up, ffn_conv_w, ffn_conv_b,
           w_down, final_g):
    assert x.shape == (BATCH, SEQ, D_MODEL) and ctx.shape == (BATCH, CTX_LEN, D_MODEL)
    assert w_ada.shape[0] == 1, "single-layer block"

    cvec = jnp.concatenate([c, c_ctx[None, :], jnp.zeros((16 - BATCH - 1, D_MODEL), F32)], axis=0)
    mod = _ada(cvec, w_ada.reshape(D_MODEL, 6 * D_MODEL), b_ada.reshape(1, 6 * D_MODEL))
    mod_lat = mod[:BATCH].reshape(BATCH, 6, D_MODEL)
    mod_ctx = mod[BATCH:BATCH + 1].reshape(1, 6, D_MODEL)

    tab = _rope_tables()
    w_lat, w_cx, w_cb, w_cc, w_ga, w_gc = _split_w_in(w_in.reshape(D_MODEL, O_END).T)
    w_kn, w_vt = _pack_w_ukv(w_ukv.reshape(KV_LORA_RANK, -1))
    w_q_p = _pack_w_uq(w_uq.reshape(Q_LORA_RANK, -1))
    g1 = norm1_g.reshape(1, D_MODEL)
    kvg = kv_norm_g.reshape(1, KV_LORA_RANK)
    qg = q_norm_g.reshape(1, Q_LORA_RANK)

    x2 = x.reshape(BATCH * SEQ, D_MODEL)
    c2 = ctx.reshape(BATCH * CTX_LEN, D_MODEL)

    kc, vct = _ctxproj(c2, mod_ctx, g1, w_lat, kvg, w_kn, w_vt)
    q, kl, vlt, gg = _inproj(x2, mod_lat, tab, g1, w_lat, w_cx, w_cb, w_cc, w_ga, w_gc, kvg, qg,
                             w_kn, w_vt, w_q_p,
                            conv_w.reshape(3, CONV_DIM), conv_b.reshape(1, CONV_DIM),
                            w_conv_out.reshape(CONV_DIM, D_MODEL).astype(BF16))
    o, (wao, wo, wup, wdn) = _attn(q, kl, vlt, kc, vct, (
        w_attn_out.reshape(ATTN_DIM, D_MODEL), w_o.reshape(D_MODEL, D_MODEL),
        w_up.reshape(D_MODEL, 2 * D_FF), w_down.reshape(D_FF, D_MODEL)))
    out = _post(o, gg, x2, mod_lat, wao, wo, norm2_g.reshape(1, D_MODEL), wup,
                ffn_conv_w.reshape(3, 2 * D_FF), ffn_conv_b.reshape(1, 2 * D_FF), wdn,
                final_g.reshape(1, D_MODEL))
    return out.reshape(BATCH, SEQ, D_MODEL)
```

```python
import functools

import jax
import jax.numpy as jnp
import numpy as np
from jax import lax
from jax.experimental import pallas as pl
from jax.experimental.pallas import tpu as pltpu

D_MODEL = 1024
BATCH = 8
SEQ = 2048
GRID_W = 64
CTX_LEN = 256
N_HEADS = 8
QK_NOPE_DIM = 64
QK_ROPE_DIM = 32
V_HEAD_DIM = 64
Q_LORA_RANK = 384
KV_LORA_RANK = 256
ROPE_THETA = 10000.0
CONV_DIM = 512
D_FF = 2816
EPS = 1e-6
ATTN_DIM = N_HEADS * V_HEAD_DIM

LANES = 128
HALF = LANES // 2
HEAD_COLS = N_HEADS * LANES
ROPE_HALF = QK_ROPE_DIM // 4
HALO = 8
HALO_BF16 = 16
CAST_ROWS = 16
DEN_ROWS = 16

C_KV = 0
C_KR = C_KV + KV_LORA_RANK
C_Q = C_KR + LANES
C_LAT = C_Q + Q_LORA_RANK
O_CX = KV_LORA_RANK + QK_ROPE_DIM + Q_LORA_RANK
O_CB = O_CX + CONV_DIM
O_CC = O_CB + CONV_DIM
O_GA = O_CC + CONV_DIM
O_GC = O_GA + D_MODEL
O_END = O_GC + D_MODEL

LOG2E = 1.4426950408889634

VMEM_LIMIT = 56 * 1024 * 1024

TM_IN = 1024
TM = 512
TQ = 1024
FF_CHUNK = 256

BF16 = jnp.bfloat16
F32 = jnp.float32


def _dot(a, b):
    return jnp.dot(a, b, preferred_element_type=F32)


def _dot_nt(a, b):
    return lax.dot_general(a, b, (((1,), (1,)), ((), ())), preferred_element_type=F32)


def _rms(x, g):
    return x * lax.rsqrt(jnp.mean(x * x, axis=-1, keepdims=True) + EPS) * g


def _sigmoid(x):
    return 1.0 / (1.0 + jnp.exp(-x))


def _rope_block(xh, c, s0, s1):
    return (xh * c + pltpu.roll(xh, LANES - ROPE_HALF, 1) * s0
            + pltpu.roll(xh, ROPE_HALF, 1) * s1)


def _conv3(u_all, m, first, last, w_ref, b_ref, sl):
    u = u_all[:m]
    ext = jnp.concatenate([jnp.where(first, 0.0, u_all[m:m + HALO]), u,
                           jnp.where(last, 0.0, u_all[m + HALO:])], axis=0)
    u_dn = pltpu.roll(ext, 1, 0)[HALO:HALO + m]
    u_up = pltpu.roll(ext, m + 2 * HALO - 1, 0)[HALO:HALO + m]
    return b_ref[:, sl] + u_dn * w_ref[0:1, sl] + u * w_ref[1:2, sl] + u_up * w_ref[2:3, sl]


def _const_spec(shape):
    nd = len(shape)
    return pl.BlockSpec(shape, lambda *_: (0,) * nd, pipeline_mode=pl.Buffered(1))


def _params(n_grid):
    return pltpu.CompilerParams(dimension_semantics=("arbitrary",) * n_grid,
                                vmem_limit_bytes=VMEM_LIMIT)


def _ada_kernel(c_ref, w_ref, b_ref, o_ref):
    c = c_ref[...]
    s = (c * _sigmoid(c)).astype(BF16)
    o_ref[...] = _dot(s, w_ref[...].astype(BF16)) + b_ref[...]


def _ada(cvec, w_ada, b_ada):
    rows = cvec.shape[0]
    n = w_ada.shape[1]
    tn = 1024
    return pl.pallas_call(
        _ada_kernel,
        name="ada",
        grid=(n // tn,),
        in_specs=[pl.BlockSpec((rows, D_MODEL), lambda j: (0, 0)),
                  pl.BlockSpec((D_MODEL, tn), lambda j: (0, j)),
                  pl.BlockSpec((1, tn), lambda j: (0, j))],
        out_specs=pl.BlockSpec((rows, tn), lambda j: (0, j)),
        out_shape=jax.ShapeDtypeStruct((rows, n), F32),
        compiler_params=_params(1),
    )(cvec, w_ada, b_ada)


def _kv_from(pa, kr_even, kvg_ref, wkn_ref, wvt_ref, k_ref, vt_ref):
    ckv = _rms(pa[:, C_KV:C_KV + KV_LORA_RANK], kvg_ref[...]).astype(BF16)
    kn = _dot(ckv, wkn_ref[...])
    kr_odd = pltpu.roll(kr_even, HALF, 1)
    lo = lax.broadcasted_iota(jnp.int32, kr_even.shape, 1) < HALF
    for p in range(N_HEADS // 2):
        kn_pair = kn[:, p * LANES:(p + 1) * LANES]
        k_ref[:, (2 * p) * LANES:(2 * p + 1) * LANES] = jnp.where(lo, kn_pair, kr_even).astype(BF16)
        k_ref[:, (2 * p + 1) * LANES:(2 * p + 2) * LANES] = jnp.where(lo, kr_odd, kn_pair).astype(BF16)
    vt = _dot_nt(wvt_ref[...], ckv).astype(BF16)
    per_seq = vt_ref.shape[1]
    for b in range(vt.shape[1] // per_seq):
        vt_ref[b * ATTN_DIM:(b + 1) * ATTN_DIM, :] = vt[:, b * per_seq:(b + 1) * per_seq]


def _inproj_kernel(x_ref, xp_ref, xn_ref, mod_ref, tab_ref, g1_ref, wlat_ref, wcx_ref, wcb_ref,
                   wcc_ref, wga_ref, wgc_ref, kvg_ref, qg_ref, wkn_ref, wvt_ref, wq_ref, cw_ref,
                   cb_ref, wco_ref, q_ref, k_ref, vt_ref, gg_ref):
    s = pl.program_id(0)
    first = s == 0
    last = s == pl.num_programs(0) - 1
    sh = mod_ref[0, 0:1, :]
    sc = mod_ref[0, 1:2, :]
    g1 = g1_ref[...]
    xcat = jnp.concatenate([x_ref[...], xp_ref[...], xn_ref[...]], axis=0)
    hcat = (_rms(xcat, g1) * (1.0 + sc) + sh).astype(BF16)
    tm = x_ref.shape[0]
    hb = hcat[:tm]

    pa = _dot(hb, wlat_ref[...])
    x_in = _dot(hcat, wcx_ref[...])
    b_gate = _dot(hb, wcb_ref[...])
    c_gate = _dot(hcat, wcc_ref[...])

    tabs = [[tab_ref[0], tab_ref[1], tab_ref[2]]]
    tabs.append([pltpu.roll(t, HALF, 1) for t in tabs[0]])
    kr = _rope_block(pa[:, C_KR:C_KR + LANES], *tabs[0])
    _kv_from(pa, kr, kvg_ref, wkn_ref, wvt_ref, k_ref, vt_ref)
    cq = _rms(pa[:, C_Q:C_Q + Q_LORA_RANK], qg_ref[...]).astype(BF16)
    qf = _dot(cq, wq_ref[...])
    for h in range(N_HEADS):
        sl = slice(h * LANES, (h + 1) * LANES)
        q_ref[sl, :] = _rope_block(qf[:, sl], *tabs[h % 2]).T.astype(BF16)

    gg_ref[:, :D_MODEL] = _sigmoid(_dot(hb, wga_ref[...])).astype(BF16)

    conv = _conv3(c_gate * x_in, tm, first, last, cw_ref, cb_ref, slice(0, CONV_DIM))
    y_conv = _dot((b_gate * conv).astype(BF16), wco_ref[...])
    gg_ref[:, D_MODEL:] = (_sigmoid(_dot(hb, wgc_ref[...])) * y_conv).astype(BF16)


def _halo_specs(tm, cols, rows=HALO):
    tps = SEQ // tm
    r = tm // rows
    nblk = BATCH * SEQ // rows
    prev = pl.BlockSpec((rows, cols), lambda s, b: (jnp.maximum((b * tps + s) * r - 1, 0), 0))
    nxt = pl.BlockSpec((rows, cols), lambda s, b: (jnp.minimum((b * tps + s + 1) * r, nblk - 1), 0))
    return prev, nxt


def _inproj(x2, mod_lat, tab, g1, w_lat, w_cx, w_cb, w_cc, w_ga, w_gc, kvg, qg, w_kn, w_vt, w_q_p,
            cw, cb, wco):
    n = x2.shape[0]
    tm = TM_IN
    tps = SEQ // tm
    row = lambda s, b: (b * tps + s, 0)
    xprev, xnext = _halo_specs(tm, D_MODEL)
    return pl.pallas_call(
        _inproj_kernel,
        name="inproj",
        grid=(tps, BATCH),
        in_specs=[
            pl.BlockSpec((tm, D_MODEL), row),
            xprev,
            xnext,
            pl.BlockSpec((1, 6, D_MODEL), lambda s, b: (b, 0, 0)),
            pl.BlockSpec((3, tm, LANES), lambda s, b: (0, s, 0)),
            _const_spec((1, D_MODEL)),
            _const_spec((D_MODEL, C_LAT)),
            _const_spec((D_MODEL, CONV_DIM)),
            _const_spec((D_MODEL, CONV_DIM)),
            _const_spec((D_MODEL, CONV_DIM)),
            _const_spec((D_MODEL, D_MODEL)),
            _const_spec((D_MODEL, D_MODEL)),
            _const_spec((1, KV_LORA_RANK)),
            _const_spec((1, Q_LORA_RANK)),
            _const_spec((KV_LORA_RANK, N_HEADS * QK_NOPE_DIM)),
            _const_spec((ATTN_DIM, KV_LORA_RANK)),
            _const_spec((Q_LORA_RANK, HEAD_COLS)),
            _const_spec((3, CONV_DIM)),
            _const_spec((1, CONV_DIM)),
            _const_spec((CONV_DIM, D_MODEL)),
        ],
        out_specs=[pl.BlockSpec((HEAD_COLS, tm), lambda s, b: (0, b * tps + s)),
                   pl.BlockSpec((tm, HEAD_COLS), row),
                   pl.BlockSpec((ATTN_DIM, tm), lambda s, b: (b, s)),
                   pl.BlockSpec((tm, 2 * D_MODEL), row)],
        out_shape=[jax.ShapeDtypeStruct((HEAD_COLS, n), BF16), jax.ShapeDtypeStruct((n, HEAD_COLS), BF16),
                   jax.ShapeDtypeStruct((BATCH * ATTN_DIM, SEQ), BF16),
                   jax.ShapeDtypeStruct((n, 2 * D_MODEL), BF16)],
        compiler_params=_params(2),
    )(x2, x2, x2, mod_lat, tab, g1, w_lat, w_cx, w_cb, w_cc, w_ga, w_gc, kvg, qg, w_kn, w_vt, w_q_p,
      cw, cb, wco)


def _ctxproj_kernel(x_ref, mod_ref, g1_ref, win_ref, kvg_ref, wkn_ref, wvt_ref, k_ref, vt_ref):
    sh = mod_ref[0, 0:1, :]
    sc = mod_ref[0, 1:2, :]
    hb = (_rms(x_ref[...], g1_ref[...]) * (1.0 + sc) + sh).astype(BF16)
    pa = _dot(hb, win_ref[...])
    _kv_from(pa, pa[:, C_KR:C_KR + LANES], kvg_ref, wkn_ref, wvt_ref, k_ref, vt_ref)


def _ctxproj(c2, mod_ctx, g1, w_lat, kvg, w_kn, w_vt):
    n = c2.shape[0]
    seqs = 4
    tm = seqs * CTX_LEN
    row = lambda i: (i, 0)
    return pl.pallas_call(
        _ctxproj_kernel,
        name="ctxproj",
        grid=(n // tm,),
        in_specs=[
            pl.BlockSpec((tm, D_MODEL), row),
            _const_spec((1, 6, D_MODEL)),
            _const_spec((1, D_MODEL)),
            _const_spec((D_MODEL, C_Q)),
            _const_spec((1, KV_LORA_RANK)),
            _const_spec((KV_LORA_RANK, N_HEADS * QK_NOPE_DIM)),
            _const_spec((ATTN_DIM, KV_LORA_RANK)),
        ],
        out_specs=[pl.BlockSpec((tm, HEAD_COLS), row),
                   pl.BlockSpec((seqs * ATTN_DIM, CTX_LEN), row)],
        out_shape=[jax.ShapeDtypeStruct((n, HEAD_COLS), BF16),
                   jax.ShapeDtypeStruct((BATCH * ATTN_DIM, CTX_LEN), BF16)],
        compiler_params=_params(1),
    )(c2, mod_ctx, g1, w_lat, kvg, w_kn, w_vt)


def _attn_kernel(qt_ref, kl_ref, vlt_ref, kc_ref, vct_ref, *rest):
    n_cast = len(rest) // 2
    o_ref = rest[n_cast]

    def scores(h):
        sl = slice(h * LANES, (h + 1) * LANES)
        qt = qt_ref[sl, :]
        s_c, s_l = _dot(kc_ref[:, sl], qt), _dot(kl_ref[:, sl], qt)
        pm = jnp.maximum(jnp.max(s_c.reshape(CTX_LEN // HALO, HALO, TQ), axis=0),
                         jnp.max(s_l.reshape(SEQ // HALO, HALO, TQ), axis=0))
        return s_c, s_l, jnp.max(pm, axis=0, keepdims=True)

    nxt = scores(0)
    halves = []
    for h in range(N_HEADS):
        s_c, s_l, m = nxt
        if h + 1 < N_HEADS:
            nxt = scores(h + 1)
        vsl = slice((h // 2) * LANES, (h // 2 + 1) * LANES)
        vct = jnp.concatenate([vct_ref[vsl, :], jnp.ones((DEN_ROWS, CTX_LEN), BF16)], axis=0)
        vlt = jnp.concatenate([vlt_ref[vsl, :], jnp.ones((DEN_ROWS, SEQ), BF16)], axis=0)
        acc = (_dot(vct, jnp.exp2(s_c - m).astype(BF16))
               + _dot(vlt, jnp.exp2(s_l - m).astype(BF16)))
        rows = slice((h % 2) * V_HEAD_DIM, (h % 2 + 1) * V_HEAD_DIM)
        halves.append(acc[rows] * (1.0 / acc[LANES:LANES + 1]))
        if h % 2 == 1:
            o_ref[:, vsl] = jnp.concatenate(halves[-2:], axis=0).T.astype(BF16)

    for src, dst in zip(rest[:n_cast], rest[n_cast + 1:]):
        dst[...] = src[...].astype(BF16)


def _attn(q, kl, vlt, kc, vct, cast_weights):
    tq = TQ
    nq = SEQ // tq
    steps = BATCH * nq
    cast_specs = []
    for w in cast_weights:
        rows, cols = w.shape
        blk = next(r for r in range(CAST_ROWS, rows + 1, CAST_ROWS)
                   if rows % r == 0 and r * steps >= rows)
        nblk = rows // blk
        cast_specs.append(pl.BlockSpec((blk, cols), functools.partial(
            lambda b, j, nblk: (jnp.minimum(b * nq + j, nblk - 1), 0), nblk=nblk)))
    outs = pl.pallas_call(
        _attn_kernel,
        name="attn",
        grid=(BATCH, nq),
        in_specs=[
            pl.BlockSpec((HEAD_COLS, tq), lambda b, j: (0, b * nq + j)),
            pl.BlockSpec((SEQ, HEAD_COLS), lambda b, j: (b, 0)),
            pl.BlockSpec((ATTN_DIM, SEQ), lambda b, j: (b, 0)),
            pl.BlockSpec((CTX_LEN, HEAD_COLS), lambda b, j: (b, 0)),
            pl.BlockSpec((ATTN_DIM, CTX_LEN), lambda b, j: (b, 0)),
        ] + cast_specs,
        out_specs=[pl.BlockSpec((tq, ATTN_DIM), lambda b, j: (b * nq + j, 0))] + cast_specs,
        out_shape=[jax.ShapeDtypeStruct((BATCH * SEQ, ATTN_DIM), BF16)]
                  + [jax.ShapeDtypeStruct(w.shape, BF16) for w in cast_weights],
        compiler_params=_params(2),
    )(q, kl, vlt, kc, vct, *cast_weights)
    return outs[0], outs[1:]


def _halo_rows_bf16(prev_ref, next_ref):
    p = prev_ref[...].astype(F32)[HALO_BF16 - HALO:]
    n = next_ref[...].astype(F32)[:HALO]
    return jnp.concatenate([p, n], axis=0).astype(BF16)


def _post_kernel(o_ref, op_ref, on_ref, gg_ref, gp_ref, gn_ref, x_ref, xp_ref, xn_ref, mod_ref,
                 wao_ref, wo_ref, g2_ref, wup_ref, fcw_ref, fcb_ref, wdn_ref, gf_ref,
                 out_ref, act_ref):
    s = pl.program_id(0)
    first = s == 0
    last = s == pl.num_programs(0) - 1
    hm = TM // 2
    os_ = (o_ref[:hm], jnp.concatenate([o_ref[hm:], _halo_rows_bf16(op_ref, on_ref)], axis=0))
    ggs = (gg_ref[:hm], jnp.concatenate([gg_ref[hm:], _halo_rows_bf16(gp_ref, gn_ref)], axis=0))
    xs = (x_ref[:hm], jnp.concatenate([x_ref[hm:], xp_ref[...], xn_ref[...]], axis=0))
    y_attn = [_dot(o, wao_ref[...]) for o in os_]
    ms = [(gg[:, :D_MODEL].astype(F32) * y + gg[:, D_MODEL:].astype(F32)).astype(BF16)
          for gg, y in zip(ggs, y_attn)]
    x1s = [x + mod_ref[0, 2:3, :] * _dot(m, wo_ref[...]) for x, m in zip(xs, ms)]
    h2 = jnp.concatenate(
        [(_rms(x1, g2_ref[...]) * (1.0 + mod_ref[0, 4:5, :]) + mod_ref[0, 3:4, :]).astype(BF16)
         for x1 in x1s], axis=0)

    for c0 in range(0, D_FF, FF_CHUNK):
        c1 = min(c0 + FF_CHUNK, D_FF)
        gsl = slice(c0, c1)
        vsl = slice(D_FF + c0, D_FF + c1)
        gate = _conv3(_dot(h2, wup_ref[:, gsl]), TM, first, last, fcw_ref, fcb_ref, gsl)
        val = _conv3(_dot(h2, wup_ref[:, vsl]), TM, first, last, fcw_ref, fcb_ref, vsl)
        act_ref[:, gsl] = (gate * _sigmoid(gate) * val).astype(BF16)

    ys = [_dot(act_ref[r * hm:(r + 1) * hm, :], wdn_ref[...]) for r in range(2)]
    for r in range(2):
        x2 = x1s[r][:hm] + mod_ref[0, 5:6, :] * ys[r]
        out_ref[r * hm:(r + 1) * hm, :] = _rms(x2, gf_ref[...])


def _post(o, gg, x2, mod_lat, wao, wo, g2, wup, fcw, fcb, wdn, gf):
    n = x2.shape[0]
    tps = SEQ // TM
    row = lambda s, b: (b * tps + s, 0)
    oprev, onext = _halo_specs(TM, ATTN_DIM, HALO_BF16)
    gprev, gnext = _halo_specs(TM, 2 * D_MODEL, HALO_BF16)
    xprev, xnext = _halo_specs(TM, D_MODEL)
    return pl.pallas_call(
        _post_kernel,
        name="post",
        grid=(tps, BATCH),
        in_specs=[
            pl.BlockSpec((TM, ATTN_DIM), row), oprev, onext,
            pl.BlockSpec((TM, 2 * D_MODEL), row), gprev, gnext,
            pl.BlockSpec((TM, D_MODEL), row), xprev, xnext,
            pl.BlockSpec((1, 6, D_MODEL), lambda s, b: (b, 0, 0)),
            _const_spec((ATTN_DIM, D_MODEL)),
            _const_spec((D_MODEL, D_MODEL)),
            _const_spec((1, D_MODEL)),
            _const_spec((D_MODEL, 2 * D_FF)),
            _const_spec((3, 2 * D_FF)),
            _const_spec((1, 2 * D_FF)),
            _const_spec((D_FF, D_MODEL)),
            _const_spec((1, D_MODEL)),
        ],
        out_specs=pl.BlockSpec((TM, D_MODEL), row),
        out_shape=jax.ShapeDtypeStruct((n, D_MODEL), F32),
        scratch_shapes=[pltpu.VMEM((TM, D_FF), BF16)],
        compiler_params=_params(2),
    )(o, o, o, gg, gg, gg, x2, x2, x2, mod_lat, wao, wo, g2, wup, fcw, fcb, wdn, gf)


def _rope_tables():
    f32 = np.float32
    rows = SEQ // GRID_W
    row = np.repeat(np.arange(rows), GRID_W).astype(f32)
    col = np.tile(np.arange(GRID_W), rows).astype(f32)
    axis_dim = QK_ROPE_DIM // 2
    inv = np.power(f32(ROPE_THETA), -np.arange(0, axis_dim, 2, dtype=f32) / f32(axis_dim)).astype(f32)
    ang = np.concatenate([row[:, None] * inv, col[:, None] * inv], axis=-1)
    j = np.arange(QK_ROPE_DIM)
    idx = (j // axis_dim) * ROPE_HALF + j % ROPE_HALF
    first_half = ((j % axis_dim) // ROPE_HALF) == 0
    cos = np.cos(ang).astype(f32)[:, idx]
    sin = np.sin(ang).astype(f32)[:, idx]
    s0 = np.where(first_half, -sin, f32(0))
    s1 = np.where(first_half, f32(0), sin)
    pad = np.zeros((SEQ, LANES - QK_NOPE_DIM - QK_ROPE_DIM), f32)

    def even(nope_val, rope_part):
        return np.concatenate([np.full((SEQ, QK_NOPE_DIM), nope_val, f32), rope_part, pad], axis=1)

    return jnp.asarray(np.stack([even(1.0, cos), even(0.0, s0), even(0.0, s1)]).astype(f32))


def _split_w_in_kernel(wt_ref, lat_ref, cx_ref, cb_ref, cc_ref, ga_ref, gc_ref):
    kr_end = KV_LORA_RANK + QK_ROPE_DIM
    krq = wt_ref[KV_LORA_RANK:KV_LORA_RANK + LANES, :].T
    lane = lax.broadcasted_iota(jnp.int32, krq.shape, 1)
    kr = jnp.where((lane >= QK_NOPE_DIM) & (lane < QK_NOPE_DIM + QK_ROPE_DIM),
                   pltpu.roll(krq, QK_NOPE_DIM, 1), 0.0)
    lat_ref[...] = jnp.concatenate([wt_ref[:KV_LORA_RANK, :].T, kr, wt_ref[kr_end:O_CX, :].T],
                                   axis=1).astype(BF16)
    for ref, (a, b) in ((cx_ref, (O_CX, O_CB)), (cb_ref, (O_CB, O_CC)), (cc_ref, (O_CC, O_GA)),
                        (ga_ref, (O_GA, O_GC)), (gc_ref, (O_GC, O_END))):
        ref[...] = wt_ref[a:b, :].T.astype(BF16)


def _split_w_in(wt):
    widths = (C_LAT, CONV_DIM, CONV_DIM, CONV_DIM, D_MODEL, D_MODEL)
    return pl.pallas_call(
        _split_w_in_kernel,
        name="split_w_in",
        grid=(D_MODEL // LANES,),
        in_specs=[pl.BlockSpec((O_END, LANES), lambda i: (0, i))],
        out_specs=[pl.BlockSpec((LANES, n), lambda i: (i, 0)) for n in widths],
        out_shape=[jax.ShapeDtypeStruct((D_MODEL, n), BF16) for n in widths],
        compiler_params=_params(1),
    )(wt)


def _pack_w_ukv(w):
    w = w.astype(BF16).reshape(KV_LORA_RANK, N_HEADS, QK_NOPE_DIM + V_HEAD_DIM)
    return (w[..., :QK_NOPE_DIM].reshape(KV_LORA_RANK, N_HEADS * QK_NOPE_DIM),
            w[..., QK_NOPE_DIM:].reshape(KV_LORA_RANK, ATTN_DIM).T)


def _pack_w_uq(w):
    qscale = (QK_NOPE_DIM + QK_ROPE_DIM) ** -0.5 * LOG2E
    w = (w * qscale).astype(BF16)
    pad = jnp.zeros((Q_LORA_RANK, LANES - QK_NOPE_DIM - QK_ROPE_DIM), BF16)
    pieces = []
    for h in range(N_HEADS):
        c0 = h * (QK_NOPE_DIM + QK_ROPE_DIM)
        nope, rope = w[:, c0:c0 + QK_NOPE_DIM], w[:, c0 + QK_NOPE_DIM:c0 + QK_NOPE_DIM + QK_ROPE_DIM]
        pieces += [nope, rope, pad] if h % 2 == 0 else [rope, pad, nope]
    return jnp.concatenate(pieces, axis=1)


def kernel(x, c, ctx, c_ctx, w_ada, b_ada, norm1_g, w_in, q_norm_g, kv_norm_g, w_uq, w_ukv,
           conv_w, conv_b, w_attn_out, w_conv_out, w_o, norm2_g, w_up, ffn_conv_w, ffn_conv_b,
           w_down, final_g):
    assert x.shape == (BATCH, SEQ, D_MODEL) and ctx.shape == (BATCH, CTX_LEN, D_MODEL)
    assert w_ada.shape[0] == 1, "single-layer block"

    cvec = jnp.concatenate([c, c_ctx[None, :], jnp.zeros((16 - BATCH - 1, D_MODEL), F32)], axis=0)
    mod = _ada(cvec, w_ada.reshape(D_MODEL, 6 * D_MODEL), b_ada.reshape(1, 6 * D_MODEL))
    mod_lat = mod[:BATCH].reshape(BATCH, 6, D_MODEL)
    mod_ctx = mod[BATCH:BATCH + 1].reshape(1, 6, D_MODEL)

    tab = _rope_tables()
    w_lat, w_cx, w_cb, w_cc, w_ga, w_gc = _split_w_in(w_in.reshape(D_MODEL, O_END).T)
    w_kn, w_vt = _pack_w_ukv(w_ukv.reshape(KV_LORA_RANK, -1))
    w_q_p = _pack_w_uq(w_uq.reshape(Q_LORA_RANK, -1))
    g1 = norm1_g.reshape(1, D_MODEL)
    kvg = kv_norm_g.reshape(1, KV_LORA_RANK)
    qg = q_norm_g.reshape(1, Q_LORA_RANK)

    x2 = x.reshape(BATCH * SEQ, D_MODEL)
    c2 = ctx.reshape(BATCH * CTX_LEN, D_MODEL)

    kc, vct = _ctxproj(c2, mod_ctx, g1, w_lat, kvg, w_kn, w_vt)
    q, kl, vlt, gg = _inproj(x2, mod_lat, tab, g1, w_lat, w_cx, w_cb, w_cc, w_ga, w_gc, kvg, qg,
                             w_kn, w_vt, w_q_p,
                            conv_w.reshape(3, CONV_DIM), conv_b.reshape(1, CONV_DIM),
                            w_conv_out.reshape(CONV_DIM, D_MODEL).astype(BF16))
    o, (wao, wo, wup, wdn) = _attn(q, kl, vlt, kc, vct, (
        w_attn_out.reshape(ATTN_DIM, D_MODEL), w_o.reshape(D_MODEL, D_MODEL),
        w_up.reshape(D_MODEL, 2 * D_FF), w_down.reshape(D_FF, D_MODEL)))
    out = _post(o, gg, x2, mod_lat, wao, wo, norm2_g.reshape(1, D_MODEL), wup,
                ffn_conv_w.reshape(3, 2 * D_FF), ffn_conv_b.reshape(1, 2 * D_FF), wdn,
                final_g.reshape(1, D_MODEL))
    return out.reshape(BATCH, SEQ, D_MODEL)
```

```python
import functools

import jax
import jax.numpy as jnp
import numpy as np
from jax import lax
from jax.experimental import pallas as pl
from jax.experimental.pallas import tpu as pltpu

D_MODEL = 1024
BATCH = 8
SEQ = 2048
GRID_W = 64
CTX_LEN = 256
N_HEADS = 8
QK_NOPE_DIM = 64
QK_ROPE_DIM = 32
V_HEAD_DIM = 64
Q_LORA_RANK = 384
KV_LORA_RANK = 256
ROPE_THETA = 10000.0
CONV_DIM = 512
D_FF = 2816
EPS = 1e-6
ATTN_DIM = N_HEADS * V_HEAD_DIM

LANES = 128
HALF = LANES // 2
HEAD_COLS = N_HEADS * LANES
ROPE_HALF = QK_ROPE_DIM // 4
HALO = 8
HALO_BF16 = 16
CAST_ROWS = 16
DEN_ROWS = 16

C_KV = 0
C_KR = C_KV + KV_LORA_RANK
C_Q = C_KR + LANES
C_LAT = C_Q + Q_LORA_RANK
O_CX = KV_LORA_RANK + QK_ROPE_DIM + Q_LORA_RANK
O_CB = O_CX + CONV_DIM
O_CC = O_CB + CONV_DIM
O_GA = O_CC + CONV_DIM
O_GC = O_GA + D_MODEL
O_END = O_GC + D_MODEL

LOG2E = 1.4426950408889634

VMEM_LIMIT = 56 * 1024 * 1024

TM_IN = 1024
TM = 512
TQ = 1024
FF_CHUNK = 256

BF16 = jnp.bfloat16
F32 = jnp.float32


def _dot(a, b):
    return jnp.dot(a, b, preferred_element_type=F32)


def _dot_nt(a, b):
    return lax.dot_general(a, b, (((1,), (1,)), ((), ())), preferred_element_type=F32)


def _rms(x, g):
    return x * lax.rsqrt(jnp.mean(x * x, axis=-1, keepdims=True) + EPS) * g


def _sigmoid(x):
    return 1.0 / (1.0 + jnp.exp(-x))


def _rope_block(xh, c, s0, s1):
    return (xh * c + pltpu.roll(xh, LANES - ROPE_HALF, 1) * s0
            + pltpu.roll(xh, ROPE_HALF, 1) * s1)


def _conv3(u_all, m, first, last, w_ref, b_ref, sl):
    u = u_all[:m]
    ext = jnp.concatenate([jnp.where(first, 0.0, u_all[m:m + HALO]), u,
                           jnp.where(last, 0.0, u_all[m + HALO:])], axis=0)
    u_dn = pltpu.roll(ext, 1, 0)[HALO:HALO + m]
    u_up = pltpu.roll(ext, m + 2 * HALO - 1, 0)[HALO:HALO + m]
    return b_ref[:, sl] + u_dn * w_ref[0:1, sl] + u * w_ref[1:2, sl] + u_up * w_ref[2:3, sl]


def _const_spec(shape):
    nd = len(shape)
    return pl.BlockSpec(shape, lambda *_: (0,) * nd, pipeline_mode=pl.Buffered(1))


def _params(n_grid):
    return pltpu.CompilerParams(dimension_semantics=("arbitrary",) * n_grid,
                                vmem_limit_bytes=VMEM_LIMIT)


def _ada_kernel(c_ref, w_ref, b_ref, o_ref):
    c = c_ref[...]
    s = (c * _sigmoid(c)).astype(BF16)
    o_ref[...] = _dot(s, w_ref[...].astype(BF16)) + b_ref[...]


def _ada(cvec, w_ada, b_ada):
    rows = cvec.shape[0]
    n = w_ada.shape[1]
    tn = 1024
    return pl.pallas_call(
        _ada_kernel,
        name="ada",
        grid=(n // tn,),
        in_specs=[pl.BlockSpec((rows, D_MODEL), lambda j: (0, 0)),
                  pl.BlockSpec((D_MODEL, tn), lambda j: (0, j)),
                  pl.BlockSpec((1, tn), lambda j: (0, j))],
        out_specs=pl.BlockSpec((rows, tn), lambda j: (0, j)),
        out_shape=jax.ShapeDtypeStruct((rows, n), F32),
        compiler_params=_params(1),
    )(cvec, w_ada, b_ada)


def _kv_from(pa, kr_even, kvg_ref, wkn_ref, wvt_ref, k_ref, vt_ref):
    ckv = _rms(pa[:, C_KV:C_KV + KV_LORA_RANK], kvg_ref[...]).astype(BF16)
    kn = _dot(ckv, wkn_ref[...])
    kr_odd = pltpu.roll(kr_even, HALF, 1)
    lo = lax.broadcasted_iota(jnp.int32, kr_even.shape, 1) < HALF
    for p in range(N_HEADS // 2):
        kn_pair = kn[:, p * LANES:(p + 1) * LANES]
        k_ref[:, (2 * p) * LANES:(2 * p + 1) * LANES] = jnp.where(lo, kn_pair, kr_even).astype(BF16)
        k_ref[:, (2 * p + 1) * LANES:(2 * p + 2) * LANES] = jnp.where(lo, kr_odd, kn_pair).astype(BF16)
    vt = _dot_nt(wvt_ref[...], ckv).astype(BF16)
    per_seq = vt_ref.shape[1]
    for b in range(vt.shape[1] // per_seq):
        vt_ref[b * ATTN_DIM:(b + 1) * ATTN_DIM, :] = vt[:, b * per_seq:(b + 1) * per_seq]


def _inproj_kernel(x_ref, xp_ref, xn_ref, mod_ref, tab_ref, g1_ref, wlat_ref, wcx_ref, wcb_ref,
                   wcc_ref, wga_ref, wgc_ref, kvg_ref, qg_ref, wkn_ref, wvt_ref, wq_ref, cw_ref,
                   cb_ref, wco_ref, q_ref, k_ref, vt_ref, gg_ref):
    s = pl.program_id(0)
    first = s == 0
    last = s == pl.num_programs(0) - 1
    sh = mod_ref[0, 0:1, :]
    sc = mod_ref[0, 1:2, :]
    g1 = g1_ref[...]
    xcat = jnp.concatenate([x_ref[...], xp_ref[...], xn_ref[...]], axis=0)
    hcat = (_rms(xcat, g1) * (1.0 + sc) + sh).astype(BF16)
    tm = x_ref.shape[0]
    hb = hcat[:tm]

    pa = _dot(hb, wlat_ref[...])
    x_in = _dot(hcat, wcx_ref[...])
    b_gate = _dot(hb, wcb_ref[...])
    c_gate = _dot(hcat, wcc_ref[...])

    tabs = [[tab_ref[0], tab_ref[1], tab_ref[2]]]
    tabs.append([pltpu.roll(t, HALF, 1) for t in tabs[0]])
    kr = _rope_block(pa[:, C_KR:C_KR + LANES], *tabs[0])
    _kv_from(pa, kr, kvg_ref, wkn_ref, wvt_ref, k_ref, vt_ref)
    cq = _rms(pa[:, C_Q:C_Q + Q_LORA_RANK], qg_ref[...]).astype(BF16)
    qf = _dot(cq, wq_ref[...])
    for h in range(N_HEADS):
        sl = slice(h * LANES, (h + 1) * LANES)
        q_ref[sl, :] = _rope_block(qf[:, sl], *tabs[h % 2]).T.astype(BF16)

    gg_ref[:, :D_MODEL] = _sigmoid(_dot(hb, wga_ref[...])).astype(BF16)

    conv = _conv3(c_gate * x_in, tm, first, last, cw_ref, cb_ref, slice(0, CONV_DIM))
    y_conv = _dot((b_gate * conv).astype(BF16), wco_ref[...])
    gg_ref[:, D_MODEL:] = (_sigmoid(_dot(hb, wgc_ref[...])) * y_conv).astype(BF16)


def _halo_specs(tm, cols, rows=HALO):
    tps = SEQ // tm
    r = tm // rows
    nblk = BATCH * SEQ // rows
    prev = pl.BlockSpec((rows, cols), lambda s, b: (jnp.maximum((b * tps + s) * r - 1, 0), 0))
    nxt = pl.BlockSpec((rows, cols), lambda s, b: (jnp.minimum((b * tps + s + 1) * r, nblk - 1), 0))
    return prev, nxt


def _inproj(x2, mod_lat, tab, g1, w_lat, w_cx, w_cb, w_cc, w_ga, w_gc, kvg, qg, w_kn, w_vt, w_q_p,
            cw, cb, wco):
    n = x2.shape[0]
    tm = TM_IN
    tps = SEQ // tm
    row = lambda s, b: (b * tps + s, 0)
    xprev, xnext = _halo_specs(tm, D_MODEL)
    return pl.pallas_call(
        _inproj_kernel,
        name="inproj",
        grid=(tps, BATCH),
        in_specs=[
            pl.BlockSpec((tm, D_MODEL), row),
            xprev,
            xnext,
            pl.BlockSpec((1, 6, D_MODEL), lambda s, b: (b, 0, 0)),
            pl.BlockSpec((3, tm, LANES), lambda s, b: (0, s, 0)),
            _const_spec((1, D_MODEL)),
            _const_spec((D_MODEL, C_LAT)),
            _const_spec((D_MODEL, CONV_DIM)),
            _const_spec((D_MODEL, CONV_DIM)),
            _const_spec((D_MODEL, CONV_DIM)),
            _const_spec((D_MODEL, D_MODEL)),
            _const_spec((D_MODEL, D_MODEL)),
            _const_spec((1, KV_LORA_RANK)),
            _const_spec((1, Q_LORA_RANK)),
            _const_spec((KV_LORA_RANK, N_HEADS * QK_NOPE_DIM)),
            _const_spec((ATTN_DIM, KV_LORA_RANK)),
            _const_spec((Q_LORA_RANK, HEAD_COLS)),
            _const_spec((3, CONV_DIM)),
            _const_spec((1, CONV_DIM)),
            _const_spec((CONV_DIM, D_MODEL)),
        ],
        out_specs=[pl.BlockSpec((HEAD_COLS, tm), lambda s, b: (0, b * tps + s)),
                   pl.BlockSpec((tm, HEAD_COLS), row),
                   pl.BlockSpec((ATTN_DIM, tm), lambda s, b: (b, s)),
                   pl.BlockSpec((tm, 2 * D_MODEL), row)],
        out_shape=[jax.ShapeDtypeStruct((HEAD_COLS, n), BF16), jax.ShapeDtypeStruct((n, HEAD_COLS), BF16),
                   jax.ShapeDtypeStruct((BATCH * ATTN_DIM, SEQ), BF16),
                   jax.ShapeDtypeStruct((n, 2 * D_MODEL), BF16)],
        compiler_params=_params(2),
    )(x2, x2, x2, mod_lat, tab, g1, w_lat, w_cx, w_cb, w_cc, w_ga, w_gc, kvg, qg, w_kn, w_vt, w_q_p,
      cw, cb, wco)


def _ctxproj_kernel(x_ref, mod_ref, g1_ref, win_ref, kvg_ref, wkn_ref, wvt_ref, k_ref, vt_ref):
    sh = mod_ref[0, 0:1, :]
    sc = mod_ref[0, 1:2, :]
    hb = (_rms(x_ref[...], g1_ref[...]) * (1.0 + sc) + sh).astype(BF16)
    pa = _dot(hb, win_ref[...])
    _kv_from(pa, pa[:, C_KR:C_KR + LANES], kvg_ref, wkn_ref, wvt_ref, k_ref, vt_ref)


def _ctxproj(c2, mod_ctx, g1, w_lat, kvg, w_kn, w_vt):
    n = c2.shape[0]
    seqs = 4
    tm = seqs * CTX_LEN
    row = lambda i: (i, 0)
    return pl.pallas_call(
        _ctxproj_kernel,
        name="ctxproj",
        grid=(n // tm,),
        in_specs=[
            pl.BlockSpec((tm, D_MODEL), row),
            _const_spec((1, 6, D_MODEL)),
            _const_spec((1, D_MODEL)),
            _const_spec((D_MODEL, C_Q)),
            _const_spec((1, KV_LORA_RANK)),
            _const_spec((KV_LORA_RANK, N_HEADS * QK_NOPE_DIM)),
            _const_spec((ATTN_DIM, KV_LORA_RANK)),
        ],
        out_specs=[pl.BlockSpec((tm, HEAD_COLS), row),
                   pl.BlockSpec((seqs * ATTN_DIM, CTX_LEN), row)],
        out_shape=[jax.ShapeDtypeStruct((n, HEAD_COLS), BF16),
                   jax.ShapeDtypeStruct((BATCH * ATTN_DIM, CTX_LEN), BF16)],
        compiler_params=_params(1),
    )(c2, mod_ctx, g1, w_lat, kvg, w_kn, w_vt)


def _attn_kernel(qt_ref, kl_ref, vlt_ref, kc_ref, vct_ref, *rest):
    n_cast = len(rest) // 2
    o_ref = rest[n_cast]

    def scores(h):
        sl = slice(h * LANES, (h + 1) * LANES)
        qt = qt_ref[sl, :]
        s_c, s_l = _dot(kc_ref[:, sl], qt), _dot(kl_ref[:, sl], qt)
        pm = jnp.maximum(jnp.max(s_c.reshape(CTX_LEN // HALO, HALO, TQ), axis=0),
                         jnp.max(s_l.reshape(SEQ // HALO, HALO, TQ), axis=0))
        return s_c, s_l, jnp.max(pm, axis=0, keepdims=True)

    nxt = scores(0)
    halves = []
    for h in range(N_HEADS):
        s_c, s_l, m = nxt
        if h + 1 < N_HEADS:
            nxt = scores(h + 1)
        vsl = slice((h // 2) * LANES, (h // 2 + 1) * LANES)
        vct = jnp.concatenate([vct_ref[vsl, :], jnp.ones((DEN_ROWS, CTX_LEN), BF16)], axis=0)
        vlt = jnp.concatenate([vlt_ref[vsl, :], jnp.ones((DEN_ROWS, SEQ), BF16)], axis=0)
        acc = (_dot(vct, jnp.exp2(s_c - m).astype(BF16))
               + _dot(vlt, jnp.exp2(s_l - m).astype(BF16)))
        rows = slice((h % 2) * V_HEAD_DIM, (h % 2 + 1) * V_HEAD_DIM)
        halves.append(acc[rows] * (1.0 / acc[LANES:LANES + 1]))
        if h % 2 == 1:
            o_ref[:, vsl] = jnp.concatenate(halves[-2:], axis=0).T.astype(BF16)

    for src, dst in zip(rest[:n_cast], rest[n_cast + 1:]):
        dst[...] = src[...].astype(BF16)


def _attn(q, kl, vlt, kc, vct, cast_weights):
    tq = TQ
    nq = SEQ // tq
    steps = BATCH * nq
    cast_specs = []
    for w in cast_weights:
        rows, cols = w.shape
        blk = next(r for r in range(CAST_ROWS, rows + 1, CAST_ROWS)
                   if rows % r == 0 and r * steps >= rows)
        nblk = rows // blk
        cast_specs.append(pl.BlockSpec((blk, cols), functools.partial(
            lambda b, j, nblk: (jnp.minimum(b * nq + j, nblk - 1), 0), nblk=nblk)))
    outs = pl.pallas_call(
        _attn_kernel,
        name="attn",
        grid=(BATCH, nq),
        in_specs=[
            pl.BlockSpec((HEAD_COLS, tq), lambda b, j: (0, b * nq + j)),
            pl.BlockSpec((SEQ, HEAD_COLS), lambda b, j: (b, 0)),
            pl.BlockSpec((ATTN_DIM, SEQ), lambda b, j: (b, 0)),
            pl.BlockSpec((CTX_LEN, HEAD_COLS), lambda b, j: (b, 0)),
            pl.BlockSpec((ATTN_DIM, CTX_LEN), lambda b, j: (b, 0)),
        ] + cast_specs,
        out_specs=[pl.BlockSpec((tq, ATTN_DIM), lambda b, j: (b * nq + j, 0))] + cast_specs,
        out_shape=[jax.ShapeDtypeStruct((BATCH * SEQ, ATTN_DIM), BF16)]
                  + [jax.ShapeDtypeStruct(w.shape, BF16) for w in cast_weights],
        compiler_params=_params(2),
    )(q, kl, vlt, kc, vct, *cast_weights)
    return outs[0], outs[1:]


def _halo_rows_bf16(prev_ref, next_ref):
    p = prev_ref[...].astype(F32)[HALO_BF16 - HALO:]
    n = next_ref[...].astype(F32)[:HALO]
    return jnp.concatenate([p, n], axis=0).astype(BF16)


def _post_kernel(o_ref, op_ref, on_ref, gg_ref, gp_ref, gn_ref, x_ref, xp_ref, xn_ref, mod_ref,
                 wao_ref, wo_ref, g2_ref, wup_ref, fcw_ref, fcb_ref, wdn_ref, gf_ref,
                 out_ref, act_ref):
    s = pl.program_id(0)
    first = s == 0
    last = s == pl.num_programs(0) - 1
    hm = TM // 2
    os_ = (o_ref[:hm], jnp.concatenate([o_ref[hm:], _halo_rows_bf16(op_ref, on_ref)], axis=0))
    ggs = (gg_ref[:hm], jnp.concatenate([gg_ref[hm:], _halo_rows_bf16(gp_ref, gn_ref)], axis=0))
    xs = (x_ref[:hm], jnp.concatenate([x_ref[hm:], xp_ref[...], xn_ref[...]], axis=0))
    y_attn = [_dot(o, wao_ref[...]) for o in os_]
    ms = [(gg[:, :D_MODEL].astype(F32) * y + gg[:, D_MODEL:].astype(F32)).astype(BF16)
          for gg, y in zip(ggs, y_attn)]
    x1s = [x + mod_ref[0, 2:3, :] * _dot(m, wo_ref[...]) for x, m in zip(xs, ms)]
    h2 = jnp.concatenate(
        [(_rms(x1, g2_ref[...]) * (1.0 + mod_ref[0, 4:5, :]) + mod_ref[0, 3:4, :]).astype(BF16)
         for x1 in x1s], axis=0)

    for c0 in range(0, D_FF, FF_CHUNK):
        c1 = min(c0 + FF_CHUNK, D_FF)
        gsl = slice(c0, c1)
        vsl = slice(D_FF + c0, D_FF + c1)
        gate = _conv3(_dot(h2, wup_ref[:, gsl]), TM, first, last, fcw_ref, fcb_ref, gsl)
        val = _conv3(_dot(h2, wup_ref[:, vsl]), TM, first, last, fcw_ref, fcb_ref, vsl)
        act_ref[:, gsl] = (gate * _sigmoid(gate) * val).astype(BF16)

    ys = [_dot(act_ref[r * hm:(r + 1) * hm, :], wdn_ref[...]) for r in range(2)]
    for r in range(2):
        x2 = x1s[r][:hm] + mod_ref[0, 5:6, :] * ys[r]
        out_ref[r * hm:(r + 1) * hm, :] = _rms(x2, gf_ref[...])


def _post(o, gg, x2, mod_lat, wao, wo, g2, wup, fcw, fcb, wdn, gf):
    n = x2.shape[0]
    tps = SEQ // TM
    row = lambda s, b: (b * tps + s, 0)
    oprev, onext = _halo_specs(TM, ATTN_DIM, HALO_BF16)
    gprev, gnext = _halo_specs(TM, 2 * D_MODEL, HALO_BF16)
    xprev, xnext = _halo_specs(TM, D_MODEL)
    return pl.pallas_call(
        _post_kernel,
        name="post",
        grid=(tps, BATCH),
        in_specs=[
            pl.BlockSpec((TM, ATTN_DIM), row), oprev, onext,
            pl.BlockSpec((TM, 2 * D_MODEL), row), gprev, gnext,
            pl.BlockSpec((TM, D_MODEL), row), xprev, xnext,
            pl.BlockSpec((1, 6, D_MODEL), lambda s, b: (b, 0, 0)),
            _const_spec((ATTN_DIM, D_MODEL)),
            _const_spec((D_MODEL, D_MODEL)),
            _const_spec((1, D_MODEL)),
            _const_spec((D_MODEL, 2 * D_FF)),
            _const_spec((3, 2 * D_FF)),
            _const_spec((1, 2 * D_FF)),
            _const_spec((D_FF, D_MODEL)),
            _const_spec((1, D_MODEL)),
        ],
        out_specs=pl.BlockSpec((TM, D_MODEL), row),
        out_shape=jax.ShapeDtypeStruct((n, D_MODEL), F32),
        scratch_shapes=[pltpu.VMEM((TM, D_FF), BF16)],
        compiler_params=_params(2),
    )(o, o, o, gg, gg, gg, x2, x2, x2, mod_lat, wao, wo, g2, wup, fcw, fcb, wdn, gf)


def _rope_tables():
    f32 = np.float32
    rows = SEQ // GRID_W
    row = np.repeat(np.arange(rows), GRID_W).astype(f32)
    col = np.tile(np.arange(GRID_W), rows).astype(f32)
    axis_dim = QK_ROPE_DIM // 2
    inv = np.power(f32(ROPE_THETA), -np.arange(0, axis_dim, 2, dtype=f32) / f32(axis_dim)).astype(f32)
    ang = np.concatenate([row[:, None] * inv, col[:, None] * inv], axis=-1)
    j = np.arange(QK_ROPE_DIM)
    idx = (j // axis_dim) * ROPE_HALF + j % ROPE_HALF
    first_half = ((j % axis_dim) // ROPE_HALF) == 0
    cos = np.cos(ang).astype(f32)[:, idx]
    sin = np.sin(ang).astype(f32)[:, idx]
    s0 = np.where(first_half, -sin, f32(0))
    s1 = np.where(first_half, f32(0), sin)
    pad = np.zeros((SEQ, LANES - QK_NOPE_DIM - QK_ROPE_DIM), f32)

    def even(nope_val, rope_part):
        return np.concatenate([np.full((SEQ, QK_NOPE_DIM), nope_val, f32), rope_part, pad], axis=1)

    return jnp.asarray(np.stack([even(1.0, cos), even(0.0, s0), even(0.0, s1)]).astype(f32))


def _split_w_in_kernel(wt_ref, lat_ref, cx_ref, cb_ref, cc_ref, ga_ref, gc_ref):
    kr_end = KV_LORA_RANK + QK_ROPE_DIM
    krq = wt_ref[KV_LORA_RANK:KV_LORA_RANK + LANES, :].T
    lane = lax.broadcasted_iota(jnp.int32, krq.shape, 1)
    kr = jnp.where((lane >= QK_NOPE_DIM) & (lane < QK_NOPE_DIM + QK_ROPE_DIM),
                   pltpu.roll(krq, QK_NOPE_DIM, 1), 0.0)
    lat_ref[...] = jnp.concatenate([wt_ref[:KV_LORA_RANK, :].T, kr, wt_ref[kr_end:O_CX, :].T],
                                   axis=1).astype(BF16)
    for ref, (a, b) in ((cx_ref, (O_CX, O_CB)), (cb_ref, (O_CB, O_CC)), (cc_ref, (O_CC, O_GA)),
                        (ga_ref, (O_GA, O_GC)), (gc_ref, (O_GC, O_END))):
        ref[...] = wt_ref[a:b, :].T.astype(BF16)


def _split_w_in(wt):
    widths = (C_LAT, CONV_DIM, CONV_DIM, CONV_DIM, D_MODEL, D_MODEL)
    return pl.pallas_call(
        _split_w_in_kernel,
        name="split_w_in",
        grid=(D_MODEL // LANES,),
        in_specs=[pl.BlockSpec((O_END, LANES), lambda i: (0, i))],
        out_specs=[pl.BlockSpec((LANES, n), lambda i: (i, 0)) for n in widths],
        out_shape=[jax.ShapeDtypeStruct((D_MODEL, n), BF16) for n in widths],
        compiler_params=_params(1),
    )(wt)


def _pack_w_ukv(w):
    w = w.astype(BF16)
    per_head = QK_NOPE_DIM + V_HEAD_DIM
    kn = [w[:, h * per_head:h * per_head + QK_NOPE_DIM] for h in range(N_HEADS)]
    vv = [w[:, h * per_head + QK_NOPE_DIM:(h + 1) * per_head] for h in range(N_HEADS)]
    return jnp.concatenate(kn, axis=1), jnp.concatenate(vv, axis=1).T


def _pack_w_uq(w):
    qscale = (QK_NOPE_DIM + QK_ROPE_DIM) ** -0.5 * LOG2E
    w = (w * qscale).astype(BF16)
    pad = jnp.zeros((Q_LORA_RANK, LANES - QK_NOPE_DIM - QK_ROPE_DIM), BF16)
    pieces = []
    for h in range(N_HEADS):
        c0 = h * (QK_NOPE_DIM + QK_ROPE_DIM)
        nope, rope = w[:, c0:c0 + QK_NOPE_DIM], w[:, c0 + QK_NOPE_DIM:c0 + QK_NOPE_DIM + QK_ROPE_DIM]
        pieces += [nope, rope, pad] if h % 2 == 0 else [rope, pad, nope]
    return jnp.concatenate(pieces, axis=1)


def kernel(x, c, ctx, c_ctx, w_ada, b_ada, norm1_g, w_in, q_norm_g, kv_norm_g, w_uq, w_ukv,
           conv_w, conv_b, w_attn_out, w_conv_out, w_o, norm2_g, w_up, ffn_conv_w, ffn_conv_b,
           w_down, final_g):
    assert x.shape == (BATCH, SEQ, D_MODEL) and ctx.shape == (BATCH, CTX_LEN, D_MODEL)
    assert w_ada.shape[0] == 1, "single-layer block"

    cvec = jnp.concatenate([c, c_ctx[None, :], jnp.zeros((16 - BATCH - 1, D_MODEL), F32)], axis=0)
    mod = _ada(cvec, w_ada.reshape(D_MODEL, 6 * D_MODEL), b_ada.reshape(1, 6 * D_MODEL))
    mod_lat = mod[:BATCH].reshape(BATCH, 6, D_MODEL)
    mod_ctx = mod[BATCH:BATCH + 1].reshape(1, 6, D_MODEL)

    tab = _rope_tables()
    w_lat, w_cx, w_cb, w_cc, w_ga, w_gc = _split_w_in(w_in.reshape(D_MODEL, O_END).T)
    w_kn, w_vt = _pack_w_ukv(w_ukv.reshape(KV_LORA_RANK, -1))
    w_q_p = _pack_w_uq(w_uq.reshape(Q_LORA_RANK, -1))
    g1 = norm1_g.reshape(1, D_MODEL)
    kvg = kv_norm_g.reshape(1, KV_LORA_RANK)
    qg = q_norm_g.reshape(1, Q_LORA_RANK)

    x2 = x.reshape(BATCH * SEQ, D_MODEL)
    c2 = ctx.reshape(BATCH * CTX_LEN, D_MODEL)

    kc, vct = _ctxproj(c2, mod_ctx, g1, w_lat, kvg, w_kn, w_vt)
    q, kl, vlt, gg = _inproj(x2, mod_lat, tab, g1, w_lat, w_cx, w_cb, w_cc, w_ga, w_gc, kvg, qg,
                             w_kn, w_vt, w_q_p,
                            conv_w.reshape(3, CONV_DIM), conv_b.reshape(1, CONV_DIM),
                            w_conv_out.reshape(CONV_DIM, D_MODEL).astype(BF16))
    o, (wao, wo, wup, wdn) = _attn(q, kl, vlt, kc, vct, (
        w_attn_out.reshape(ATTN_DIM, D_MODEL), w_o.reshape(D_MODEL, D_MODEL),
        w_up.reshape(D_MODEL, 2 * D_FF), w_down.reshape(D_FF, D_MODEL)))
    out = _post(o, gg, x2, mod_lat, wao, wo, norm2_g.reshape(1, D_MODEL), wup,
                ffn_conv_w.reshape(3, 2 * D_FF), ffn_conv_b.reshape(1, 2 * D_FF), wdn,
                final_g.reshape(1, D_MODEL))
    return out.reshape(BATCH, SEQ, D_MODEL)
```

```python
import functools

import jax
import jax.numpy as jnp
import numpy as np
from jax import lax
from jax.experimental import pallas as pl
from jax.experimental.pallas import tpu as pltpu

D_MODEL = 1024
BATCH = 8
SEQ = 2048
GRID_W = 64
CTX_LEN = 256
N_HEADS = 8
QK_NOPE_DIM = 64
QK_ROPE_DIM = 32
V_HEAD_DIM = 64
Q_LORA_RANK = 384
KV_LORA_RANK = 256
ROPE_THETA = 10000.0
CONV_DIM = 512
D_FF = 2816
EPS = 1e-6
ATTN_DIM = N_HEADS * V_HEAD_DIM

LANES = 128
HALF = LANES // 2
HEAD_COLS = N_HEADS * LANES
ROPE_HALF = QK_ROPE_DIM // 4
HALO = 8
HALO_BF16 = 16
CAST_ROWS = 16
DEN_ROWS = 16

C_KV = 0
C_KR = C_KV + KV_LORA_RANK
C_Q = C_KR + LANES
C_LAT = C_Q + Q_LORA_RANK
O_CX = KV_LORA_RANK + QK_ROPE_DIM + Q_LORA_RANK
O_CB = O_CX + CONV_DIM
O_CC = O_CB + CONV_DIM
O_GA = O_CC + CONV_DIM
O_GC = O_GA + D_MODEL
O_END = O_GC + D_MODEL

LOG2E = 1.4426950408889634

VMEM_LIMIT = 56 * 1024 * 1024

TM_IN = 1024
TM = 512
TQ = 1024
FF_CHUNK = 256

BF16 = jnp.bfloat16
F32 = jnp.float32


def _dot(a, b):
    return jnp.dot(a, b, preferred_element_type=F32)


def _dot_nt(a, b):
    return lax.dot_general(a, b, (((1,), (1,)), ((), ())), preferred_element_type=F32)


def _rms(x, g):
    return x * lax.rsqrt(jnp.mean(x * x, axis=-1, keepdims=True) + EPS) * g


def _sigmoid(x):
    return 1.0 / (1.0 + jnp.exp(-x))


def _rope_block(xh, c, s0, s1):
    return (xh * c + pltpu.roll(xh, LANES - ROPE_HALF, 1) * s0
            + pltpu.roll(xh, ROPE_HALF, 1) * s1)


def _conv3(u_all, m, first, last, w_ref, b_ref, sl):
    u = u_all[:m]
    ext = jnp.concatenate([jnp.where(first, 0.0, u_all[m:m + HALO]), u,
                           jnp.where(last, 0.0, u_all[m + HALO:])], axis=0)
    u_dn = pltpu.roll(ext, 1, 0)[HALO:HALO + m]
    u_up = pltpu.roll(ext, m + 2 * HALO - 1, 0)[HALO:HALO + m]
    return b_ref[:, sl] + u_dn * w_ref[0:1, sl] + u * w_ref[1:2, sl] + u_up * w_ref[2:3, sl]


def _const_spec(shape):
    nd = len(shape)
    return pl.BlockSpec(shape, lambda *_: (0,) * nd, pipeline_mode=pl.Buffered(1))


def _params(n_grid):
    return pltpu.CompilerParams(dimension_semantics=("arbitrary",) * n_grid,
                                vmem_limit_bytes=VMEM_LIMIT)


def _ada_kernel(c_ref, w_ref, b_ref, o_ref):
    c = c_ref[...]
    s = (c * _sigmoid(c)).astype(BF16)
    o_ref[...] = _dot(s, w_ref[...].astype(BF16)) + b_ref[...]


def _kv_from(pa, kr_even, kvg_ref, wkn_ref, wvt_ref, k_ref, vt_ref):
    ckv = _rms(pa[:, C_KV:C_KV + KV_LORA_RANK], kvg_ref[...]).astype(BF16)
    kn = _dot(ckv, wkn_ref[...])
    kr_odd = pltpu.roll(kr_even, HALF, 1)
    lo = lax.broadcasted_iota(jnp.int32, kr_even.shape, 1) < HALF
    for p in range(N_HEADS // 2):
        kn_pair = kn[:, p * LANES:(p + 1) * LANES]
        k_ref[:, (2 * p) * LANES:(2 * p + 1) * LANES] = jnp.where(lo, kn_pair, kr_even).astype(BF16)
        k_ref[:, (2 * p + 1) * LANES:(2 * p + 2) * LANES] = jnp.where(lo, kr_odd, kn_pair).astype(BF16)
    vt = _dot_nt(wvt_ref[...], ckv).astype(BF16)
    per_seq = vt_ref.shape[1]
    for b in range(vt.shape[1] // per_seq):
        vt_ref[b * ATTN_DIM:(b + 1) * ATTN_DIM, :] = vt[:, b * per_seq:(b + 1) * per_seq]


def _inproj_kernel(x_ref, xp_ref, xn_ref, mod_ref, tab_ref, g1_ref, wlat_ref, wcx_ref, wcb_ref,
                   wcc_ref, wga_ref, wgc_ref, kvg_ref, qg_ref, wkn_ref, wvt_ref, wq_ref, cw_ref,
                   cb_ref, wco_ref, q_ref, k_ref, vt_ref, gg_ref):
    s = pl.program_id(0)
    first = s == 0
    last = s == pl.num_programs(0) - 1
    sh = mod_ref[0, 0:1, :]
    sc = mod_ref[0, 1:2, :]
    g1 = g1_ref[...]
    xcat = jnp.concatenate([x_ref[...], xp_ref[...], xn_ref[...]], axis=0)
    hcat = (_rms(xcat, g1) * (1.0 + sc) + sh).astype(BF16)
    tm = x_ref.shape[0]
    hb = hcat[:tm]

    pa = _dot(hb, wlat_ref[...])
    x_in = _dot(hcat, wcx_ref[...])
    b_gate = _dot(hb, wcb_ref[...])
    c_gate = _dot(hcat, wcc_ref[...])

    tabs = [[tab_ref[0], tab_ref[1], tab_ref[2]]]
    tabs.append([pltpu.roll(t, HALF, 1) for t in tabs[0]])
    kr = _rope_block(pa[:, C_KR:C_KR + LANES], *tabs[0])
    _kv_from(pa, kr, kvg_ref, wkn_ref, wvt_ref, k_ref, vt_ref)
    cq = _rms(pa[:, C_Q:C_Q + Q_LORA_RANK], qg_ref[...]).astype(BF16)
    qf = _dot(cq, wq_ref[...])
    for h in range(N_HEADS):
        sl = slice(h * LANES, (h + 1) * LANES)
        q_ref[sl, :] = _rope_block(qf[:, sl], *tabs[h % 2]).T.astype(BF16)

    gg_ref[:, :D_MODEL] = _sigmoid(_dot(hb, wga_ref[...])).astype(BF16)

    conv = _conv3(c_gate * x_in, tm, first, last, cw_ref, cb_ref, slice(0, CONV_DIM))
    y_conv = _dot((b_gate * conv).astype(BF16), wco_ref[...])
    gg_ref[:, D_MODEL:] = (_sigmoid(_dot(hb, wgc_ref[...])) * y_conv).astype(BF16)


def _halo_specs(tm, cols, rows=HALO):
    tps = SEQ // tm
    r = tm // rows
    nblk = BATCH * SEQ // rows
    prev = pl.BlockSpec((rows, cols), lambda s, b: (jnp.maximum((b * tps + s) * r - 1, 0), 0))
    nxt = pl.BlockSpec((rows, cols), lambda s, b: (jnp.minimum((b * tps + s + 1) * r, nblk - 1), 0))
    return prev, nxt


def _inproj(x2, mod_lat, tab, g1, w_lat, w_cx, w_cb, w_cc, w_ga, w_gc, kvg, qg, w_kn, w_vt, w_q_p,
            cw, cb, wco):
    n = x2.shape[0]
    tm = TM_IN
    tps = SEQ // tm
    row = lambda s, b: (b * tps + s, 0)
    xprev, xnext = _halo_specs(tm, D_MODEL)
    return pl.pallas_call(
        _inproj_kernel,
        name="inproj",
        grid=(tps, BATCH),
        in_specs=[
            pl.BlockSpec((tm, D_MODEL), row),
            xprev,
            xnext,
            pl.BlockSpec((1, 6, D_MODEL), lambda s, b: (b, 0, 0)),
            pl.BlockSpec((3, tm, LANES), lambda s, b: (0, s, 0)),
            _const_spec((1, D_MODEL)),
            _const_spec((D_MODEL, C_LAT)),
            _const_spec((D_MODEL, CONV_DIM)),
            _const_spec((D_MODEL, CONV_DIM)),
            _const_spec((D_MODEL, CONV_DIM)),
            _const_spec((D_MODEL, D_MODEL)),
            _const_spec((D_MODEL, D_MODEL)),
            _const_spec((1, KV_LORA_RANK)),
            _const_spec((1, Q_LORA_RANK)),
            _const_spec((KV_LORA_RANK, N_HEADS * QK_NOPE_DIM)),
            _const_spec((ATTN_DIM, KV_LORA_RANK)),
            _const_spec((Q_LORA_RANK, HEAD_COLS)),
            _const_spec((3, CONV_DIM)),
            _const_spec((1, CONV_DIM)),
            _const_spec((CONV_DIM, D_MODEL)),
        ],
        out_specs=[pl.BlockSpec((HEAD_COLS, tm), lambda s, b: (0, b * tps + s)),
                   pl.BlockSpec((tm, HEAD_COLS), row),
                   pl.BlockSpec((ATTN_DIM, tm), lambda s, b: (b, s)),
                   pl.BlockSpec((tm, 2 * D_MODEL), row)],
        out_shape=[jax.ShapeDtypeStruct((HEAD_COLS, n), BF16), jax.ShapeDtypeStruct((n, HEAD_COLS), BF16),
                   jax.ShapeDtypeStruct((BATCH * ATTN_DIM, SEQ), BF16),
                   jax.ShapeDtypeStruct((n, 2 * D_MODEL), BF16)],
        compiler_params=_params(2),
    )(x2, x2, x2, mod_lat, tab, g1, w_lat, w_cx, w_cb, w_cc, w_ga, w_gc, kvg, qg, w_kn, w_vt, w_q_p,
      cw, cb, wco)


def _ctxproj_kernel(x_ref, mod_ref, g1_ref, win_ref, kvg_ref, wkn_ref, wvt_ref, k_ref, vt_ref):
    sh = mod_ref[0, 0:1, :]
    sc = mod_ref[0, 1:2, :]
    hb = (_rms(x_ref[...], g1_ref[...]) * (1.0 + sc) + sh).astype(BF16)
    pa = _dot(hb, win_ref[...])
    _kv_from(pa, pa[:, C_KR:C_KR + LANES], kvg_ref, wkn_ref, wvt_ref, k_ref, vt_ref)


def _ctxproj(c2, mod_ctx, g1, w_lat, kvg, w_kn, w_vt):
    n = c2.shape[0]
    seqs = 4
    tm = seqs * CTX_LEN
    row = lambda i: (i, 0)
    return pl.pallas_call(
        _ctxproj_kernel,
        name="ctxproj",
        grid=(n // tm,),
        in_specs=[
            pl.BlockSpec((tm, D_MODEL), row),
            _const_spec((1, 6, D_MODEL)),
            _const_spec((1, D_MODEL)),
            _const_spec((D_MODEL, C_Q)),
            _const_spec((1, KV_LORA_RANK)),
            _const_spec((KV_LORA_RANK, N_HEADS * QK_NOPE_DIM)),
            _const_spec((ATTN_DIM, KV_LORA_RANK)),
        ],
        out_specs=[pl.BlockSpec((tm, HEAD_COLS), row),
                   pl.BlockSpec((seqs * ATTN_DIM, CTX_LEN), row)],
        out_shape=[jax.ShapeDtypeStruct((n, HEAD_COLS), BF16),
                   jax.ShapeDtypeStruct((BATCH * ATTN_DIM, CTX_LEN), BF16)],
        compiler_params=_params(1),
    )(c2, mod_ctx, g1, w_lat, kvg, w_kn, w_vt)


def _attn_kernel(qt_ref, kl_ref, vlt_ref, kc_ref, vct_ref, *rest):
    n_cast = len(rest) // 2
    o_ref = rest[n_cast]

    def scores(h):
        sl = slice(h * LANES, (h + 1) * LANES)
        qt = qt_ref[sl, :]
        s_c, s_l = _dot(kc_ref[:, sl], qt), _dot(kl_ref[:, sl], qt)
        pm = jnp.maximum(jnp.max(s_c.reshape(CTX_LEN // HALO, HALO, TQ), axis=0),
                         jnp.max(s_l.reshape(SEQ // HALO, HALO, TQ), axis=0))
        return s_c, s_l, jnp.max(pm, axis=0, keepdims=True)

    nxt = scores(0)
    halves = []
    for h in range(N_HEADS):
        s_c, s_l, m = nxt
        if h + 1 < N_HEADS:
            nxt = scores(h + 1)
        vsl = slice((h // 2) * LANES, (h // 2 + 1) * LANES)
        vct = jnp.concatenate([vct_ref[vsl, :], jnp.ones((DEN_ROWS, CTX_LEN), BF16)], axis=0)
        vlt = jnp.concatenate([vlt_ref[vsl, :], jnp.ones((DEN_ROWS, SEQ), BF16)], axis=0)
        acc = (_dot(vct, jnp.exp2(s_c - m).astype(BF16))
               + _dot(vlt, jnp.exp2(s_l - m).astype(BF16)))
        rows = slice((h % 2) * V_HEAD_DIM, (h % 2 + 1) * V_HEAD_DIM)
        halves.append(acc[rows] * (1.0 / acc[LANES:LANES + 1]))
        if h % 2 == 1:
            o_ref[:, vsl] = jnp.concatenate(halves[-2:], axis=0).T.astype(BF16)

    for src, dst in zip(rest[:n_cast], rest[n_cast + 1:]):
        dst[...] = src[...].astype(BF16)


def _attn(q, kl, vlt, kc, vct, cast_weights):
    tq = TQ
    nq = SEQ // tq
    steps = BATCH * nq
    cast_specs = []
    for w in cast_weights:
        rows, cols = w.shape
        blk = next(r for r in range(CAST_ROWS, rows + 1, CAST_ROWS)
                   if rows % r == 0 and r * steps >= rows)
        nblk = rows // blk
        cast_specs.append(pl.BlockSpec((blk, cols), functools.partial(
            lambda b, j, nblk: (jnp.minimum(b * nq + j, nblk - 1), 0), nblk=nblk)))
    outs = pl.pallas_call(
        _attn_kernel,
        name="attn",
        grid=(BATCH, nq),
        in_specs=[
            pl.BlockSpec((HEAD_COLS, tq), lambda b, j: (0, b * nq + j)),
            pl.BlockSpec((SEQ, HEAD_COLS), lambda b, j: (b, 0)),
            pl.BlockSpec((ATTN_DIM, SEQ), lambda b, j: (b, 0)),
            pl.BlockSpec((CTX_LEN, HEAD_COLS), lambda b, j: (b, 0)),
            pl.BlockSpec((ATTN_DIM, CTX_LEN), lambda b, j: (b, 0)),
        ] + cast_specs,
        out_specs=[pl.BlockSpec((tq, ATTN_DIM), lambda b, j: (b * nq + j, 0))] + cast_specs,
        out_shape=[jax.ShapeDtypeStruct((BATCH * SEQ, ATTN_DIM), BF16)]
                  + [jax.ShapeDtypeStruct(w.shape, BF16) for w in cast_weights],
        compiler_params=_params(2),
    )(q, kl, vlt, kc, vct, *cast_weights)
    return outs[0], outs[1:]


def _halo_rows_bf16(prev_ref, next_ref):
    p = prev_ref[...].astype(F32)[HALO_BF16 - HALO:]
    n = next_ref[...].astype(F32)[:HALO]
    return jnp.concatenate([p, n], axis=0).astype(BF16)


def _post_kernel(o_ref, op_ref, on_ref, gg_ref, gp_ref, gn_ref, x_ref, xp_ref, xn_ref, mod_ref,
                 wao_ref, wo_ref, g2_ref, wup_ref, fcw_ref, fcb_ref, wdn_ref, gf_ref,
                 out_ref, act_ref):
    s = pl.program_id(0)
    first = s == 0
    last = s == pl.num_programs(0) - 1
    hm = TM // 2
    os_ = (o_ref[:hm], jnp.concatenate([o_ref[hm:], _halo_rows_bf16(op_ref, on_ref)], axis=0))
    ggs = (gg_ref[:hm], jnp.concatenate([gg_ref[hm:], _halo_rows_bf16(gp_ref, gn_ref)], axis=0))
    xs = (x_ref[:hm], jnp.concatenate([x_ref[hm:], xp_ref[...], xn_ref[...]], axis=0))
    y_attn = [_dot(o, wao_ref[...]) for o in os_]
    ms = [(gg[:, :D_MODEL].astype(F32) * y + gg[:, D_MODEL:].astype(F32)).astype(BF16)
          for gg, y in zip(ggs, y_attn)]
    x1s = [x + mod_ref[0, 2:3, :] * _dot(m, wo_ref[...]) for x, m in zip(xs, ms)]
    h2 = jnp.concatenate(
        [(_rms(x1, g2_ref[...]) * (1.0 + mod_ref[0, 4:5, :]) + mod_ref[0, 3:4, :]).astype(BF16)
         for x1 in x1s], axis=0)

    for c0 in range(0, D_FF, FF_CHUNK):
        c1 = min(c0 + FF_CHUNK, D_FF)
        gsl = slice(c0, c1)
        vsl = slice(D_FF + c0, D_FF + c1)
        gate = _conv3(_dot(h2, wup_ref[:, gsl]), TM, first, last, fcw_ref, fcb_ref, gsl)
        val = _conv3(_dot(h2, wup_ref[:, vsl]), TM, first, last, fcw_ref, fcb_ref, vsl)
        act_ref[:, gsl] = (gate * _sigmoid(gate) * val).astype(BF16)

    ys = [_dot(act_ref[r * hm:(r + 1) * hm, :], wdn_ref[...]) for r in range(2)]
    for r in range(2):
        x2 = x1s[r][:hm] + mod_ref[0, 5:6, :] * ys[r]
        out_ref[r * hm:(r + 1) * hm, :] = _rms(x2, gf_ref[...])


def _post(o, gg, x2, mod_lat, wao, wo, g2, wup, fcw, fcb, wdn, gf):
    n = x2.shape[0]
    tps = SEQ // TM
    row = lambda s, b: (b * tps + s, 0)
    oprev, onext = _halo_specs(TM, ATTN_DIM, HALO_BF16)
    gprev, gnext = _halo_specs(TM, 2 * D_MODEL, HALO_BF16)
    xprev, xnext = _halo_specs(TM, D_MODEL)
    return pl.pallas_call(
        _post_kernel,
        name="post",
        grid=(tps, BATCH),
        in_specs=[
            pl.BlockSpec((TM, ATTN_DIM), row), oprev, onext,
            pl.BlockSpec((TM, 2 * D_MODEL), row), gprev, gnext,
            pl.BlockSpec((TM, D_MODEL), row), xprev, xnext,
            pl.BlockSpec((1, 6, D_MODEL), lambda s, b: (b, 0, 0)),
            _const_spec((ATTN_DIM, D_MODEL)),
            _const_spec((D_MODEL, D_MODEL)),
            _const_spec((1, D_MODEL)),
            _const_spec((D_MODEL, 2 * D_FF)),
            _const_spec((3, 2 * D_FF)),
            _const_spec((1, 2 * D_FF)),
            _const_spec((D_FF, D_MODEL)),
            _const_spec((1, D_MODEL)),
        ],
        out_specs=pl.BlockSpec((TM, D_MODEL), row),
        out_shape=jax.ShapeDtypeStruct((n, D_MODEL), F32),
        scratch_shapes=[pltpu.VMEM((TM, D_FF), BF16)],
        compiler_params=_params(2),
    )(o, o, o, gg, gg, gg, x2, x2, x2, mod_lat, wao, wo, g2, wup, fcw, fcb, wdn, gf)


def _rope_tables():
    f32 = np.float32
    rows = SEQ // GRID_W
    row = np.repeat(np.arange(rows), GRID_W).astype(f32)
    col = np.tile(np.arange(GRID_W), rows).astype(f32)
    axis_dim = QK_ROPE_DIM // 2
    inv = np.power(f32(ROPE_THETA), -np.arange(0, axis_dim, 2, dtype=f32) / f32(axis_dim)).astype(f32)
    ang = np.concatenate([row[:, None] * inv, col[:, None] * inv], axis=-1)
    j = np.arange(QK_ROPE_DIM)
    idx = (j // axis_dim) * ROPE_HALF + j % ROPE_HALF
    first_half = ((j % axis_dim) // ROPE_HALF) == 0
    cos = np.cos(ang).astype(f32)[:, idx]
    sin = np.sin(ang).astype(f32)[:, idx]
    s0 = np.where(first_half, -sin, f32(0))
    s1 = np.where(first_half, f32(0), sin)
    pad = np.zeros((SEQ, LANES - QK_NOPE_DIM - QK_ROPE_DIM), f32)

    def even(nope_val, rope_part):
        return np.concatenate([np.full((SEQ, QK_NOPE_DIM), nope_val, f32), rope_part, pad], axis=1)

    return jnp.asarray(np.stack([even(1.0, cos), even(0.0, s0), even(0.0, s1)]).astype(f32))


def _split_w_in_kernel(wt_ref, lat_ref, cx_ref, cb_ref, cc_ref, ga_ref, gc_ref):
    kr_end = KV_LORA_RANK + QK_ROPE_DIM
    krq = wt_ref[KV_LORA_RANK:KV_LORA_RANK + LANES, :].T
    lane = lax.broadcasted_iota(jnp.int32, krq.shape, 1)
    kr = jnp.where((lane >= QK_NOPE_DIM) & (lane < QK_NOPE_DIM + QK_ROPE_DIM),
                   pltpu.roll(krq, QK_NOPE_DIM, 1), 0.0)
    lat_ref[...] = jnp.concatenate([wt_ref[:KV_LORA_RANK, :].T, kr, wt_ref[kr_end:O_CX, :].T],
                                   axis=1).astype(BF16)
    for ref, (a, b) in ((cx_ref, (O_CX, O_CB)), (cb_ref, (O_CB, O_CC)), (cc_ref, (O_CC, O_GA)),
                        (ga_ref, (O_GA, O_GC)), (gc_ref, (O_GC, O_END))):
        ref[...] = wt_ref[a:b, :].T.astype(BF16)


def _prep_kernel(c_ref, wada_ref, bada_ref, wt_ref, mod_ref, *seg_refs):
    _ada_kernel(c_ref, wada_ref, bada_ref, mod_ref)
    _split_w_in_kernel(wt_ref, *seg_refs)


def _prep(cvec, w_ada, b_ada, wt):
    rows = cvec.shape[0]
    n_ada = w_ada.shape[1]
    steps = D_MODEL // LANES
    tn = n_ada // steps
    widths = (C_LAT, CONV_DIM, CONV_DIM, CONV_DIM, D_MODEL, D_MODEL)
    outs = pl.pallas_call(
        _prep_kernel,
        name="prep",
        grid=(steps,),
        in_specs=[pl.BlockSpec((rows, D_MODEL), lambda i: (0, 0)),
                  pl.BlockSpec((D_MODEL, tn), lambda i: (0, i)),
                  pl.BlockSpec((1, tn), lambda i: (0, i)),
                  pl.BlockSpec((O_END, LANES), lambda i: (0, i))],
        out_specs=[pl.BlockSpec((rows, tn), lambda i: (0, i))]
                  + [pl.BlockSpec((LANES, n), lambda i: (i, 0)) for n in widths],
        out_shape=[jax.ShapeDtypeStruct((rows, n_ada), F32)]
                  + [jax.ShapeDtypeStruct((D_MODEL, n), BF16) for n in widths],
        compiler_params=_params(1),
    )(cvec, w_ada, b_ada, wt)
    return outs[0], outs[1:]


def _pack_w_ukv(w):
    w = w.astype(BF16).reshape(KV_LORA_RANK, N_HEADS, QK_NOPE_DIM + V_HEAD_DIM)
    return (w[..., :QK_NOPE_DIM].reshape(KV_LORA_RANK, N_HEADS * QK_NOPE_DIM),
            w[..., QK_NOPE_DIM:].reshape(KV_LORA_RANK, ATTN_DIM).T)


def _pack_w_uq(w):
    qscale = (QK_NOPE_DIM + QK_ROPE_DIM) ** -0.5 * LOG2E
    w = (w * qscale).astype(BF16)
    pad = jnp.zeros((Q_LORA_RANK, LANES - QK_NOPE_DIM - QK_ROPE_DIM), BF16)
    pieces = []
    for h in range(N_HEADS):
        c0 = h * (QK_NOPE_DIM + QK_ROPE_DIM)
        nope, rope = w[:, c0:c0 + QK_NOPE_DIM], w[:, c0 + QK_NOPE_DIM:c0 + QK_NOPE_DIM + QK_ROPE_DIM]
        pieces += [nope, rope, pad] if h % 2 == 0 else [rope, pad, nope]
    return jnp.concatenate(pieces, axis=1)


def kernel(x, c, ctx, c_ctx, w_ada, b_ada, norm1_g, w_in, q_norm_g, kv_norm_g, w_uq, w_ukv,
           conv_w, conv_b, w_attn_out, w_conv_out, w_o, norm2_g, w_up, ffn_conv_w, ffn_conv_b,
           w_down, final_g):
    assert x.shape == (BATCH, SEQ, D_MODEL) and ctx.shape == (BATCH, CTX_LEN, D_MODEL)
    assert w_ada.shape[0] == 1, "single-layer block"

    cvec = jnp.concatenate([c, c_ctx[None, :], jnp.zeros((16 - BATCH - 1, D_MODEL), F32)], axis=0)
    mod, (w_lat, w_cx, w_cb, w_cc, w_ga, w_gc) = _prep(
        cvec, w_ada.reshape(D_MODEL, 6 * D_MODEL), b_ada.reshape(1, 6 * D_MODEL),
        w_in.reshape(D_MODEL, O_END).T)
    mod_lat = mod[:BATCH].reshape(BATCH, 6, D_MODEL)
    mod_ctx = mod[BATCH:BATCH + 1].reshape(1, 6, D_MODEL)

    tab = _rope_tables()
    w_kn, w_vt = _pack_w_ukv(w_ukv.reshape(KV_LORA_RANK, -1))
    w_q_p = _pack_w_uq(w_uq.reshape(Q_LORA_RANK, -1))
    g1 = norm1_g.reshape(1, D_MODEL)
    kvg = kv_norm_g.reshape(1, KV_LORA_RANK)
    qg = q_norm_g.reshape(1, Q_LORA_RANK)

    x2 = x.reshape(BATCH * SEQ, D_MODEL)
    c2 = ctx.reshape(BATCH * CTX_LEN, D_MODEL)

    kc, vct = _ctxproj(c2, mod_ctx, g1, w_lat, kvg, w_kn, w_vt)
    q, kl, vlt, gg = _inproj(x2, mod_lat, tab, g1, w_lat, w_cx, w_cb, w_cc, w_ga, w_gc, kvg, qg,
                             w_kn, w_vt, w_q_p,
                            conv_w.reshape(3, CONV_DIM), conv_b.reshape(1, CONV_DIM),
                            w_conv_out.reshape(CONV_DIM, D_MODEL).astype(BF16))
    o, (wao, wo, wup, wdn) = _attn(q, kl, vlt, kc, vct, (
        w_attn_out.reshape(ATTN_DIM, D_MODEL), w_o.reshape(D_MODEL, D_MODEL),
        w_up.reshape(D_MODEL, 2 * D_FF), w_down.reshape(D_FF, D_MODEL)))
    out = _post(o, gg, x2, mod_lat, wao, wo, norm2_g.reshape(1, D_MODEL), wup,
                ffn_conv_w.reshape(3, 2 * D_FF), ffn_conv_b.reshape(1, 2 * D_FF), wdn,
                final_g.reshape(1, D_MODEL))
    return out.reshape(BATCH, SEQ, D_MODEL)
```
